```python
import math
import jax, jax.numpy as jnp
from jax import lax
import numpy as np

D_MODEL = 1024
BATCH = 32
SEQ = 2048
DEPTH = 2
DEC_BATCH = 4
DEC_SEQ = 4096
PAST_LEN = 128

GRID_W = 64
N_MIXERS = 2
N_RET_LAYERS = (DEPTH + 1) // 2
N_NA_LAYERS = DEPTH // 2
RET_HEADS = 4
RET_QK_DIM = D_MODEL // RET_HEADS
RET_V_DIM = 2 * D_MODEL // RET_HEADS
RET_CHUNK = 128
ROPE_THETA = 10000.0
NA_HEADS = 16
NA_HEAD_DIM = D_MODEL // NA_HEADS
NA_MAX_ROWS = 8
NA_WIN_COLS = 16
NA_Q_COLS = 16
NA_BAND_COLS = 32
N_EXPERTS = 16
EC_CAPACITY = 2
D_FF_EXPERT = D_MODEL
PLE_DIM = 256
NORM_EPS = 1e-6

kernel_name = "hybrid_retention_natten_ec_encoder"

F32 = jnp.float32


def rms_norm(x, g):
    xf = x.astype(F32)
    y = xf * lax.rsqrt(jnp.mean(xf * xf, axis=-1, keepdims=True) + NORM_EPS)
    return (y * g.astype(F32)).astype(x.dtype)


def rope(x):
    S, d = x.shape[-2], x.shape[-1]
    inv = ROPE_THETA ** (-jnp.arange(0, d, 2, dtype=F32) / d)
    ang = jnp.arange(S, dtype=F32)[:, None] * inv[None, :]
    cos, sin = jnp.cos(ang), jnp.sin(ang)
    xf = x.astype(F32)
    x1, x2 = xf[..., : d // 2], xf[..., d // 2:]
    return jnp.concatenate([x1 * cos - x2 * sin, x1 * sin + x2 * cos], axis=-1).astype(x.dtype)


def causal_retention_chunks(q, k, v, log_g, strict):
    B, H, S, dk = q.shape
    dv = v.shape[-1]
    C = RET_CHUNK
    NC = S // C

    def chunks(t):
        return jnp.moveaxis(t.reshape(B, H, NC, C, t.shape[-1]), 2, 0)

    pos = jnp.arange(C, dtype=F32)
    diff = pos[:, None] - pos[None, :]
    mask = (diff > 0) if strict else (diff >= 0)
    decay = jnp.where(mask, jnp.exp(jnp.where(mask, diff, 0.0)[None] * log_g[:, None, None]), 0.0)
    q_decay = jnp.exp((pos + 1.0)[None, :] * log_g[:, None])[:, :, None]
    k_decay = jnp.exp((C - 1.0 - pos)[None, :] * log_g[:, None])[:, :, None]
    chunk_decay = jnp.exp(C * log_g)[:, None, None]

    def step(state, qkv):
        qc, kc, vc = qkv
        scores = jnp.einsum('bhid,bhjd->bhij', qc, kc) * decay
        inner = jnp.einsum('bhij,bhjv->bhiv', scores, vc)
        cross = jnp.einsum('bhid,bhdv->bhiv', qc * q_decay, state)
        new_state = state * chunk_decay + jnp.einsum('bhjd,bhjv->bhdv', kc * k_decay, vc)
        return new_state, inner + cross

    state0 = jnp.zeros((B, H, dk, dv), F32)
    _, out = lax.scan(step, state0, (chunks(q), chunks(k), chunks(v)))
    return jnp.moveaxis(out, 0, 2).reshape(B, H, S, dv)


def retention_mixer(h, w_in, decay_logit, gn_w, w_out):
    B, S, _ = h.shape
    proj = h @ w_in
    q, k, v, g = jnp.split(proj, [D_MODEL, 2 * D_MODEL, 4 * D_MODEL], axis=-1)

    def heads(t, d):
        return t.reshape(B, S, RET_HEADS, d).transpose(0, 2, 1, 3)

    q = rope(heads(q, RET_QK_DIM)).astype(F32)
    k = rope(heads(k, RET_QK_DIM)).astype(F32) * (RET_QK_DIM ** -0.5)
    v = heads(v, RET_V_DIM).astype(F32)
    log_g = jax.nn.log_sigmoid(decay_logit.astype(F32))
    fwd = causal_retention_chunks(q, k, v, log_g[0], False)

    def flip(t):
        return jnp.flip(t, axis=2)

    bwd = flip(causal_retention_chunks(flip(q), flip(k), flip(v), log_g[1], True))
    o = fwd + bwd
    mu = jnp.mean(o, axis=-1, keepdims=True)
    var = jnp.mean(jnp.square(o - mu), axis=-1, keepdims=True)
    o = (o - mu) * lax.rsqrt(var + NORM_EPS)
    o = o.transpose(0, 2, 1, 3).reshape(B, S, 2 * D_MODEL) * gn_w.astype(F32)
    out = (jax.nn.silu(g.astype(F32)) * o).astype(h.dtype)
    return out @ w_out


def neighbourhood_attention(h, w_in, rpb, w_out):
    B, S, _ = h.shape
    rows = S // GRID_W
    kr = min(NA_MAX_ROWS, rows)
    qkv = (h @ w_in).reshape(B, rows, GRID_W, 3, NA_HEADS, NA_HEAD_DIM).transpose(3, 0, 4, 1, 2, 5)
    q = qkv[0] * (NA_HEAD_DIM ** -0.5)
    k, v = qkv[1], qkv[2]

    n_cb = GRID_W // NA_Q_COLS
    c0 = np.arange(n_cb) * NA_Q_COLS
    band_start = np.clip(c0 - NA_WIN_COLS // 2, 0, GRID_W - NA_BAND_COLS)
    band_cols = band_start[:, None] + np.arange(NA_BAND_COLS)
    q_cols = c0[:, None] + np.arange(NA_Q_COLS)
    win_start = np.clip(q_cols - NA_WIN_COLS // 2, 0, GRID_W - NA_WIN_COLS)
    kcol = band_cols[:, None, :]
    col_valid = (kcol >= win_start[..., None]) & (kcol < win_start[..., None] + NA_WIN_COLS)
    dc_idx = np.clip(kcol - q_cols[..., None] + NA_WIN_COLS - 1, 0, 2 * NA_WIN_COLS - 2)
    mask_add = jnp.where(jnp.asarray(col_valid), 0.0, -1e30).astype(F32)[:, :, None, :]
    rpb_cols = rpb[:, :, dc_idx]

    def row_block(r):
        rs = jnp.clip(r - kr // 2, 0, rows - kr)
        k_rows = lax.dynamic_slice_in_dim(k, rs, kr, axis=2)[:, :, :, band_cols]
        v_rows = lax.dynamic_slice_in_dim(v, rs, kr, axis=2)[:, :, :, band_cols]
        q_r = lax.dynamic_index_in_dim(q, r, axis=2, keepdims=False).reshape(B, NA_HEADS, n_cb, NA_Q_COLS, NA_HEAD_DIM)
        dr_idx = rs + jnp.arange(kr) - r + NA_MAX_ROWS - 1
        bias = jnp.take(rpb_cols, dr_idx, axis=1).transpose(0, 2, 3, 1, 4).astype(F32) + mask_add
        s = jnp.einsum('bhnid,bhanjd->bhniaj', q_r, k_rows).astype(F32) + bias
        p = jax.nn.softmax(s.reshape(B, NA_HEADS, n_cb, NA_Q_COLS, kr * NA_BAND_COLS), axis=-1)
        p = p.reshape(s.shape).astype(v.dtype)
        o = jnp.einsum('bhniaj,bhanjd->bhnid', p, v_rows)
        return o.reshape(B, NA_HEADS, GRID_W, NA_HEAD_DIM)

    out = lax.map(row_block, jnp.arange(rows))
    out = out.transpose(1, 0, 3, 2, 4).reshape(B, S, D_MODEL)
    return out @ w_out


def expert_choice_moe(h, router_w, w_gate, w_up, w_down):
    B, S, D = h.shape
    n_tok = B * S
    cap = EC_CAPACITY * n_tok // N_EXPERTS
    xt = h.reshape(n_tok, D)
    aff = jax.nn.softmax((xt @ router_w).astype(F32), axis=-1)
    gates, idx = lax.top_k(aff.T, cap)
    xe = xt[idx]
    hid = jax.nn.silu(jnp.einsum('ecd,edf->ecf', xe, w_gate)) * jnp.einsum('ecd,edf->ecf', xe, w_up)
    ye = jnp.einsum('ecf,efd->ecd', hid, w_down) * gates[..., None].astype(h.dtype)
    y = jnp.zeros((n_tok, D), h.dtype).at[idx.reshape(-1)].add(ye.reshape(-1, D))
    return y.reshape(B, S, D)


def per_layer_embedding(h, p_i, norm_g, w_up, w_gate):
    gate = jax.nn.sigmoid((rms_norm(h, norm_g) @ w_gate).astype(F32))
    return ((p_i @ w_up).astype(F32) * gate).astype(h.dtype)


def trunk(x, p, ret_w_in, ret_decay_logit, ret_gn_w, ret_w_out, na_w_in, na_rpb, na_w_out,
          mix_norm, ffn_norm, ple_norm, router_w, exp_w_gate, exp_w_up, exp_w_down,
          ple_w_up, ple_w_gate, final_norm):
    h = x
    for i in range(DEPTH):
        hn = rms_norm(h, mix_norm[i])
        j = i // N_MIXERS
        if i % N_MIXERS == 0:
            h = h + retention_mixer(hn, ret_w_in[j], ret_decay_logit[j], ret_gn_w[j], ret_w_out[j])
        else:
            h = h + neighbourhood_attention(hn, na_w_in[j], na_rpb[j], na_w_out[j])
        h = h + expert_choice_moe(rms_norm(h, ffn_norm[i]), router_w[i], exp_w_gate[i], exp_w_up[i], exp_w_down[i])
        h = h + per_layer_embedding(h, p[i], ple_norm[i], ple_w_up[i], ple_w_gate[i])
    return rms_norm(h, final_norm)


def setup_inputs(seed: int = 0) -> dict:
    key = jax.random.key(seed)
    ks = jax.random.split(key, 24)

    def nrm(k, shape, scale):
        return jax.random.normal(k, shape, F32) * scale

    def gain(k, shape):
        return 1.0 + 0.05 * jax.random.normal(k, shape, F32)

    D = D_MODEL
    base_logit = jnp.asarray(np.log(2.0 ** (5 + np.arange(RET_HEADS)) - 1.0), F32)
    return {
        "x_prompt": nrm(ks[0], (BATCH, SEQ, D), 1.0),
        "x_sample": nrm(ks[1], (DEC_BATCH, DEC_SEQ, D), 1.0),
        "p_prompt": nrm(ks[2], (DEPTH, BATCH, SEQ, PLE_DIM), 1.0),
        "p_sample": nrm(ks[3], (DEPTH, DEC_BATCH, DEC_SEQ, PLE_DIM), 1.0),
        "ret_w_in": nrm(ks[4], (N_RET_LAYERS, D, 6 * D), D ** -0.5),
        "ret_decay_logit": base_logit[None, None, :] + nrm(ks[5], (N_RET_LAYERS, 2, RET_HEADS), 0.1),
        "ret_gn_w": gain(ks[6], (N_RET_LAYERS, 2 * D)),
        "ret_w_out": nrm(ks[7], (N_RET_LAYERS, 2 * D, D), (2 * D) ** -0.5),
        "na_w_in": nrm(ks[8], (N_NA_LAYERS, D, 3 * D), D ** -0.5),
        "na_rpb": nrm(ks[9], (N_NA_LAYERS, NA_HEADS, 2 * NA_MAX_ROWS - 1, 2 * NA_WIN_COLS - 1), 0.1),
        "na_w_out": nrm(ks[10], (N_NA_LAYERS, D, D), D ** -0.5),
        "mix_norm": gain(ks[11], (DEPTH, D)),
        "ffn_norm": gain(ks[12], (DEPTH, D)),
        "ple_norm": gain(ks[13], (DEPTH, D)),
        "router_w": nrm(ks[14], (DEPTH, D, N_EXPERTS), D ** -0.5),
        "exp_w_gate": nrm(ks[15], (DEPTH, N_EXPERTS, D, D_FF_EXPERT), D ** -0.5),
        "exp_w_up": nrm(ks[16], (DEPTH, N_EXPERTS, D, D_FF_EXPERT), D ** -0.5),
        "exp_w_down": nrm(ks[17], (DEPTH, N_EXPERTS, D_FF_EXPERT, D), D_FF_EXPERT ** -0.5),
        "ple_w_up": nrm(ks[18], (DEPTH, PLE_DIM, D), PLE_DIM ** -0.5),
        "ple_w_gate": nrm(ks[19], (DEPTH, D, D), D ** -0.5),
        "final_norm": gain(ks[20], (D,)),
    }


def reference(x_prompt, x_sample, p_prompt, p_sample, ret_w_in, ret_decay_logit, ret_gn_w, ret_w_out,
              na_w_in, na_rpb, na_w_out, mix_norm, ffn_norm, ple_norm, router_w, exp_w_gate, exp_w_up,
              exp_w_down, ple_w_up, ple_w_gate, final_norm):
    y_prompt = trunk(x_prompt, p_prompt, ret_w_in, ret_decay_logit, ret_gn_w, ret_w_out, na_w_in, na_rpb,
                     na_w_out, mix_norm, ffn_norm, ple_norm, router_w, exp_w_gate, exp_w_up, exp_w_down,
                     ple_w_up, ple_w_gate, final_norm)
    y_sample = trunk(x_sample, p_sample, ret_w_in, ret_decay_logit, ret_gn_w, ret_w_out, na_w_in, na_rpb,
                     na_w_out, mix_norm, ffn_norm, ple_norm, router_w, exp_w_gate, exp_w_up, exp_w_down,
                     ple_w_up, ple_w_gate, final_norm)
    return (y_prompt, y_sample)
```

```python
import functools

import numpy as np
import jax
import jax.numpy as jnp
from jax import lax
from jax.experimental import pallas as pl
from jax.experimental.pallas import tpu as pltpu

F32 = jnp.float32
BF16 = jnp.bfloat16
I32 = jnp.int32

NORM_EPS = 1e-6
ROPE_THETA = 10000.0
GRID_W = 64
RET_HEADS = 4
RET_CHUNK = 256
NA_HEADS = 16
NA_HEAD_DIM = 64
NA_MAX_ROWS = 8
NA_WIN_COLS = 16
N_EXPERTS = 16
EC_CAPACITY = 2
LANES = 128
SUBLANES = 8
MOE_TILE = 256
NEG_BIG = -1e30

_VMEM_LIMIT = 56 * 1024 * 1024


def _cparams(sem):
    return pltpu.CompilerParams(dimension_semantics=sem, vmem_limit_bytes=_VMEM_LIMIT)


def _rms(x, g):
    return x * lax.rsqrt(jnp.mean(x * x, axis=-1, keepdims=True) + NORM_EPS) * g


def _in_proj_kernel(x_ref, g_ref, w_ref, cos_ref, sin_ref, o_ref, xn_ref, *, rope_tiles, tile_scales, rope_dim):
    j = pl.program_id(1)

    @pl.when(j == 0)
    def _():
        xn_ref[...] = _rms(x_ref[...], g_ref[...]).astype(BF16)

    y = jnp.dot(xn_ref[...], w_ref[...], preferred_element_type=F32)
    scale = jnp.float32(1.0)
    for t, s in enumerate(tile_scales):
        if s != 1.0:
            scale = jnp.where(j == t, jnp.float32(s), scale)
    y = y * scale
    tn = y.shape[1]
    half = rope_dim // 2

    if rope_tiles > 0:
        @pl.when(j < rope_tiles)
        def _():
            cos = cos_ref[...]
            sin = sin_ref[...]
            for hh in range(tn // rope_dim):
                lo = hh * rope_dim
                x1 = y[:, lo:lo + half]
                x2 = y[:, lo + half:lo + rope_dim]
                o_ref[:, lo:lo + half] = (x1 * cos - x2 * sin).astype(o_ref.dtype)
                o_ref[:, lo + half:lo + rope_dim] = (x1 * sin + x2 * cos).astype(o_ref.dtype)

        @pl.when(j >= rope_tiles)
        def _():
            o_ref[...] = y.astype(o_ref.dtype)
    else:
        o_ref[...] = y.astype(o_ref.dtype)


def _in_proj(x, g, w_bf16, cos, sin, *, seq, rope_tiles, tile_scales, tm=512, tn=1024):
    n, d = x.shape
    ncol = w_bf16.shape[1]
    rope_dim = 2 * cos.shape[1]
    nseq = seq // tm
    kern = functools.partial(_in_proj_kernel, rope_tiles=rope_tiles, tile_scales=tile_scales, rope_dim=rope_dim)
    return pl.pallas_call(
        kern,
        grid=(n // tm, ncol // tn),
        in_specs=[
            pl.BlockSpec((tm, d), lambda i, j: (i, 0)),
            pl.BlockSpec((1, d), lambda i, j: (0, 0)),
            pl.BlockSpec((d, tn), lambda i, j: (0, j)),
            pl.BlockSpec((tm, cos.shape[1]), lambda i, j: (i % nseq, 0)),
            pl.BlockSpec((tm, cos.shape[1]), lambda i, j: (i % nseq, 0)),
        ],
        out_specs=pl.BlockSpec((tm, tn), lambda i, j: (i, j)),
        out_shape=jax.ShapeDtypeStruct((n, ncol), BF16),
        scratch_shapes=[pltpu.VMEM((tm, d), BF16)],
        compiler_params=_cparams(("parallel", "arbitrary")),
        name="in_proj",
    )(x, g.reshape(1, d), w_bf16, cos, sin)


def _retention_kernel(dl_ref, q_ref, k_ref, v_ref, g_ref, gnw_ref, o_ref, acc_ref, st_ref, *, seq, chunk):
    h = pl.program_id(1)
    nc = seq // chunk
    dl = dl_ref[...]
    lg = jnp.minimum(dl, 0.0) - jnp.log1p(jnp.exp(-jnp.abs(dl)))
    col = lax.broadcasted_iota(I32, dl.shape, 1)
    lgh = jnp.sum(jnp.where(col == h, lg, 0.0), axis=1, keepdims=True)
    lgf = lgh[0:1, :]
    lgb = lgh[1:2, :]

    ri = lax.broadcasted_iota(I32, (chunk, chunk), 0)
    ci = lax.broadcasted_iota(I32, (chunk, chunk), 1)
    diff = (ri - ci).astype(F32)
    dmat = jnp.exp(jnp.where(diff >= 0, diff * lgf, -diff * lgb))
    pos = lax.broadcasted_iota(I32, (chunk, 1), 0).astype(F32)
    qdec_f = jnp.exp((pos + 1.0) * lgf)
    kdec_f = jnp.exp((chunk - 1.0 - pos) * lgf)
    qdec_b = jnp.exp((chunk - pos) * lgb)
    kdec_b = jnp.exp(pos * lgb)
    cdec_f = jnp.exp(chunk * lgf)
    cdec_b = jnp.exp(chunk * lgb)
    nt = (((1,), (1,)), ((), ()))
    tn = (((0,), (0,)), ((), ()))

    st_ref[...] = jnp.zeros_like(st_ref)

    def fwd(c, carry):
        sl = pl.ds(pl.multiple_of(c * chunk, chunk), chunk)
        qc = q_ref[sl, :]
        kc = k_ref[sl, :]
        vc = v_ref[sl, :]
        s = lax.dot_general(qc, kc, nt, preferred_element_type=F32) * dmat
        inner = jnp.dot(s.astype(BF16), vc, preferred_element_type=F32)
        qd = (qc.astype(F32) * qdec_f).astype(BF16)
        cross = jnp.dot(qd, st_ref[...].astype(BF16), preferred_element_type=F32)
        acc_ref[sl, :] = inner + cross
        kd = (kc.astype(F32) * kdec_f).astype(BF16)
        st_ref[...] = st_ref[...] * cdec_f + lax.dot_general(kd, vc, tn, preferred_element_type=F32)
        return carry

    lax.fori_loop(0, nc, fwd, 0)
    st_ref[...] = jnp.zeros_like(st_ref)

    def bwd(t, carry):
        c = nc - 1 - t
        sl = pl.ds(pl.multiple_of(c * chunk, chunk), chunk)
        qc = q_ref[sl, :]
        kc = k_ref[sl, :]
        vc = v_ref[sl, :]
        qd = (qc.astype(F32) * qdec_b).astype(BF16)
        o = acc_ref[sl, :] + jnp.dot(qd, st_ref[...].astype(BF16), preferred_element_type=F32)
        kd = (kc.astype(F32) * kdec_b).astype(BF16)
        st_ref[...] = st_ref[...] * cdec_b + lax.dot_general(kd, vc, tn, preferred_element_type=F32)
        mu = jnp.mean(o, axis=-1, keepdims=True)
        oc = o - mu
        var = jnp.mean(oc * oc, axis=-1, keepdims=True)
        on = oc * lax.rsqrt(var + NORM_EPS) * gnw_ref[...]
        gate = g_ref[sl, :].astype(F32)
        o_ref[sl, :] = (gate * jax.nn.sigmoid(gate) * on).astype(o_ref.dtype)
        return carry

    lax.fori_loop(0, nc, bwd, 0)


def _retention(proj, decay_logit, gn_w, *, batch, seq, d_model):
    heads = RET_HEADS
    dk = d_model // heads
    dv = 2 * d_model // heads
    n = batch * seq
    kern = functools.partial(_retention_kernel, seq=seq, chunk=RET_CHUNK)
    return pl.pallas_call(
        kern,
        grid=(batch, heads),
        in_specs=[
            pl.BlockSpec((2, heads), lambda b, h: (0, 0)),
            pl.BlockSpec((seq, dk), lambda b, h: (b, h)),
            pl.BlockSpec((seq, dk), lambda b, h: (b, heads + h)),
            pl.BlockSpec((seq, dv), lambda b, h: (b, heads + h)),
            pl.BlockSpec((seq, dv), lambda b, h: (b, 2 * heads + h)),
            pl.BlockSpec((1, dv), lambda b, h: (0, h)),
        ],
        out_specs=pl.BlockSpec((seq, dv), lambda b, h: (b, h)),
        out_shape=jax.ShapeDtypeStruct((n, 2 * d_model), BF16),
        scratch_shapes=[pltpu.VMEM((seq, dv), F32), pltpu.VMEM((dk, dv), F32)],
        compiler_params=_cparams(("parallel", "parallel")),
        name="retention",
    )(decay_logit, proj, proj, proj, proj, gn_w.reshape(1, -1))


def _na_kernel(bias_ref, q_ref, k_ref, v_ref, o_ref, *, rows):
    gw = GRID_W
    kr = min(NA_MAX_ROWS, rows)
    dh = NA_HEAD_DIM
    lane_q = lax.broadcasted_iota(I32, (gw, 2 * dh), 1)
    nt = (((1,), (1,)), ((), ()))

    def body(r, carry):
        rs = jnp.clip(r - kr // 2, 0, rows - kr)
        d = r - rs
        q2 = q_ref[pl.ds(pl.multiple_of(r * gw, gw), gw), :]
        zero = jnp.zeros_like(q2)
        qq = jnp.concatenate([jnp.where(lane_q < dh, q2, zero), jnp.where(lane_q >= dh, q2, zero)], axis=0)
        ksl = pl.ds(pl.multiple_of(rs * gw, gw), kr * gw)
        k2 = k_ref[ksl, :]
        v2 = v_ref[ksl, :]
        s = lax.dot_general(qq, k2, nt, preferred_element_type=F32) + bias_ref[0, d]
        m = jnp.max(s, axis=-1, keepdims=True)
        p = jnp.exp(s - m)
        p = p / jnp.sum(p, axis=-1, keepdims=True)
        o = jnp.dot(p.astype(BF16), v2, preferred_element_type=F32)
        out = jnp.where(lane_q < dh, o[0:gw, :], o[gw:2 * gw, :])
        o_ref[pl.ds(pl.multiple_of(r * gw, gw), gw), :] = out.astype(o_ref.dtype)
        return carry

    lax.fori_loop(0, rows, body, 0)


def _na_bias_table(rpb, rows):
    kr = min(NA_MAX_ROWS, rows)
    c = np.arange(GRID_W)
    kc = np.arange(GRID_W)
    win_start = np.clip(c - NA_WIN_COLS // 2, 0, GRID_W - NA_WIN_COLS)
    valid = (kc[None, :] >= win_start[:, None]) & (kc[None, :] < win_start[:, None] + NA_WIN_COLS)
    dc_idx = np.clip(kc[None, :] - c[:, None] + NA_WIN_COLS - 1, 0, 2 * NA_WIN_COLS - 2)
    delta = np.arange(kr)
    a = np.arange(kr)
    dr_idx = a[None, :] - delta[:, None] + NA_MAX_ROWS - 1
    t = rpb[:, dr_idx][:, :, :, dc_idx]
    t = jnp.where(jnp.asarray(valid)[None, None, None], t.astype(F32), NEG_BIG)
    t = t.transpose(0, 1, 3, 2, 4)
    h = rpb.shape[0]
    t = t.reshape(h // 2, 2, kr, GRID_W, kr * GRID_W).transpose(0, 2, 1, 3, 4)
    return t.reshape(h // 2, kr, 2 * GRID_W, kr * GRID_W)


def _na_attention(qkv, bias, *, batch, seq, d_model):
    n = batch * seq
    rows = seq // GRID_W
    kr = min(NA_MAX_ROWS, rows)
    pairs = NA_HEADS // 2
    pw = 2 * NA_HEAD_DIM
    kern = functools.partial(_na_kernel, rows=rows)
    return pl.pallas_call(
        kern,
        grid=(batch, pairs),
        in_specs=[
            pl.BlockSpec((1, kr, 2 * GRID_W, kr * GRID_W), lambda b, hp: (hp, 0, 0, 0)),
            pl.BlockSpec((seq, pw), lambda b, hp: (b, hp)),
            pl.BlockSpec((seq, pw), lambda b, hp: (b, pairs + hp)),
            pl.BlockSpec((seq, pw), lambda b, hp: (b, 2 * pairs + hp)),
        ],
        out_specs=pl.BlockSpec((seq, pw), lambda b, hp: (b, hp)),
        out_shape=jax.ShapeDtypeStruct((n, d_model), BF16),
        compiler_params=_cparams(("parallel", "parallel")),
        name="na_attention",
    )(bias, qkv, qkv, qkv)


def _out_router_kernel(a_ref, w_ref, h_ref, g_ref, rw_ref, h1_ref, xn_ref, aff_ref):
    y = jnp.dot(a_ref[...], w_ref[...], preferred_element_type=F32) + h_ref[...]
    h1_ref[...] = y
    xn = _rms(y, g_ref[...])
    xn_ref[...] = xn
    nt = (((1,), (1,)), ((), ()))
    logits = lax.dot_general(rw_ref[...], xn, nt, precision=lax.Precision.HIGHEST,
                             preferred_element_type=F32)
    m = jnp.max(logits, axis=0, keepdims=True)
    e = jnp.exp(logits - m)
    aff = e / jnp.sum(e, axis=0, keepdims=True)
    for t in range(aff_ref.shape[0]):
        aff_ref[t] = aff[:, t * MOE_TILE:(t + 1) * MOE_TILE]


def _out_router(a, w_bf16, h, g, router_w, *, tm=512):
    n, kdim = a.shape
    d = h.shape[1]
    ne = router_w.shape[1]
    tpb = tm // MOE_TILE
    return pl.pallas_call(
        _out_router_kernel,
        grid=(n // tm,),
        in_specs=[
            pl.BlockSpec((tm, kdim), lambda i: (i, 0)),
            pl.BlockSpec((kdim, d), lambda i: (0, 0)),
            pl.BlockSpec((tm, d), lambda i: (i, 0)),
            pl.BlockSpec((1, d), lambda i: (0, 0)),
            pl.BlockSpec((ne, d), lambda i: (0, 0)),
        ],
        out_specs=[
            pl.BlockSpec((tm, d), lambda i: (i, 0)),
            pl.BlockSpec((tm, d), lambda i: (i, 0)),
            pl.BlockSpec((tpb, ne, MOE_TILE), lambda i: (i, 0, 0)),
        ],
        out_shape=[
            jax.ShapeDtypeStruct((n, d), F32),
            jax.ShapeDtypeStruct((n, d), F32),
            jax.ShapeDtypeStruct((n // MOE_TILE, ne, MOE_TILE), F32),
        ],
        compiler_params=_cparams(("parallel",)),
        name="out_router",
    )(a, w_bf16, h, g.reshape(1, d), router_w.T)


def _select_kernel(aff_ref, tok_ref, pair_ref, cnt_ref, pidx_ref, pblk_ref, *, cap):
    ntile, ne, tt = aff_ref.shape
    nsub = tt // LANES
    nblk = ntile * nsub

    def count(pred_fn):
        def body(c, acc):
            x = pltpu.bitcast(aff_ref[c], I32)
            return acc + pred_fn(x).astype(I32)
        acc = lax.fori_loop(0, ntile, body, jnp.zeros((ne, tt), I32))
        return jnp.sum(acc, axis=1, keepdims=True)

    def bit_step(i, t):
        cand = t | jnp.left_shift(jnp.int32(1), 30 - i)
        return jnp.where(count(lambda x: x >= cand) >= cap, cand, t)

    thr = lax.fori_loop(0, 31, bit_step, jnp.zeros((ne, 1), I32))
    need_eq = (cap - count(lambda x: x > thr)).astype(F32)

    li = lax.broadcasted_iota(I32, (LANES, LANES), 0)
    lj = lax.broadcasted_iota(I32, (LANES, LANES), 1)
    upper = (li < lj).astype(BF16)
    ei = lax.broadcasted_iota(I32, (ne, ne), 0)
    ej = lax.broadcasted_iota(I32, (ne, ne), 1)
    lower_e = (ej < ei).astype(BF16)
    slot = li.astype(F32)
    lane_row = lax.broadcasted_iota(I32, (8, LANES), 1).astype(F32)
    krow = lax.broadcasted_iota(I32, (8, LANES), 0)
    nt = (((1,), (1,)), ((), ()))

    def tile_body(c, carry):
        eq_carry, pair_carry = carry
        xt = pltpu.bitcast(aff_ref[c], I32)
        pid_parts = []
        for j in range(nsub):
            blk = c * nsub + j
            x = xt[:, j * LANES:(j + 1) * LANES]
            gt = x > thr
            eq = x == thr
            eqf = eq.astype(F32)
            eq_rank = eq_carry + jnp.dot(eqf.astype(BF16), upper, preferred_element_type=F32)
            sel = gt | (eq & (eq_rank < need_eq))
            eq_carry = eq_carry + jnp.sum(eqf, axis=1, keepdims=True)
            self_ = sel.astype(F32)
            selb = self_.astype(BF16)
            cexc = jnp.dot(selb, upper, preferred_element_type=F32)
            rank_e = jnp.dot(lower_e, selb, preferred_element_type=F32)
            cnt_tok = jnp.sum(self_, axis=0, keepdims=True)
            tok_pre = jnp.dot(jnp.broadcast_to(cnt_tok, (ne, LANES)).astype(BF16), upper,
                              preferred_element_type=F32)
            pair = pair_carry + tok_pre + rank_e
            pblk_ref[pl.ds(blk, 1), :] = jnp.broadcast_to(pair_carry, (1, LANES)).astype(I32)
            pair_carry = pair_carry + jnp.sum(cnt_tok, axis=1, keepdims=True)
            pair_i = pair.astype(I32)
            pid_parts.append(jnp.where(sel, pair_i, -1))
            cnt_ref[blk] = jnp.broadcast_to(jnp.sum(self_, axis=1, keepdims=True), (ne, LANES)).astype(I32)
            key = jnp.where(sel, cexc, -1.0)
            b0 = (pair_i & 255).astype(F32)
            b1 = ((pair_i >> 8) & 255).astype(F32)
            b2 = (pair_i >> 16).astype(F32)
            for e in range(ne):
                onehot = (jnp.broadcast_to(key[e:e + 1, :], (LANES, LANES)) == slot).astype(BF16)
                vals = jnp.where(krow == 0, lane_row,
                                 jnp.where(krow == 1, jnp.broadcast_to(b0[e:e + 1, :], (8, LANES)),
                                           jnp.where(krow == 2, jnp.broadcast_to(b1[e:e + 1, :], (8, LANES)),
                                                     jnp.where(krow == 3, jnp.broadcast_to(b2[e:e + 1, :], (8, LANES)),
                                                               0.0))))
                comp = lax.dot_general(vals.astype(BF16), onehot, nt, preferred_element_type=F32)
                row = e * nblk + blk
                tok_ref[pl.ds(row, 1), :] = comp[0:1, :].astype(I32) + blk * LANES
                pair_ref[pl.ds(row, 1), :] = (comp[1:2, :] + 256.0 * comp[2:3, :] + 65536.0 * comp[3:4, :]).astype(I32)
        pidx_ref[c] = jnp.concatenate(pid_parts, axis=1)
        return eq_carry, pair_carry

    lax.fori_loop(0, ntile, tile_body, (jnp.zeros((ne, 1), F32), jnp.zeros((1, 1), F32)))


def _select(aff3, *, cap):
    ntile, ne, tt = aff3.shape
    nblk = ntile * (tt // LANES)
    kern = functools.partial(_select_kernel, cap=cap)
    return pl.pallas_call(
        kern,
        out_shape=[
            jax.ShapeDtypeStruct((ne * nblk, LANES), I32),
            jax.ShapeDtypeStruct((ne * nblk, LANES), I32),
            jax.ShapeDtypeStruct((nblk, ne, LANES), I32),
            jax.ShapeDtypeStruct((ntile, ne, tt), I32),
            jax.ShapeDtypeStruct((nblk, LANES), I32),
        ],
        compiler_params=pltpu.CompilerParams(vmem_limit_bytes=_VMEM_LIMIT),
        name="ec_select",
    )(aff3)


def _permute_kernel(cnt_ref, idx_ref, src_ref, dst_ref, sem, ctr_ref, *, gather, nblk, gblk, cap):
    e = pl.program_id(0)
    g = pl.program_id(1)

    @pl.when(g == 0)
    def _():
        ctr_ref[0] = 0

    start = ctr_ref[0]

    def blk_body(b, ctr):
        c = cnt_ref[e * nblk + g * gblk + b]

        def row_body(r, ctr):
            idx = idx_ref[b, r]
            erow = e * cap + ctr
            if gather:
                src, dst = idx, erow
            else:
                src, dst = erow, idx
            pltpu.make_async_copy(src_ref.at[pl.ds(src, 1)], dst_ref.at[pl.ds(dst, 1)], sem).start()
            return ctr + 1

        return lax.fori_loop(0, c, row_body, ctr)

    end = lax.fori_loop(0, gblk, blk_body, start)
    ctr_ref[0] = end

    def wait_body(i, carry):
        pltpu.make_async_copy(src_ref.at[pl.ds(0, 1)], dst_ref.at[pl.ds(0, 1)], sem).wait()
        return carry

    lax.fori_loop(0, end - start, wait_body, 0)


def _permute_rows(cnt_flat, idx_lists, src, *, gather, cap, gblk=32):
    nrows_list = idx_lists.shape[0]
    ne = N_EXPERTS
    nblk = nrows_list // ne
    gblk = min(gblk, nblk)
    kern = functools.partial(_permute_kernel, gather=gather, nblk=nblk, gblk=gblk, cap=cap)
    nsteps = nblk // gblk
    return pl.pallas_call(
        kern,
        grid_spec=pltpu.PrefetchScalarGridSpec(
            num_scalar_prefetch=1,
            grid=(ne, nsteps),
            in_specs=[
                pl.BlockSpec((gblk, LANES), lambda e, g, cnt: (e * nsteps + g, 0), memory_space=pltpu.SMEM),
                pl.BlockSpec(memory_space=pl.ANY),
            ],
            out_specs=pl.BlockSpec(memory_space=pl.ANY),
            scratch_shapes=[pltpu.SemaphoreType.DMA(()), pltpu.SMEM((1,), I32)],
        ),
        out_shape=jax.ShapeDtypeStruct((ne * cap, src.shape[1]), src.dtype),
        compiler_params=pltpu.CompilerParams(dimension_semantics=("arbitrary", "arbitrary")),
        name="dispatch_rows" if gather else "unpermute_rows",
    )(cnt_flat, idx_lists, src)


def _ffn_kernel(x_ref, wg_ref, wu_ref, wd_ref, o_ref):
    x = x_ref[...].astype(BF16)
    a = jnp.dot(x, wg_ref[0], preferred_element_type=F32)
    u = jnp.dot(x, wu_ref[0], preferred_element_type=F32)
    hid = (a * jax.nn.sigmoid(a) * u).astype(BF16)
    o_ref[...] = jnp.dot(hid, wd_ref[0], preferred_element_type=F32)


def _expert_ffn(xe, wg, wu, wd, *, cap, tr=512):
    ne, d, f = wg.shape
    tr = min(tr, cap)
    nt = cap // tr
    return pl.pallas_call(
        _ffn_kernel,
        grid=(ne, nt),
        in_specs=[
            pl.BlockSpec((tr, d), lambda e, t: (e * nt + t, 0)),
            pl.BlockSpec((1, d, f), lambda e, t: (e, 0, 0)),
            pl.BlockSpec((1, d, f), lambda e, t: (e, 0, 0)),
            pl.BlockSpec((1, f, d), lambda e, t: (e, 0, 0)),
        ],
        out_specs=pl.BlockSpec((tr, d), lambda e, t: (e * nt + t, 0)),
        out_shape=jax.ShapeDtypeStruct((ne * cap, d), F32),
        compiler_params=_cparams(("parallel", "parallel")),
        name="expert_ffn",
    )(xe, wg, wu, wd)


def _combine_kernel(pblk_ref, h_ref, aff_ref, pidx_ref, p_ref, yp_ref, png_ref, wpg_ref, wpu_ref, fng_ref,
                    o_ref, buf_ref, acc_ref, sem, *, npairs, final_norm):
    t = pl.program_id(0)
    tt = h_ref.shape[0]
    win = buf_ref.shape[0]
    ch = win - SUBLANES
    nsub = tt // LANES
    p0 = pblk_ref[t * nsub]
    p1 = pblk_ref[(t + 1) * nsub]
    nch = (p1 - p0 + ch - 1) // ch
    ne = aff_ref.shape[1]
    aff = aff_ref[0]
    pid = pidx_ref[0]
    slot = lax.broadcasted_iota(I32, (win, tt), 0)
    acc_ref[...] = jnp.zeros_like(acc_ref)

    def chunk_body(c, carry):
        base = p0 + c * ch
        start = pl.multiple_of(jnp.minimum((base // SUBLANES) * SUBLANES, npairs - win), SUBLANES)
        cp = pltpu.make_async_copy(yp_ref.at[pl.ds(start, win)], buf_ref, sem)
        cp.start()
        cp.wait()
        a_t = jnp.zeros((win, tt), F32)
        for e in range(ne):
            pe = pid[e:e + 1, :]
            hit = (jnp.broadcast_to(pe - start, (win, tt)) == slot) & jnp.broadcast_to((pe >= base) & (pe < base + ch), (win, tt))
            a_t = a_t + jnp.where(hit, jnp.broadcast_to(aff[e:e + 1, :], (win, tt)), 0.0)
        acc_ref[...] += lax.dot_general(a_t.astype(BF16), buf_ref[...].astype(BF16), (((0,), (0,)), ((), ())),
                                        preferred_element_type=F32)
        return carry

    lax.fori_loop(0, nch, chunk_body, 0)
    h2 = h_ref[...] + acc_ref[...]
    gate = jax.nn.sigmoid(jnp.dot(_rms(h2, png_ref[...]).astype(BF16), wpg_ref[...], preferred_element_type=F32))
    up = jnp.dot(p_ref[...].astype(BF16), wpu_ref[...], preferred_element_type=F32)
    h3 = h2 + up * gate
    if final_norm:
        h3 = _rms(h3, fng_ref[...])
    o_ref[...] = h3


def _combine(pblk_flat, h1, aff3, pidx3, p_i, yp, ple_norm, wpg, wpu, final_g, *, final_norm, ch=256):
    n, d = h1.shape
    tt = aff3.shape[2]
    ne = aff3.shape[1]
    pd = p_i.shape[1]
    npairs = yp.shape[0]
    kern = functools.partial(_combine_kernel, npairs=npairs, final_norm=final_norm)
    return pl.pallas_call(
        kern,
        grid_spec=pltpu.PrefetchScalarGridSpec(
            num_scalar_prefetch=1,
            grid=(n // tt,),
            in_specs=[
                pl.BlockSpec((tt, d), lambda i, pb: (i, 0)),
                pl.BlockSpec((1, ne, tt), lambda i, pb: (i, 0, 0)),
                pl.BlockSpec((1, ne, tt), lambda i, pb: (i, 0, 0)),
                pl.BlockSpec((tt, pd), lambda i, pb: (i, 0)),
                pl.BlockSpec(memory_space=pl.ANY),
                pl.BlockSpec((1, d), lambda i, pb: (0, 0)),
                pl.BlockSpec((d, d), lambda i, pb: (0, 0)),
                pl.BlockSpec((pd, d), lambda i, pb: (0, 0)),
                pl.BlockSpec((1, d), lambda i, pb: (0, 0)),
            ],
            out_specs=pl.BlockSpec((tt, d), lambda i, pb: (i, 0)),
            scratch_shapes=[pltpu.VMEM((ch + SUBLANES, d), F32), pltpu.VMEM((tt, d), F32), pltpu.SemaphoreType.DMA(())],
        ),
        out_shape=jax.ShapeDtypeStruct((n, d), F32),
        compiler_params=_cparams(("arbitrary",)),
        name="combine_ple",
    )(pblk_flat, h1, aff3, pidx3, p_i, yp, ple_norm.reshape(1, d), wpg, wpu, final_g.reshape(1, d))


def _moe_block(h1, xn, aff3, p_i, w, i, *, final_norm):
    n, d = h1.shape
    cap = EC_CAPACITY * n // N_EXPERTS
    tok_l, pair_l, cnt3, pidx3, pblk2 = _select(aff3, cap=cap)
    nblk = cnt3.shape[0]
    cnt_flat = cnt3[:, :, 0].T.reshape(-1)
    pblk_flat = jnp.concatenate([pblk2[:, 0], jnp.full((1,), N_EXPERTS * cap, I32)])
    xe = _permute_rows(cnt_flat, tok_l, xn, gather=True, cap=cap)
    ye = _expert_ffn(xe, w["exp_w_gate"][i], w["exp_w_up"][i], w["exp_w_down"][i], cap=cap)
    yp = _permute_rows(cnt_flat, pair_l, ye, gather=False, cap=cap)
    return _combine(pblk_flat, h1, aff3, pidx3, p_i, yp, w["ple_norm"][i], w["ple_w_gate"][i], w["ple_w_up"][i],
                    w["final_norm"], final_norm=final_norm)


def _rope_tables(seq, dim):
    inv = ROPE_THETA ** (-jnp.arange(0, dim, 2, dtype=F32) / dim)
    ang = jnp.arange(seq, dtype=F32)[:, None] * inv[None, :]
    return jnp.cos(ang), jnp.sin(ang)


def _trunk(x, p, w):
    batch, seq, d = x.shape
    n = batch * seq
    h = x.reshape(n, d)
    dk = d // RET_HEADS
    cos, sin = _rope_tables(seq, dk)

    proj = _in_proj(h, w["mix_norm"][0], w["ret_w_in"][0], cos, sin, seq=seq, rope_tiles=2,
                    tile_scales=(1.0, float(dk) ** -0.5), tn=d)
    a = _retention(proj, w["ret_decay_logit"][0], w["ret_gn_w"][0], batch=batch, seq=seq, d_model=d)
    h1, xn, aff3 = _out_router(a, w["ret_w_out"][0], h, w["ffn_norm"][0], w["router_w"][0])
    h = _moe_block(h1, xn, aff3, p[0].reshape(n, -1), w, 0, final_norm=False)

    qkv = _in_proj(h, w["mix_norm"][1], w["na_w_in"][0], cos, sin, seq=seq, rope_tiles=0,
                   tile_scales=(float(NA_HEAD_DIM) ** -0.5,), tn=d)
    bias = _na_bias_table(w["na_rpb"][0], seq // GRID_W)
    a = _na_attention(qkv, bias, batch=batch, seq=seq, d_model=d)
    h1, xn, aff3 = _out_router(a, w["na_w_out"][0], h, w["ffn_norm"][1], w["router_w"][1])
    y = _moe_block(h1, xn, aff3, p[1].reshape(n, -1), w, 1, final_norm=True)
    return y.reshape(batch, seq, d)


def kernel(x_prompt, x_sample, p_prompt, p_sample, ret_w_in, ret_decay_logit, ret_gn_w, ret_w_out, na_w_in, na_rpb, na_w_out, mix_norm, ffn_norm, ple_norm, router_w, exp_w_gate, exp_w_up, exp_w_down, ple_w_up, ple_w_gate, final_norm):
    w = dict(
        ret_w_in=ret_w_in.astype(BF16), ret_decay_logit=ret_decay_logit, ret_gn_w=ret_gn_w,
        ret_w_out=ret_w_out.astype(BF16), na_w_in=na_w_in.astype(BF16), na_rpb=na_rpb,
        na_w_out=na_w_out.astype(BF16), mix_norm=mix_norm, ffn_norm=ffn_norm, ple_norm=ple_norm,
        router_w=router_w, exp_w_gate=exp_w_gate.astype(BF16), exp_w_up=exp_w_up.astype(BF16),
        exp_w_down=exp_w_down.astype(BF16), ple_w_up=ple_w_up.astype(BF16), ple_w_gate=ple_w_gate.astype(BF16),
        final_norm=final_norm,
    )
    return _trunk(x_prompt, p_prompt, w), _trunk(x_sample, p_sample, w)
```

```python
import functools

import numpy as np
import jax
import jax.numpy as jnp
from jax import lax
from jax.experimental import pallas as pl
from jax.experimental.pallas import tpu as pltpu

F32 = jnp.float32
BF16 = jnp.bfloat16
I32 = jnp.int32

NORM_EPS = 1e-6
ROPE_THETA = 10000.0
GRID_W = 64
RET_HEADS = 4
RET_CHUNK = 256
NA_HEADS = 16
NA_HEAD_DIM = 64
NA_MAX_ROWS = 8
NA_WIN_COLS = 16
N_EXPERTS = 16
EC_CAPACITY = 2
LANES = 128
BF16_ROWS = 16
MOE_TILE = 256
MOE_SLOTS = 64
NEG_BIG = -1e30

_VMEM_LIMIT = 56 * 1024 * 1024


def _cparams(sem):
    return pltpu.CompilerParams(dimension_semantics=sem, vmem_limit_bytes=_VMEM_LIMIT)


def _rms(x, g):
    return x * lax.rsqrt(jnp.mean(x * x, axis=-1, keepdims=True) + NORM_EPS) * g


def _in_proj_kernel(x_ref, g_ref, w_ref, cos_ref, sin_ref, o_ref, xn_ref, *, rope_tiles, tile_scales, rope_dim):
    j = pl.program_id(1)

    @pl.when(j == 0)
    def _():
        xn_ref[...] = _rms(x_ref[...], g_ref[...]).astype(BF16)

    y = jnp.dot(xn_ref[...], w_ref[...], preferred_element_type=F32)
    scale = jnp.float32(1.0)
    for t, s in enumerate(tile_scales):
        if s != 1.0:
            scale = jnp.where(j == t, jnp.float32(s), scale)
    y = y * scale
    tn = y.shape[1]
    half = rope_dim // 2

    if rope_tiles > 0:
        @pl.when(j < rope_tiles)
        def _():
            cos = cos_ref[...]
            sin = sin_ref[...]
            for hh in range(tn // rope_dim):
                lo = hh * rope_dim
                x1 = y[:, lo:lo + half]
                x2 = y[:, lo + half:lo + rope_dim]
                o_ref[:, lo:lo + half] = (x1 * cos - x2 * sin).astype(o_ref.dtype)
                o_ref[:, lo + half:lo + rope_dim] = (x1 * sin + x2 * cos).astype(o_ref.dtype)

        @pl.when(j >= rope_tiles)
        def _():
            o_ref[...] = y.astype(o_ref.dtype)
    else:
        o_ref[...] = y.astype(o_ref.dtype)


def _in_proj(x, g, w_bf16, cos, sin, *, seq, rope_tiles, tile_scales, tm=512, tn=1024):
    n, d = x.shape
    ncol = w_bf16.shape[1]
    rope_dim = 2 * cos.shape[1]
    nseq = seq // tm
    kern = functools.partial(_in_proj_kernel, rope_tiles=rope_tiles, tile_scales=tile_scales, rope_dim=rope_dim)
    return pl.pallas_call(
        kern,
        grid=(n // tm, ncol // tn),
        in_specs=[
            pl.BlockSpec((tm, d), lambda i, j: (i, 0)),
            pl.BlockSpec((1, d), lambda i, j: (0, 0)),
            pl.BlockSpec((d, tn), lambda i, j: (0, j)),
            pl.BlockSpec((tm, cos.shape[1]), lambda i, j: (i % nseq, 0)),
            pl.BlockSpec((tm, cos.shape[1]), lambda i, j: (i % nseq, 0)),
        ],
        out_specs=pl.BlockSpec((tm, tn), lambda i, j: (i, j)),
        out_shape=jax.ShapeDtypeStruct((n, ncol), BF16),
        scratch_shapes=[pltpu.VMEM((tm, d), BF16)],
        compiler_params=_cparams(("parallel", "arbitrary")),
        name="in_proj",
    )(x, g.reshape(1, d), w_bf16, cos, sin)


def _retention_kernel(dl_ref, q_ref, k_ref, v_ref, g_ref, gnw_ref, o_ref, acc_ref, st_ref, *, seq, chunk):
    h = pl.program_id(1)
    nc = seq // chunk
    dl = dl_ref[...]
    lg = jnp.minimum(dl, 0.0) - jnp.log1p(jnp.exp(-jnp.abs(dl)))
    col = lax.broadcasted_iota(I32, dl.shape, 1)
    lgh = jnp.sum(jnp.where(col == h, lg, 0.0), axis=1, keepdims=True)
    lgf = lgh[0:1, :]
    lgb = lgh[1:2, :]

    ri = lax.broadcasted_iota(I32, (chunk, chunk), 0)
    ci = lax.broadcasted_iota(I32, (chunk, chunk), 1)
    diff = (ri - ci).astype(F32)
    dmat = jnp.exp(jnp.where(diff >= 0, diff * lgf, -diff * lgb))
    pos = lax.broadcasted_iota(I32, (chunk, 1), 0).astype(F32)
    qdec_f = jnp.exp((pos + 1.0) * lgf)
    kdec_f = jnp.exp((chunk - 1.0 - pos) * lgf)
    qdec_b = jnp.exp((chunk - pos) * lgb)
    kdec_b = jnp.exp(pos * lgb)
    cdec_f = jnp.exp(chunk * lgf)
    cdec_b = jnp.exp(chunk * lgb)
    nt = (((1,), (1,)), ((), ()))
    tn = (((0,), (0,)), ((), ()))

    st_ref[...] = jnp.zeros_like(st_ref)

    def fwd(c, carry):
        sl = pl.ds(pl.multiple_of(c * chunk, chunk), chunk)
        qc = q_ref[sl, :]
        kc = k_ref[sl, :]
        vc = v_ref[sl, :]
        s = lax.dot_general(qc, kc, nt, preferred_element_type=F32) * dmat
        inner = jnp.dot(s.astype(BF16), vc, preferred_element_type=F32)
        qd = (qc.astype(F32) * qdec_f).astype(BF16)
        cross = jnp.dot(qd, st_ref[...].astype(BF16), preferred_element_type=F32)
        acc_ref[sl, :] = inner + cross
        kd = (kc.astype(F32) * kdec_f).astype(BF16)
        st_ref[...] = st_ref[...] * cdec_f + lax.dot_general(kd, vc, tn, preferred_element_type=F32)
        return carry

    lax.fori_loop(0, nc, fwd, 0)
    st_ref[...] = jnp.zeros_like(st_ref)

    def bwd(t, carry):
        c = nc - 1 - t
        sl = pl.ds(pl.multiple_of(c * chunk, chunk), chunk)
        qc = q_ref[sl, :]
        kc = k_ref[sl, :]
        vc = v_ref[sl, :]
        qd = (qc.astype(F32) * qdec_b).astype(BF16)
        o = acc_ref[sl, :] + jnp.dot(qd, st_ref[...].astype(BF16), preferred_element_type=F32)
        kd = (kc.astype(F32) * kdec_b).astype(BF16)
        st_ref[...] = st_ref[...] * cdec_b + lax.dot_general(kd, vc, tn, preferred_element_type=F32)
        mu = jnp.mean(o, axis=-1, keepdims=True)
        oc = o - mu
        var = jnp.mean(oc * oc, axis=-1, keepdims=True)
        on = oc * lax.rsqrt(var + NORM_EPS) * gnw_ref[...]
        gate = g_ref[sl, :].astype(F32)
        o_ref[sl, :] = (gate * jax.nn.sigmoid(gate) * on).astype(o_ref.dtype)
        return carry

    lax.fori_loop(0, nc, bwd, 0)


def _retention(proj, decay_logit, gn_w, *, batch, seq, d_model):
    heads = RET_HEADS
    dk = d_model // heads
    dv = 2 * d_model // heads
    n = batch * seq
    kern = functools.partial(_retention_kernel, seq=seq, chunk=RET_CHUNK)
    return pl.pallas_call(
        kern,
        grid=(batch, heads),
        in_specs=[
            pl.BlockSpec((2, heads), lambda b, h: (0, 0)),
            pl.BlockSpec((seq, dk), lambda b, h: (b, h)),
            pl.BlockSpec((seq, dk), lambda b, h: (b, heads + h)),
            pl.BlockSpec((seq, dv), lambda b, h: (b, heads + h)),
            pl.BlockSpec((seq, dv), lambda b, h: (b, 2 * heads + h)),
            pl.BlockSpec((1, dv), lambda b, h: (0, h)),
        ],
        out_specs=pl.BlockSpec((seq, dv), lambda b, h: (b, h)),
        out_shape=jax.ShapeDtypeStruct((n, 2 * d_model), BF16),
        scratch_shapes=[pltpu.VMEM((seq, dv), F32), pltpu.VMEM((dk, dv), F32)],
        compiler_params=_cparams(("parallel", "parallel")),
        name="retention",
    )(decay_logit, proj, proj, proj, proj, gn_w.reshape(1, -1))


def _na_kernel(bias_ref, q_ref, k_ref, v_ref, o_ref, *, rows):
    gw = GRID_W
    kr = min(NA_MAX_ROWS, rows)
    dh = NA_HEAD_DIM
    lane_q = lax.broadcasted_iota(I32, (gw, 2 * dh), 1)
    nt = (((1,), (1,)), ((), ()))

    def body(r, carry):
        rs = jnp.clip(r - kr // 2, 0, rows - kr)
        d = r - rs
        q2 = q_ref[pl.ds(pl.multiple_of(r * gw, gw), gw), :]
        zero = jnp.zeros_like(q2)
        qq = jnp.concatenate([jnp.where(lane_q < dh, q2, zero), jnp.where(lane_q >= dh, q2, zero)], axis=0)
        ksl = pl.ds(pl.multiple_of(rs * gw, gw), kr * gw)
        k2 = k_ref[ksl, :]
        v2 = v_ref[ksl, :]
        s = lax.dot_general(qq, k2, nt, preferred_element_type=F32) + bias_ref[0, d]
        m = jnp.max(s, axis=-1, keepdims=True)
        p = jnp.exp(s - m)
        p = p / jnp.sum(p, axis=-1, keepdims=True)
        o = jnp.dot(p.astype(BF16), v2, preferred_element_type=F32)
        out = jnp.where(lane_q < dh, o[0:gw, :], o[gw:2 * gw, :])
        o_ref[pl.ds(pl.multiple_of(r * gw, gw), gw), :] = out.astype(o_ref.dtype)
        return carry

    lax.fori_loop(0, rows, body, 0)


def _na_bias_table(rpb, rows):
    kr = min(NA_MAX_ROWS, rows)
    c = np.arange(GRID_W)
    kc = np.arange(GRID_W)
    win_start = np.clip(c - NA_WIN_COLS // 2, 0, GRID_W - NA_WIN_COLS)
    valid = (kc[None, :] >= win_start[:, None]) & (kc[None, :] < win_start[:, None] + NA_WIN_COLS)
    dc_idx = np.clip(kc[None, :] - c[:, None] + NA_WIN_COLS - 1, 0, 2 * NA_WIN_COLS - 2)
    delta = np.arange(kr)
    a = np.arange(kr)
    dr_idx = a[None, :] - delta[:, None] + NA_MAX_ROWS - 1
    t = rpb[:, dr_idx][:, :, :, dc_idx]
    t = jnp.where(jnp.asarray(valid)[None, None, None], t.astype(F32), NEG_BIG)
    t = t.transpose(0, 1, 3, 2, 4)
    h = rpb.shape[0]
    t = t.reshape(h // 2, 2, kr, GRID_W, kr * GRID_W).transpose(0, 2, 1, 3, 4)
    return t.reshape(h // 2, kr, 2 * GRID_W, kr * GRID_W)


def _na_attention(qkv, bias, *, batch, seq, d_model):
    n = batch * seq
    rows = seq // GRID_W
    kr = min(NA_MAX_ROWS, rows)
    pairs = NA_HEADS // 2
    pw = 2 * NA_HEAD_DIM
    kern = functools.partial(_na_kernel, rows=rows)
    return pl.pallas_call(
        kern,
        grid=(batch, pairs),
        in_specs=[
            pl.BlockSpec((1, kr, 2 * GRID_W, kr * GRID_W), lambda b, hp: (hp, 0, 0, 0)),
            pl.BlockSpec((seq, pw), lambda b, hp: (b, hp)),
            pl.BlockSpec((seq, pw), lambda b, hp: (b, pairs + hp)),
            pl.BlockSpec((seq, pw), lambda b, hp: (b, 2 * pairs + hp)),
        ],
        out_specs=pl.BlockSpec((seq, pw), lambda b, hp: (b, hp)),
        out_shape=jax.ShapeDtypeStruct((n, d_model), BF16),
        compiler_params=_cparams(("parallel", "parallel")),
        name="na_attention",
    )(bias, qkv, qkv, qkv)


def _out_router_kernel(a_ref, w_ref, h_ref, g_ref, rw_ref, h1_ref, xn_ref, aff_ref):
    y = jnp.dot(a_ref[...], w_ref[...], preferred_element_type=F32) + h_ref[...]
    h1_ref[...] = y
    xn = _rms(y, g_ref[...])
    xn_ref[...] = xn.astype(xn_ref.dtype)
    nt = (((1,), (1,)), ((), ()))
    logits = lax.dot_general(rw_ref[...], xn, nt, precision=lax.Precision.HIGHEST,
                             preferred_element_type=F32)
    m = jnp.max(logits, axis=0, keepdims=True)
    e = jnp.exp(logits - m)
    aff = e / jnp.sum(e, axis=0, keepdims=True)
    for t in range(aff_ref.shape[0]):
        aff_ref[t] = aff[:, t * MOE_TILE:(t + 1) * MOE_TILE]


def _out_router(a, w_bf16, h, g, router_w, *, tm=512):
    n, kdim = a.shape
    d = h.shape[1]
    ne = router_w.shape[1]
    tpb = tm // MOE_TILE
    return pl.pallas_call(
        _out_router_kernel,
        grid=(n // tm,),
        in_specs=[
            pl.BlockSpec((tm, kdim), lambda i: (i, 0)),
            pl.BlockSpec((kdim, d), lambda i: (0, 0)),
            pl.BlockSpec((tm, d), lambda i: (i, 0)),
            pl.BlockSpec((1, d), lambda i: (0, 0)),
            pl.BlockSpec((ne, d), lambda i: (0, 0)),
        ],
        out_specs=[
            pl.BlockSpec((tm, d), lambda i: (i, 0)),
            pl.BlockSpec((tm, d), lambda i: (i, 0)),
            pl.BlockSpec((tpb, ne, MOE_TILE), lambda i: (i, 0, 0)),
        ],
        out_shape=[
            jax.ShapeDtypeStruct((n, d), F32),
            jax.ShapeDtypeStruct((n, d), BF16),
            jax.ShapeDtypeStruct((n // MOE_TILE, ne, MOE_TILE), F32),
        ],
        compiler_params=_cparams(("parallel",)),
        name="out_router",
    )(a, w_bf16, h, g.reshape(1, d), router_w.T)


def _select_kernel(aff_ref, pos_ref, off_ref, *, cap):
    ntile, ne, tt = aff_ref.shape
    nsub = tt // LANES

    def count(pred_fn):
        def body(c, acc):
            x = pltpu.bitcast(aff_ref[c], I32)
            return acc + pred_fn(x).astype(I32)
        acc = lax.fori_loop(0, ntile, body, jnp.zeros((ne, tt), I32))
        return jnp.sum(acc, axis=1, keepdims=True)

    def bit_step(i, t):
        cand = t | jnp.left_shift(jnp.int32(1), 30 - i)
        return jnp.where(count(lambda x: x >= cand) >= cap, cand, t)

    thr = lax.fori_loop(0, 31, bit_step, jnp.zeros((ne, 1), I32))
    need_eq = (cap - count(lambda x: x > thr)).astype(F32)

    li = lax.broadcasted_iota(I32, (LANES, LANES), 0)
    lj = lax.broadcasted_iota(I32, (LANES, LANES), 1)
    upper = (li < lj).astype(BF16)

    def tile_body(c, carry):
        eq_carry, pos_carry = carry
        off_ref[c] = jnp.broadcast_to(pos_carry, (ne, LANES)).astype(I32)
        xt = pltpu.bitcast(aff_ref[c], I32)
        parts = []
        for j in range(nsub):
            x = xt[:, j * LANES:(j + 1) * LANES]
            gt = x > thr
            eq = x == thr
            eqf = eq.astype(F32)
            eq_rank = eq_carry + jnp.dot(eqf.astype(BF16), upper, preferred_element_type=F32)
            sel = gt | (eq & (eq_rank < need_eq))
            eq_carry = eq_carry + jnp.sum(eqf, axis=1, keepdims=True)
            self_ = sel.astype(F32)
            cexc = jnp.dot(self_.astype(BF16), upper, preferred_element_type=F32)
            parts.append(jnp.where(sel, (pos_carry + cexc).astype(I32), -1))
            pos_carry = pos_carry + jnp.sum(self_, axis=1, keepdims=True)
        pos_ref[c] = jnp.concatenate(parts, axis=1)
        return eq_carry, pos_carry

    lax.fori_loop(0, ntile, tile_body, (jnp.zeros((ne, 1), F32), jnp.zeros((ne, 1), F32)))


def _select(aff3, *, cap):
    ntile, ne, tt = aff3.shape
    kern = functools.partial(_select_kernel, cap=cap)
    pos3, off3 = pl.pallas_call(
        kern,
        out_shape=[
            jax.ShapeDtypeStruct((ntile, ne, tt), I32),
            jax.ShapeDtypeStruct((ntile, ne, LANES), I32),
        ],
        compiler_params=pltpu.CompilerParams(vmem_limit_bytes=_VMEM_LIMIT),
        name="ec_select",
    )(aff3)
    offs = jnp.concatenate([off3[:, :, 0], jnp.full((1, ne), cap, I32)], axis=0).reshape(-1)
    return pos3, offs


def _tile_windows(off_ref, s, ne):
    starts, used = [], []
    for e in range(ne):
        off = off_ref[s * ne + e]
        nxt = off_ref[(s + 1) * ne + e]
        st = (off // BF16_ROWS) * BF16_ROWS
        starts.append(st)
        used.append(nxt - st)
    return starts, used


def _num_rounds(used):
    m = used[0]
    for u in used[1:]:
        m = jnp.maximum(m, u)
    return (m + MOE_SLOTS - 1) // MOE_SLOTS


def _dispatch_kernel(off_ref, x_ref, pos_ref, xe_ref, wins_ref, carry_ref, sem, nout_ref, *, cap):
    s = pl.program_id(0)
    ne = pos_ref.shape[1]
    tt = x_ref.shape[0]
    gr = BF16_ROWS
    gpr = MOE_SLOTS // gr

    @pl.when(s == 0)
    def _():
        carry_ref[...] = jnp.zeros_like(carry_ref)
        nout_ref[0] = 0
        nout_ref[1] = 0

    starts, used = _tile_windows(off_ref, s, ne)
    pos = pos_ref[0]
    x = x_ref[...]
    slot = lax.broadcasted_iota(I32, (MOE_SLOTS, tt), 0)

    def drain(b):
        def wait_one(i, c):
            pltpu.make_async_copy(wins_ref.at[b, pl.ds(0, gr)], xe_ref.at[pl.ds(0, gr)], sem.at[b]).wait()
            return c
        lax.fori_loop(0, nout_ref[b], wait_one, 0)
        nout_ref[b] = 0

    def round_body(w, c):
        b = (s + w) % 2
        drain(b)
        pieces = []
        for e in range(ne):
            key = pos[e:e + 1, :] - (starts[e] + w * MOE_SLOTS)
            pieces.append((jnp.broadcast_to(key, (MOE_SLOTS, tt)) == slot).astype(F32))
        onehot = jnp.concatenate(pieces, axis=0).astype(BF16)
        wins_ref[b] = jnp.dot(onehot, x, preferred_element_type=F32).astype(BF16)

        @pl.when(w == 0)
        def _():
            for e in range(ne):
                rows = pl.ds(e * MOE_SLOTS, gr)
                wins_ref[b, rows, :] = (wins_ref[b, rows, :].astype(F32)
                                        + carry_ref[pl.ds(e * gr, gr), :].astype(F32)).astype(BF16)

        issued = 0
        for e in range(ne):
            ngc = used[e] // gr
            rem = used[e] % gr
            ng_w = jnp.clip(ngc - w * gpr, 0, gpr)

            def issue(j, c2, e=e):
                src = wins_ref.at[b, pl.ds(e * MOE_SLOTS + j * gr, gr)]
                dst = xe_ref.at[pl.ds(pl.multiple_of(e * cap + starts[e] + w * MOE_SLOTS + j * gr, gr), gr)]
                pltpu.make_async_copy(src, dst, sem.at[b]).start()
                return c2

            lax.fori_loop(0, ng_w, issue, 0)
            issued = issued + ng_w

            @pl.when((rem > 0) & (ngc // gpr == w))
            def _(e=e, ngc=ngc):
                carry_ref[pl.ds(e * gr, gr), :] = wins_ref[b, pl.ds(e * MOE_SLOTS + (ngc % gpr) * gr, gr), :]

            @pl.when((rem == 0) & (w == 0))
            def _(e=e):
                carry_ref[pl.ds(e * gr, gr), :] = jnp.zeros((gr, carry_ref.shape[1]), carry_ref.dtype)

        nout_ref[b] = issued
        return c

    lax.fori_loop(0, _num_rounds(used), round_body, 0)

    @pl.when(s == pl.num_programs(0) - 1)
    def _():
        drain(0)
        drain(1)


def _dispatch(offs, xn, pos3, *, cap):
    n, d = xn.shape
    ntile, ne, tt = pos3.shape
    kern = functools.partial(_dispatch_kernel, cap=cap)
    return pl.pallas_call(
        kern,
        grid_spec=pltpu.PrefetchScalarGridSpec(
            num_scalar_prefetch=1,
            grid=(ntile,),
            in_specs=[
                pl.BlockSpec((tt, d), lambda i, off: (i, 0)),
                pl.BlockSpec((1, ne, tt), lambda i, off: (i, 0, 0)),
            ],
            out_specs=pl.BlockSpec(memory_space=pl.ANY),
            scratch_shapes=[
                pltpu.VMEM((2, ne * MOE_SLOTS, d), BF16),
                pltpu.VMEM((ne * BF16_ROWS, d), BF16),
                pltpu.SemaphoreType.DMA((2,)),
                pltpu.SMEM((2,), I32),
            ],
        ),
        out_shape=jax.ShapeDtypeStruct((ne * cap, d), BF16),
        compiler_params=_cparams(("arbitrary",)),
        name="ec_dispatch",
    )(offs, xn, pos3)


def _ffn_kernel(x_ref, wg_ref, wu_ref, wd_ref, o_ref):
    x = x_ref[...]
    a = jnp.dot(x, wg_ref[0], preferred_element_type=F32)
    u = jnp.dot(x, wu_ref[0], preferred_element_type=F32)
    hid = (a * jax.nn.sigmoid(a) * u).astype(BF16)
    o_ref[...] = jnp.dot(hid, wd_ref[0], preferred_element_type=F32).astype(o_ref.dtype)


def _expert_ffn(xe, wg, wu, wd, *, cap, tr=512):
    ne, d, f = wg.shape
    tr = min(tr, cap)
    nt = cap // tr
    return pl.pallas_call(
        _ffn_kernel,
        grid=(ne, nt),
        in_specs=[
            pl.BlockSpec((tr, d), lambda e, t: (e * nt + t, 0)),
            pl.BlockSpec((1, d, f), lambda e, t: (e, 0, 0)),
            pl.BlockSpec((1, d, f), lambda e, t: (e, 0, 0)),
            pl.BlockSpec((1, f, d), lambda e, t: (e, 0, 0)),
        ],
        out_specs=pl.BlockSpec((tr, d), lambda e, t: (e * nt + t, 0)),
        out_shape=jax.ShapeDtypeStruct((ne * cap, d), BF16),
        compiler_params=_cparams(("parallel", "parallel")),
        name="expert_ffn",
    )(xe, wg, wu, wd)


def _combine_kernel(off_ref, h_ref, aff_ref, pos_ref, p_ref, ye_ref, png_ref, wpg_ref, wpu_ref, fng_ref,
                    o_ref, wins_ref, acc_ref, sem, *, cap, final_norm):
    s = pl.program_id(0)
    ne = pos_ref.shape[1]
    tt = h_ref.shape[0]

    @pl.when(s == 0)
    def _():
        wins_ref[...] = jnp.zeros_like(wins_ref)

    starts, used = _tile_windows(off_ref, s, ne)
    pos = pos_ref[0]
    aff = aff_ref[0]
    slot = lax.broadcasted_iota(I32, (MOE_SLOTS, tt), 0)
    acc_ref[...] = jnp.zeros_like(acc_ref)

    def round_body(w, c):
        copies = []
        wstarts = []
        for e in range(ne):
            ws = pl.multiple_of(jnp.minimum(starts[e] + w * MOE_SLOTS, cap - MOE_SLOTS), BF16_ROWS)
            cp = pltpu.make_async_copy(ye_ref.at[pl.ds(pl.multiple_of(e * cap + ws, BF16_ROWS), MOE_SLOTS)],
                                       wins_ref.at[pl.ds(e * MOE_SLOTS, MOE_SLOTS)], sem)
            cp.start()
            copies.append(cp)
            wstarts.append(ws)
        pieces = []
        for e in range(ne):
            base = starts[e] + w * MOE_SLOTS
            pe = pos[e:e + 1, :]
            in_round = (pe >= base) & (pe < base + MOE_SLOTS)
            hit = (jnp.broadcast_to(pe - wstarts[e], (MOE_SLOTS, tt)) == slot) & jnp.broadcast_to(in_round, (MOE_SLOTS, tt))
            pieces.append(jnp.where(hit, jnp.broadcast_to(aff[e:e + 1, :], (MOE_SLOTS, tt)), 0.0))
        a_t = jnp.concatenate(pieces, axis=0).astype(BF16)
        for cp in copies:
            cp.wait()
        acc_ref[...] += lax.dot_general(a_t, wins_ref[...], (((0,), (0,)), ((), ())), preferred_element_type=F32)
        return c

    lax.fori_loop(0, _num_rounds(used), round_body, 0)
    h2 = h_ref[...] + acc_ref[...]
    gate = jax.nn.sigmoid(jnp.dot(_rms(h2, png_ref[...]).astype(BF16), wpg_ref[...], preferred_element_type=F32))
    up = jnp.dot(p_ref[...].astype(BF16), wpu_ref[...], preferred_element_type=F32)
    h3 = h2 + up * gate
    if final_norm:
        h3 = _rms(h3, fng_ref[...])
    o_ref[...] = h3


def _combine(offs, h1, aff3, pos3, p_i, ye, ple_norm, wpg, wpu, final_g, *, cap, final_norm):
    n, d = h1.shape
    ntile, ne, tt = pos3.shape
    pd = p_i.shape[1]
    kern = functools.partial(_combine_kernel, cap=cap, final_norm=final_norm)
    return pl.pallas_call(
        kern,
        grid_spec=pltpu.PrefetchScalarGridSpec(
            num_scalar_prefetch=1,
            grid=(ntile,),
            in_specs=[
                pl.BlockSpec((tt, d), lambda i, off: (i, 0)),
                pl.BlockSpec((1, ne, tt), lambda i, off: (i, 0, 0)),
                pl.BlockSpec((1, ne, tt), lambda i, off: (i, 0, 0)),
                pl.BlockSpec((tt, pd), lambda i, off: (i, 0)),
                pl.BlockSpec(memory_space=pl.ANY),
                pl.BlockSpec((1, d), lambda i, off: (0, 0)),
                pl.BlockSpec((d, d), lambda i, off: (0, 0)),
                pl.BlockSpec((pd, d), lambda i, off: (0, 0)),
                pl.BlockSpec((1, d), lambda i, off: (0, 0)),
            ],
            out_specs=pl.BlockSpec((tt, d), lambda i, off: (i, 0)),
            scratch_shapes=[pltpu.VMEM((ne * MOE_SLOTS, d), BF16), pltpu.VMEM((tt, d), F32), pltpu.SemaphoreType.DMA(())],
        ),
        out_shape=jax.ShapeDtypeStruct((n, d), F32),
        compiler_params=_cparams(("arbitrary",)),
        name="combine_ple",
    )(offs, h1, aff3, pos3, p_i, ye, ple_norm.reshape(1, d), wpg, wpu, final_g.reshape(1, d))


def _moe_block(h1, xn, aff3, p_i, w, i, *, final_norm):
    n, d = h1.shape
    cap = EC_CAPACITY * n // N_EXPERTS
    pos3, offs = _select(aff3, cap=cap)
    xe = _dispatch(offs, xn, pos3, cap=cap)
    ye = _expert_ffn(xe, w["exp_w_gate"][i], w["exp_w_up"][i], w["exp_w_down"][i], cap=cap)
    return _combine(offs, h1, aff3, pos3, p_i, ye, w["ple_norm"][i], w["ple_w_gate"][i], w["ple_w_up"][i],
                    w["final_norm"], cap=cap, final_norm=final_norm)


def _rope_tables(seq, dim):
    inv = ROPE_THETA ** (-jnp.arange(0, dim, 2, dtype=F32) / dim)
    ang = jnp.arange(seq, dtype=F32)[:, None] * inv[None, :]
    return jnp.cos(ang), jnp.sin(ang)


def _trunk(x, p, w):
    batch, seq, d = x.shape
    n = batch * seq
    h = x.reshape(n, d)
    dk = d // RET_HEADS
    cos, sin = _rope_tables(seq, dk)

    proj = _in_proj(h, w["mix_norm"][0], w["ret_w_in"][0], cos, sin, seq=seq, rope_tiles=2,
                    tile_scales=(1.0, float(dk) ** -0.5), tn=d)
    a = _retention(proj, w["ret_decay_logit"][0], w["ret_gn_w"][0], batch=batch, seq=seq, d_model=d)
    h1, xn, aff3 = _out_router(a, w["ret_w_out"][0], h, w["ffn_norm"][0], w["router_w"][0])
    h = _moe_block(h1, xn, aff3, p[0].reshape(n, -1), w, 0, final_norm=False)

    qkv = _in_proj(h, w["mix_norm"][1], w["na_w_in"][0], cos, sin, seq=seq, rope_tiles=0,
                   tile_scales=(float(NA_HEAD_DIM) ** -0.5,), tn=d)
    bias = _na_bias_table(w["na_rpb"][0], seq // GRID_W)
    a = _na_attention(qkv, bias, batch=batch, seq=seq, d_model=d)
    h1, xn, aff3 = _out_router(a, w["na_w_out"][0], h, w["ffn_norm"][1], w["router_w"][1])
    y = _moe_block(h1, xn, aff3, p[1].reshape(n, -1), w, 1, final_norm=True)
    return y.reshape(batch, seq, d)


def kernel(x_prompt, x_sample, p_prompt, p_sample, ret_w_in, ret_decay_logit, ret_gn_w, ret_w_out, na_w_in, na_rpb, na_w_out, mix_norm, ffn_norm, ple_norm, router_w, exp_w_gate, exp_w_up, exp_w_down, ple_w_up, ple_w_gate, final_norm):
    w = dict(
        ret_w_in=ret_w_in.astype(BF16), ret_decay_logit=ret_decay_logit, ret_gn_w=ret_gn_w,
        ret_w_out=ret_w_out.astype(BF16), na_w_in=na_w_in.astype(BF16), na_rpb=na_rpb,
        na_w_out=na_w_out.astype(BF16), mix_norm=mix_norm, ffn_norm=ffn_norm, ple_norm=ple_norm,
        router_w=router_w, exp_w_gate=exp_w_gate.astype(BF16), exp_w_up=exp_w_up.astype(BF16),
        exp_w_down=exp_w_down.astype(BF16), ple_w_up=ple_w_up.astype(BF16), ple_w_gate=ple_w_gate.astype(BF16),
        final_norm=final_norm,
    )
    return _trunk(x_prompt, p_prompt, w), _trunk(x_sample, p_sample, w)
```

```python
import functools

import numpy as np
import jax
import jax.numpy as jnp
from jax import lax
from jax.experimental import pallas as pl
from jax.experimental.pallas import tpu as pltpu

F32 = jnp.float32
BF16 = jnp.bfloat16
I32 = jnp.int32

NORM_EPS = 1e-6
ROPE_THETA = 10000.0
GRID_W = 64
RET_HEADS = 4
RET_CHUNK = 256
NA_HEADS = 16
NA_HEAD_DIM = 64
NA_MAX_ROWS = 8
NA_WIN_COLS = 16
NA_ROW_UNROLL = 4
N_EXPERTS = 16
EC_CAPACITY = 2
LANES = 128
BF16_ROWS = 16
MOE_TILE = 256
MOE_SLOTS = 64
NEG_BIG = -1e30

_VMEM_LIMIT = 56 * 1024 * 1024


def _cparams(sem):
    return pltpu.CompilerParams(dimension_semantics=sem, vmem_limit_bytes=_VMEM_LIMIT)


def _rms(x, g):
    return x * lax.rsqrt(jnp.mean(x * x, axis=-1, keepdims=True) + NORM_EPS) * g


def _in_proj_kernel(x_ref, g_ref, w_ref, cos_ref, sin_ref, o_ref, xn_ref, *, rope_tiles, tile_scales, rope_dim):
    j = pl.program_id(1)

    @pl.when(j == 0)
    def _():
        xn_ref[...] = _rms(x_ref[...], g_ref[...]).astype(BF16)

    y = jnp.dot(xn_ref[...], w_ref[...], preferred_element_type=F32)
    scale = jnp.float32(1.0)
    for t, s in enumerate(tile_scales):
        if s != 1.0:
            scale = jnp.where(j == t, jnp.float32(s), scale)
    y = y * scale
    tn = y.shape[1]
    half = rope_dim // 2

    if rope_tiles > 0:
        @pl.when(j < rope_tiles)
        def _():
            cos = cos_ref[...]
            sin = sin_ref[...]
            for hh in range(tn // rope_dim):
                lo = hh * rope_dim
                x1 = y[:, lo:lo + half]
                x2 = y[:, lo + half:lo + rope_dim]
                o_ref[:, lo:lo + half] = (x1 * cos - x2 * sin).astype(o_ref.dtype)
                o_ref[:, lo + half:lo + rope_dim] = (x1 * sin + x2 * cos).astype(o_ref.dtype)

        @pl.when(j >= rope_tiles)
        def _():
            o_ref[...] = y.astype(o_ref.dtype)
    else:
        o_ref[...] = y.astype(o_ref.dtype)


def _in_proj(x, g, w_bf16, cos, sin, *, seq, rope_tiles, tile_scales, tm=1024, tn=1024):
    n, d = x.shape
    tm = min(tm, seq)
    ncol = w_bf16.shape[1]
    rope_dim = 2 * cos.shape[1]
    nseq = seq // tm
    kern = functools.partial(_in_proj_kernel, rope_tiles=rope_tiles, tile_scales=tile_scales, rope_dim=rope_dim)
    return pl.pallas_call(
        kern,
        grid=(n // tm, ncol // tn),
        in_specs=[
            pl.BlockSpec((tm, d), lambda i, j: (i, 0)),
            pl.BlockSpec((1, d), lambda i, j: (0, 0)),
            pl.BlockSpec((d, tn), lambda i, j: (0, j)),
            pl.BlockSpec((tm, cos.shape[1]), lambda i, j: (i % nseq, 0)),
            pl.BlockSpec((tm, cos.shape[1]), lambda i, j: (i % nseq, 0)),
        ],
        out_specs=pl.BlockSpec((tm, tn), lambda i, j: (i, j)),
        out_shape=jax.ShapeDtypeStruct((n, ncol), BF16),
        scratch_shapes=[pltpu.VMEM((tm, d), BF16)],
        compiler_params=_cparams(("parallel", "arbitrary")),
        name="in_proj",
    )(x, g.reshape(1, d), w_bf16, cos, sin)


def _retention_kernel(dl_ref, q_ref, k_ref, v_ref, g_ref, gnw_ref, o_ref, acc_ref, st_ref, *, seq, chunk):
    h = pl.program_id(1)
    nc = seq // chunk
    dl = dl_ref[...]
    lg = jnp.minimum(dl, 0.0) - jnp.log1p(jnp.exp(-jnp.abs(dl)))
    col = lax.broadcasted_iota(I32, dl.shape, 1)
    lgh = jnp.sum(jnp.where(col == h, lg, 0.0), axis=1, keepdims=True)
    lgf = lgh[0:1, :]
    lgb = lgh[1:2, :]

    ri = lax.broadcasted_iota(I32, (chunk, chunk), 0)
    ci = lax.broadcasted_iota(I32, (chunk, chunk), 1)
    diff = (ri - ci).astype(F32)
    dmat = jnp.exp(jnp.where(diff >= 0, diff * lgf, -diff * lgb))
    pos = lax.broadcasted_iota(I32, (chunk, 1), 0).astype(F32)
    qdec_f = jnp.exp((pos + 1.0) * lgf)
    kdec_f = jnp.exp((chunk - 1.0 - pos) * lgf)
    qdec_b = jnp.exp((chunk - pos) * lgb)
    kdec_b = jnp.exp(pos * lgb)
    cdec_f = jnp.exp(chunk * lgf)
    cdec_b = jnp.exp(chunk * lgb)
    nt = (((1,), (1,)), ((), ()))
    tn = (((0,), (0,)), ((), ()))

    st_ref[...] = jnp.zeros_like(st_ref)

    def fwd(c, carry):
        sl = pl.ds(pl.multiple_of(c * chunk, chunk), chunk)
        qc = q_ref[sl, :]
        kc = k_ref[sl, :]
        vc = v_ref[sl, :]
        s = lax.dot_general(qc, kc, nt, preferred_element_type=F32) * dmat
        inner = jnp.dot(s.astype(BF16), vc, preferred_element_type=F32)
        qd = (qc.astype(F32) * qdec_f).astype(BF16)
        cross = jnp.dot(qd, st_ref[...].astype(BF16), preferred_element_type=F32)
        acc_ref[sl, :] = inner + cross
        kd = (kc.astype(F32) * kdec_f).astype(BF16)
        st_ref[...] = st_ref[...] * cdec_f + lax.dot_general(kd, vc, tn, preferred_element_type=F32)
        return carry

    lax.fori_loop(0, nc, fwd, 0, unroll=2)
    st_ref[...] = jnp.zeros_like(st_ref)

    def bwd(t, carry):
        c = nc - 1 - t
        sl = pl.ds(pl.multiple_of(c * chunk, chunk), chunk)
        qc = q_ref[sl, :]
        kc = k_ref[sl, :]
        vc = v_ref[sl, :]
        qd = (qc.astype(F32) * qdec_b).astype(BF16)
        o = acc_ref[sl, :] + jnp.dot(qd, st_ref[...].astype(BF16), preferred_element_type=F32)
        kd = (kc.astype(F32) * kdec_b).astype(BF16)
        st_ref[...] = st_ref[...] * cdec_b + lax.dot_general(kd, vc, tn, preferred_element_type=F32)
        mu = jnp.mean(o, axis=-1, keepdims=True)
        oc = o - mu
        var = jnp.mean(oc * oc, axis=-1, keepdims=True)
        on = oc * lax.rsqrt(var + NORM_EPS) * gnw_ref[...]
        gate = g_ref[sl, :].astype(F32)
        o_ref[sl, :] = (gate * jax.nn.sigmoid(gate) * on).astype(o_ref.dtype)
        return carry

    lax.fori_loop(0, nc, bwd, 0, unroll=2)


def _retention(proj, decay_logit, gn_w, *, batch, seq, d_model):
    heads = RET_HEADS
    dk = d_model // heads
    dv = 2 * d_model // heads
    n = batch * seq
    kern = functools.partial(_retention_kernel, seq=seq, chunk=RET_CHUNK)
    return pl.pallas_call(
        kern,
        grid=(batch, heads),
        in_specs=[
            pl.BlockSpec((2, heads), lambda b, h: (0, 0)),
            pl.BlockSpec((seq, dk), lambda b, h: (b, h)),
            pl.BlockSpec((seq, dk), lambda b, h: (b, heads + h)),
            pl.BlockSpec((seq, dv), lambda b, h: (b, heads + h)),
            pl.BlockSpec((seq, dv), lambda b, h: (b, 2 * heads + h)),
            pl.BlockSpec((1, dv), lambda b, h: (0, h)),
        ],
        out_specs=pl.BlockSpec((seq, dv), lambda b, h: (b, h)),
        out_shape=jax.ShapeDtypeStruct((n, 2 * d_model), BF16),
        scratch_shapes=[pltpu.VMEM((seq, dv), F32), pltpu.VMEM((dk, dv), F32)],
        compiler_params=_cparams(("parallel", "parallel")),
        name="retention",
    )(decay_logit, proj, proj, proj, proj, gn_w.reshape(1, -1))


def _na_kernel(bias_ref, q_ref, k_ref, v_ref, o_ref, *, rows):
    gw = GRID_W
    kr = min(NA_MAX_ROWS, rows)
    dh = NA_HEAD_DIM
    lane_q = lax.broadcasted_iota(I32, (gw, 2 * dh), 1)
    nt = (((1,), (1,)), ((), ()))

    def body(r, carry):
        rs = jnp.clip(r - kr // 2, 0, rows - kr)
        d = r - rs
        q2 = q_ref[pl.ds(pl.multiple_of(r * gw, gw), gw), :]
        zero = jnp.zeros_like(q2)
        qq = jnp.concatenate([jnp.where(lane_q < dh, q2, zero), jnp.where(lane_q >= dh, q2, zero)], axis=0)
        ksl = pl.ds(pl.multiple_of(rs * gw, gw), kr * gw)
        k2 = k_ref[ksl, :]
        v2 = v_ref[ksl, :]
        s = lax.dot_general(qq, k2, nt, preferred_element_type=F32) + bias_ref[0, d]
        m = jnp.max(s, axis=-1, keepdims=True)
        p = jnp.exp(s - m)
        inv = 1.0 / jnp.sum(p, axis=-1, keepdims=True)
        o = jnp.dot(p.astype(BF16), v2, preferred_element_type=F32) * inv
        out = jnp.where(lane_q < dh, o[0:gw, :], o[gw:2 * gw, :])
        o_ref[pl.ds(pl.multiple_of(r * gw, gw), gw), :] = out.astype(o_ref.dtype)
        return carry

    lax.fori_loop(0, rows, body, 0, unroll=NA_ROW_UNROLL)


def _na_bias_table(rpb, rows):
    kr = min(NA_MAX_ROWS, rows)
    c = np.arange(GRID_W)
    kc = np.arange(GRID_W)
    win_start = np.clip(c - NA_WIN_COLS // 2, 0, GRID_W - NA_WIN_COLS)
    valid = (kc[None, :] >= win_start[:, None]) & (kc[None, :] < win_start[:, None] + NA_WIN_COLS)
    dc_idx = np.clip(kc[None, :] - c[:, None] + NA_WIN_COLS - 1, 0, 2 * NA_WIN_COLS - 2)
    delta = np.arange(kr)
    a = np.arange(kr)
    dr_idx = a[None, :] - delta[:, None] + NA_MAX_ROWS - 1
    t = rpb[:, dr_idx][:, :, :, dc_idx]
    t = jnp.where(jnp.asarray(valid)[None, None, None], t.astype(F32), NEG_BIG)
    t = t.transpose(0, 1, 3, 2, 4)
    h = rpb.shape[0]
    t = t.reshape(h // 2, 2, kr, GRID_W, kr * GRID_W).transpose(0, 2, 1, 3, 4)
    return t.reshape(h // 2, kr, 2 * GRID_W, kr * GRID_W)


def _na_attention(qkv, bias, *, batch, seq, d_model):
    n = batch * seq
    rows = seq // GRID_W
    kr = min(NA_MAX_ROWS, rows)
    pairs = NA_HEADS // 2
    pw = 2 * NA_HEAD_DIM
    kern = functools.partial(_na_kernel, rows=rows)
    return pl.pallas_call(
        kern,
        grid=(batch, pairs),
        in_specs=[
            pl.BlockSpec((1, kr, 2 * GRID_W, kr * GRID_W), lambda b, hp: (hp, 0, 0, 0)),
            pl.BlockSpec((seq, pw), lambda b, hp: (b, hp)),
            pl.BlockSpec((seq, pw), lambda b, hp: (b, pairs + hp)),
            pl.BlockSpec((seq, pw), lambda b, hp: (b, 2 * pairs + hp)),
        ],
        out_specs=pl.BlockSpec((seq, pw), lambda b, hp: (b, hp)),
        out_shape=jax.ShapeDtypeStruct((n, d_model), BF16),
        compiler_params=_cparams(("parallel", "parallel")),
        name="na_attention",
    )(bias, qkv, qkv, qkv)


def _out_router_kernel(a_ref, w_ref, h_ref, g_ref, rw_ref, h1_ref, xn_ref, aff_ref):
    y = jnp.dot(a_ref[...], w_ref[...], preferred_element_type=F32) + h_ref[...]
    h1_ref[...] = y
    xn = _rms(y, g_ref[...])
    xn_ref[...] = xn.astype(xn_ref.dtype)
    nt = (((1,), (1,)), ((), ()))
    ne = rw_ref.shape[0]
    xh = xn.astype(BF16)
    xl = (xn - xh.astype(F32)).astype(BF16)
    rw = rw_ref[...]
    rh = rw.astype(BF16)
    rl = (rw - rh.astype(F32)).astype(BF16)
    t1 = lax.dot_general(jnp.concatenate([rh, rl], axis=0), xh, nt, preferred_element_type=F32)
    t2 = lax.dot_general(rh, xl, nt, preferred_element_type=F32)
    logits = t1[0:ne, :] + (t1[ne:2 * ne, :] + t2)
    m = jnp.max(logits, axis=0, keepdims=True)
    e = jnp.exp(logits - m)
    aff = e / jnp.sum(e, axis=0, keepdims=True)
    for t in range(aff_ref.shape[0]):
        aff_ref[t] = aff[:, t * MOE_TILE:(t + 1) * MOE_TILE]


def _out_router(a, w_bf16, h, g, router_w, *, tm=512):
    n, kdim = a.shape
    d = h.shape[1]
    ne = router_w.shape[1]
    tpb = tm // MOE_TILE
    return pl.pallas_call(
        _out_router_kernel,
        grid=(n // tm,),
        in_specs=[
            pl.BlockSpec((tm, kdim), lambda i: (i, 0)),
            pl.BlockSpec((kdim, d), lambda i: (0, 0)),
            pl.BlockSpec((tm, d), lambda i: (i, 0)),
            pl.BlockSpec((1, d), lambda i: (0, 0)),
            pl.BlockSpec((ne, d), lambda i: (0, 0)),
        ],
        out_specs=[
            pl.BlockSpec((tm, d), lambda i: (i, 0)),
            pl.BlockSpec((tm, d), lambda i: (i, 0)),
            pl.BlockSpec((tpb, ne, MOE_TILE), lambda i: (i, 0, 0)),
        ],
        out_shape=[
            jax.ShapeDtypeStruct((n, d), F32),
            jax.ShapeDtypeStruct((n, d), BF16),
            jax.ShapeDtypeStruct((n // MOE_TILE, ne, MOE_TILE), F32),
        ],
        compiler_params=_cparams(("parallel",)),
        name="out_router",
    )(a, w_bf16, h, g.reshape(1, d), router_w.T)


def _select_kernel(aff_ref, pos_ref, off_ref, *, cap):
    ntile, ne, tt = aff_ref.shape
    nsub = tt // LANES

    def count(pred_fn):
        def body(c, acc):
            x = pltpu.bitcast(aff_ref[c], I32)
            return acc + pred_fn(x).astype(I32)
        acc = lax.fori_loop(0, ntile, body, jnp.zeros((ne, tt), I32))
        return jnp.sum(acc, axis=1, keepdims=True)

    def bit_step(i, t):
        cand = t | jnp.left_shift(jnp.int32(1), 30 - i)
        return jnp.where(count(lambda x: x >= cand) >= cap, cand, t)

    thr = lax.fori_loop(0, 31, bit_step, jnp.zeros((ne, 1), I32))
    need_eq = (cap - count(lambda x: x > thr)).astype(F32)

    li = lax.broadcasted_iota(I32, (LANES, LANES), 0)
    lj = lax.broadcasted_iota(I32, (LANES, LANES), 1)
    upper = (li < lj).astype(BF16)

    def tile_body(c, carry):
        eq_carry, pos_carry = carry
        off_ref[c] = jnp.broadcast_to(pos_carry, (ne, LANES)).astype(I32)
        xt = pltpu.bitcast(aff_ref[c], I32)
        parts = []
        for j in range(nsub):
            x = xt[:, j * LANES:(j + 1) * LANES]
            gt = x > thr
            eq = x == thr
            eqf = eq.astype(F32)
            eq_rank = eq_carry + jnp.dot(eqf.astype(BF16), upper, preferred_element_type=F32)
            sel = gt | (eq & (eq_rank < need_eq))
            eq_carry = eq_carry + jnp.sum(eqf, axis=1, keepdims=True)
            self_ = sel.astype(F32)
            cexc = jnp.dot(self_.astype(BF16), upper, preferred_element_type=F32)
            parts.append(jnp.where(sel, (pos_carry + cexc).astype(I32), -1))
            pos_carry = pos_carry + jnp.sum(self_, axis=1, keepdims=True)
        pos_ref[c] = jnp.concatenate(parts, axis=1)
        return eq_carry, pos_carry

    lax.fori_loop(0, ntile, tile_body, (jnp.zeros((ne, 1), F32), jnp.zeros((ne, 1), F32)))


def _select(aff3, *, cap):
    ntile, ne, tt = aff3.shape
    kern = functools.partial(_select_kernel, cap=cap)
    pos3, off3 = pl.pallas_call(
        kern,
        out_shape=[
            jax.ShapeDtypeStruct((ntile, ne, tt), I32),
            jax.ShapeDtypeStruct((ntile, ne, LANES), I32),
        ],
        compiler_params=pltpu.CompilerParams(vmem_limit_bytes=_VMEM_LIMIT),
        name="ec_select",
    )(aff3)
    offs = jnp.concatenate([off3[:, :, 0], jnp.full((1, ne), cap, I32)], axis=0).reshape(-1)
    return pos3, offs


def _tile_windows(off_ref, s, ne):
    starts, used = [], []
    for e in range(ne):
        off = off_ref[s * ne + e]
        nxt = off_ref[(s + 1) * ne + e]
        st = (off // BF16_ROWS) * BF16_ROWS
        starts.append(st)
        used.append(nxt - st)
    return starts, used


def _num_rounds(used):
    m = used[0]
    for u in used[1:]:
        m = jnp.maximum(m, u)
    return (m + MOE_SLOTS - 1) // MOE_SLOTS


def _dispatch_kernel(off_ref, x_ref, pos_ref, xe_ref, wins_ref, carry_ref, sem, nout_ref, *, cap):
    s = pl.program_id(0)
    ne = pos_ref.shape[1]
    tt = x_ref.shape[0]
    gr = BF16_ROWS
    gpr = MOE_SLOTS // gr

    @pl.when(s == 0)
    def _():
        carry_ref[...] = jnp.zeros_like(carry_ref)
        nout_ref[0] = 0
        nout_ref[1] = 0

    starts, used = _tile_windows(off_ref, s, ne)
    pos = pos_ref[0]
    x = x_ref[...]
    slot = lax.broadcasted_iota(I32, (MOE_SLOTS, tt), 0)

    def drain(b):
        def wait_one(i, c):
            pltpu.make_async_copy(wins_ref.at[b, pl.ds(0, gr)], xe_ref.at[pl.ds(0, gr)], sem.at[b]).wait()
            return c
        lax.fori_loop(0, nout_ref[b], wait_one, 0)
        nout_ref[b] = 0

    def round_body(w, c):
        b = (s + w) % 2
        drain(b)
        pieces = []
        for e in range(ne):
            key = pos[e:e + 1, :] - (starts[e] + w * MOE_SLOTS)
            pieces.append((jnp.broadcast_to(key, (MOE_SLOTS, tt)) == slot).astype(F32))
        onehot = jnp.concatenate(pieces, axis=0).astype(BF16)
        wins_ref[b] = jnp.dot(onehot, x, preferred_element_type=F32).astype(BF16)

        @pl.when(w == 0)
        def _():
            for e in range(ne):
                rows = pl.ds(e * MOE_SLOTS, gr)
                wins_ref[b, rows, :] = (wins_ref[b, rows, :].astype(F32)
                                        + carry_ref[pl.ds(e * gr, gr), :].astype(F32)).astype(BF16)

        issued = 0
        for e in range(ne):
            ngc = used[e] // gr
            rem = used[e] % gr
            ng_w = jnp.clip(ngc - w * gpr, 0, gpr)

            def issue(j, c2, e=e):
                src = wins_ref.at[b, pl.ds(e * MOE_SLOTS + j * gr, gr)]
                dst = xe_ref.at[pl.ds(pl.multiple_of(e * cap + starts[e] + w * MOE_SLOTS + j * gr, gr), gr)]
                pltpu.make_async_copy(src, dst, sem.at[b]).start()
                return c2

            lax.fori_loop(0, ng_w, issue, 0)
            issued = issued + ng_w

            @pl.when((rem > 0) & (ngc // gpr == w))
            def _(e=e, ngc=ngc):
                carry_ref[pl.ds(e * gr, gr), :] = wins_ref[b, pl.ds(e * MOE_SLOTS + (ngc % gpr) * gr, gr), :]

            @pl.when((rem == 0) & (w == 0))
            def _(e=e):
                carry_ref[pl.ds(e * gr, gr), :] = jnp.zeros((gr, carry_ref.shape[1]), carry_ref.dtype)

        nout_ref[b] = issued
        return c

    lax.fori_loop(0, _num_rounds(used), round_body, 0)

    @pl.when(s == pl.num_programs(0) - 1)
    def _():
        drain(0)
        drain(1)


def _dispatch(offs, xn, pos3, *, cap):
    n, d = xn.shape
    ntile, ne, tt = pos3.shape
    kern = functools.partial(_dispatch_kernel, cap=cap)
    return pl.pallas_call(
        kern,
        grid_spec=pltpu.PrefetchScalarGridSpec(
            num_scalar_prefetch=1,
            grid=(ntile,),
            in_specs=[
                pl.BlockSpec((tt, d), lambda i, off: (i, 0)),
                pl.BlockSpec((1, ne, tt), lambda i, off: (i, 0, 0)),
            ],
            out_specs=pl.BlockSpec(memory_space=pl.ANY),
            scratch_shapes=[
                pltpu.VMEM((2, ne * MOE_SLOTS, d), BF16),
                pltpu.VMEM((ne * BF16_ROWS, d), BF16),
                pltpu.SemaphoreType.DMA((2,)),
                pltpu.SMEM((2,), I32),
            ],
        ),
        out_shape=jax.ShapeDtypeStruct((ne * cap, d), BF16),
        compiler_params=_cparams(("arbitrary",)),
        name="ec_dispatch",
    )(offs, xn, pos3)


def _ffn_kernel(x_ref, wg_ref, wu_ref, wd_ref, o_ref):
    x = x_ref[...]
    a = jnp.dot(x, wg_ref[0], preferred_element_type=F32)
    u = jnp.dot(x, wu_ref[0], preferred_element_type=F32)
    hid = (a * jax.nn.sigmoid(a) * u).astype(BF16)
    o_ref[...] = jnp.dot(hid, wd_ref[0], preferred_element_type=F32).astype(o_ref.dtype)


def _expert_ffn(xe, wg, wu, wd, *, cap, tr=512):
    ne, d, f = wg.shape
    tr = min(tr, cap)
    nt = cap // tr
    return pl.pallas_call(
        _ffn_kernel,
        grid=(ne, nt),
        in_specs=[
            pl.BlockSpec((tr, d), lambda e, t: (e * nt + t, 0)),
            pl.BlockSpec((1, d, f), lambda e, t: (e, 0, 0)),
            pl.BlockSpec((1, d, f), lambda e, t: (e, 0, 0)),
            pl.BlockSpec((1, f, d), lambda e, t: (e, 0, 0)),
        ],
        out_specs=pl.BlockSpec((tr, d), lambda e, t: (e * nt + t, 0)),
        out_shape=jax.ShapeDtypeStruct((ne * cap, d), BF16),
        compiler_params=_cparams(("parallel", "parallel")),
        name="expert_ffn",
    )(xe, wg, wu, wd)


def _combine_kernel(off_ref, h_ref, aff_ref, pos_ref, p_ref, ye_ref, png_ref, wpg_ref, wpu_ref, fng_ref,
                    o_ref, wins_ref, acc_ref, sem, *, cap, final_norm):
    s = pl.program_id(0)
    ntile = pl.num_programs(0)
    ne = pos_ref.shape[1]
    tt = h_ref.shape[0]
    spill_buf = 2

    def window_copies(starts_t, w, buf):
        copies, wstarts = [], []
        for e in range(ne):
            ws = pl.multiple_of(jnp.minimum(starts_t[e] + w * MOE_SLOTS, cap - MOE_SLOTS), BF16_ROWS)
            copies.append(pltpu.make_async_copy(
                ye_ref.at[pl.ds(pl.multiple_of(e * cap + ws, BF16_ROWS), MOE_SLOTS)],
                wins_ref.at[buf, pl.ds(e * MOE_SLOTS, MOE_SLOTS)], sem.at[buf]))
            wstarts.append(ws)
        return copies, wstarts

    starts, used = _tile_windows(off_ref, s, ne)
    cur = s % 2

    @pl.when(s == 0)
    def _():
        for cp in window_copies(starts, 0, cur)[0]:
            cp.start()

    @pl.when(s + 1 < ntile)
    def _():
        nxt_starts, _ = _tile_windows(off_ref, s + 1, ne)
        for cp in window_copies(nxt_starts, 0, 1 - cur)[0]:
            cp.start()

    pos = pos_ref[0]
    aff = aff_ref[0]
    slot = lax.broadcasted_iota(I32, (MOE_SLOTS, tt), 0)

    def gates(w, wstarts):
        pieces = []
        for e in range(ne):
            base = starts[e] + w * MOE_SLOTS
            pe = pos[e:e + 1, :]
            in_round = (pe >= base) & (pe < base + MOE_SLOTS)
            hit = (jnp.broadcast_to(pe - wstarts[e], (MOE_SLOTS, tt)) == slot) & jnp.broadcast_to(in_round, (MOE_SLOTS, tt))
            pieces.append(jnp.where(hit, jnp.broadcast_to(aff[e:e + 1, :], (MOE_SLOTS, tt)), 0.0))
        return jnp.concatenate(pieces, axis=0).astype(BF16)

    tn = (((0,), (0,)), ((), ()))
    copies0, wstarts0 = window_copies(starts, 0, cur)
    a0 = gates(0, wstarts0)
    for cp in copies0:
        cp.wait()
    acc_ref[...] = lax.dot_general(a0, wins_ref[cur], tn, preferred_element_type=F32)

    def round_body(w, c):
        copies, wstarts = window_copies(starts, w, spill_buf)
        for cp in copies:
            cp.start()
        a_w = gates(w, wstarts)
        for cp in copies:
            cp.wait()
        acc_ref[...] += lax.dot_general(a_w, wins_ref[spill_buf], tn, preferred_element_type=F32)
        return c

    lax.fori_loop(1, _num_rounds(used), round_body, 0)
    h2 = h_ref[...] + acc_ref[...]
    gate = jax.nn.sigmoid(jnp.dot(_rms(h2, png_ref[...]).astype(BF16), wpg_ref[...], preferred_element_type=F32))
    up = jnp.dot(p_ref[...].astype(BF16), wpu_ref[...], preferred_element_type=F32)
    h3 = h2 + up * gate
    if final_norm:
        h3 = _rms(h3, fng_ref[...])
    o_ref[...] = h3


def _combine(offs, h1, aff3, pos3, p_i, ye, ple_norm, wpg, wpu, final_g, *, cap, final_norm):
    n, d = h1.shape
    ntile, ne, tt = pos3.shape
    pd = p_i.shape[1]
    kern = functools.partial(_combine_kernel, cap=cap, final_norm=final_norm)
    return pl.pallas_call(
        kern,
        grid_spec=pltpu.PrefetchScalarGridSpec(
            num_scalar_prefetch=1,
            grid=(ntile,),
            in_specs=[
                pl.BlockSpec((tt, d), lambda i, off: (i, 0)),
                pl.BlockSpec((1, ne, tt), lambda i, off: (i, 0, 0)),
                pl.BlockSpec((1, ne, tt), lambda i, off: (i, 0, 0)),
                pl.BlockSpec((tt, pd), lambda i, off: (i, 0)),
                pl.BlockSpec(memory_space=pl.ANY),
                pl.BlockSpec((1, d), lambda i, off: (0, 0)),
                pl.BlockSpec((d, d), lambda i, off: (0, 0)),
                pl.BlockSpec((pd, d), lambda i, off: (0, 0)),
                pl.BlockSpec((1, d), lambda i, off: (0, 0)),
            ],
            out_specs=pl.BlockSpec((tt, d), lambda i, off: (i, 0)),
            scratch_shapes=[pltpu.VMEM((3, ne * MOE_SLOTS, d), BF16), pltpu.VMEM((tt, d), F32),
                            pltpu.SemaphoreType.DMA((3,))],
        ),
        out_shape=jax.ShapeDtypeStruct((n, d), F32),
        compiler_params=_cparams(("arbitrary",)),
        name="combine_ple",
    )(offs, h1, aff3, pos3, p_i, ye, ple_norm.reshape(1, d), wpg, wpu, final_g.reshape(1, d))


def _moe_block(h1, xn, aff3, p_i, w, i, *, final_norm):
    n, d = h1.shape
    cap = EC_CAPACITY * n // N_EXPERTS
    pos3, offs = _select(aff3, cap=cap)
    xe = _dispatch(offs, xn, pos3, cap=cap)
    ye = _expert_ffn(xe, w["exp_w_gate"][i], w["exp_w_up"][i], w["exp_w_down"][i], cap=cap)
    return _combine(offs, h1, aff3, pos3, p_i, ye, w["ple_norm"][i], w["ple_w_gate"][i], w["ple_w_up"][i],
                    w["final_norm"], cap=cap, final_norm=final_norm)


def _rope_tables(seq, dim):
    inv = ROPE_THETA ** (-jnp.arange(0, dim, 2, dtype=F32) / dim)
    ang = jnp.arange(seq, dtype=F32)[:, None] * inv[None, :]
    return jnp.cos(ang), jnp.sin(ang)


def _trunk(x, p, w):
    batch, seq, d = x.shape
    n = batch * seq
    h = x.reshape(n, d)
    dk = d // RET_HEADS
    cos, sin = _rope_tables(seq, dk)

    proj = _in_proj(h, w["mix_norm"][0], w["ret_w_in"][0], cos, sin, seq=seq, rope_tiles=2,
                    tile_scales=(1.0, float(dk) ** -0.5), tn=d)
    a = _retention(proj, w["ret_decay_logit"][0], w["ret_gn_w"][0], batch=batch, seq=seq, d_model=d)
    h1, xn, aff3 = _out_router(a, w["ret_w_out"][0], h, w["ffn_norm"][0], w["router_w"][0])
    h = _moe_block(h1, xn, aff3, p[0].reshape(n, -1), w, 0, final_norm=False)

    qkv = _in_proj(h, w["mix_norm"][1], w["na_w_in"][0], cos, sin, seq=seq, rope_tiles=0,
                   tile_scales=(float(NA_HEAD_DIM) ** -0.5,), tn=d)
    bias = _na_bias_table(w["na_rpb"][0], seq // GRID_W)
    a = _na_attention(qkv, bias, batch=batch, seq=seq, d_model=d)
    h1, xn, aff3 = _out_router(a, w["na_w_out"][0], h, w["ffn_norm"][1], w["router_w"][1])
    y = _moe_block(h1, xn, aff3, p[1].reshape(n, -1), w, 1, final_norm=True)
    return y.reshape(batch, seq, d)


def kernel(x_prompt, x_sample, p_prompt, p_sample, ret_w_in, ret_decay_logit, ret_gn_w, ret_w_out, na_w_in, na_rpb, na_w_out, mix_norm, ffn_norm, ple_norm, router_w, exp_w_gate, exp_w_up, exp_w_down, ple_w_up, ple_w_gate, final_norm):
    w = dict(
        ret_w_in=ret_w_in.astype(BF16), ret_decay_logit=ret_decay_logit, ret_gn_w=ret_gn_w,
        ret_w_out=ret_w_out.astype(BF16), na_w_in=na_w_in.astype(BF16), na_rpb=na_rpb,
        na_w_out=na_w_out.astype(BF16), mix_norm=mix_norm, ffn_norm=ffn_norm, ple_norm=ple_norm,
        router_w=router_w, exp_w_gate=exp_w_gate.astype(BF16), exp_w_up=exp_w_up.astype(BF16),
        exp_w_down=exp_w_down.astype(BF16), ple_w_up=ple_w_up.astype(BF16), ple_w_gate=ple_w_gate.astype(BF16),
        final_norm=final_norm,
    )
    return _trunk(x_prompt, p_prompt, w), _trunk(x_sample, p_sample, w)
```

```python
import functools

import numpy as np
import jax
import jax.numpy as jnp
from jax import lax
from jax.experimental import pallas as pl
from jax.experimental.pallas import tpu as pltpu

F32 = jnp.float32
BF16 = jnp.bfloat16
I32 = jnp.int32

NORM_EPS = 1e-6
ROPE_THETA = 10000.0
GRID_W = 64
RET_HEADS = 4
RET_CHUNK = 256
NA_HEADS = 16
NA_HEAD_DIM = 64
NA_MAX_ROWS = 8
NA_WIN_COLS = 16
NA_ROW_UNROLL = 8
N_EXPERTS = 16
EC_CAPACITY = 2
LANES = 128
BF16_ROWS = 16
MOE_TILE = 256
MOE_SLOTS = 64
NEG_BIG = -1e30

_VMEM_LIMIT = 56 * 1024 * 1024


def _cparams(sem):
    return pltpu.CompilerParams(dimension_semantics=sem, vmem_limit_bytes=_VMEM_LIMIT)


def _rms(x, g):
    return x * lax.rsqrt(jnp.mean(x * x, axis=-1, keepdims=True) + NORM_EPS) * g


def _in_proj_kernel(x_ref, g_ref, w_ref, cos_ref, sin_ref, o_ref, xn_ref, *, rope_tiles, tile_scales, rope_dim):
    j = pl.program_id(1)

    @pl.when(j == 0)
    def _():
        xn_ref[...] = _rms(x_ref[...], g_ref[...]).astype(BF16)

    y = jnp.dot(xn_ref[...], w_ref[...], preferred_element_type=F32)
    scale = jnp.float32(1.0)
    for t, s in enumerate(tile_scales):
        if s != 1.0:
            scale = jnp.where(j == t, jnp.float32(s), scale)
    y = y * scale
    tn = y.shape[1]
    half = rope_dim // 2

    if rope_tiles > 0:
        @pl.when(j < rope_tiles)
        def _():
            cos = cos_ref[...]
            sin = sin_ref[...]
            for hh in range(tn // rope_dim):
                lo = hh * rope_dim
                x1 = y[:, lo:lo + half]
                x2 = y[:, lo + half:lo + rope_dim]
                o_ref[:, lo:lo + half] = (x1 * cos - x2 * sin).astype(o_ref.dtype)
                o_ref[:, lo + half:lo + rope_dim] = (x1 * sin + x2 * cos).astype(o_ref.dtype)

        @pl.when(j >= rope_tiles)
        def _():
            o_ref[...] = y.astype(o_ref.dtype)
    else:
        o_ref[...] = y.astype(o_ref.dtype)


def _in_proj(x, g, w_bf16, cos, sin, *, seq, rope_tiles, tile_scales, tm=1024, tn=1024):
    n, d = x.shape
    tm = min(tm, seq)
    ncol = w_bf16.shape[1]
    rope_dim = 2 * cos.shape[1]
    nseq = seq // tm
    kern = functools.partial(_in_proj_kernel, rope_tiles=rope_tiles, tile_scales=tile_scales, rope_dim=rope_dim)
    return pl.pallas_call(
        kern,
        grid=(n // tm, ncol // tn),
        in_specs=[
            pl.BlockSpec((tm, d), lambda i, j: (i, 0)),
            pl.BlockSpec((1, d), lambda i, j: (0, 0)),
            pl.BlockSpec((d, tn), lambda i, j: (0, j)),
            pl.BlockSpec((tm, cos.shape[1]), lambda i, j: (i % nseq, 0)),
            pl.BlockSpec((tm, cos.shape[1]), lambda i, j: (i % nseq, 0)),
        ],
        out_specs=pl.BlockSpec((tm, tn), lambda i, j: (i, j)),
        out_shape=jax.ShapeDtypeStruct((n, ncol), BF16),
        scratch_shapes=[pltpu.VMEM((tm, d), BF16)],
        compiler_params=_cparams(("parallel", "arbitrary")),
        name="in_proj",
    )(x, g.reshape(1, d), w_bf16, cos, sin)


def _retention_kernel(dl_ref, q_ref, k_ref, v_ref, g_ref, gnw_ref, o_ref, acc_ref, stf_ref, stb_ref, *, seq, chunk):
    h = pl.program_id(1)
    nc = seq // chunk
    dl = dl_ref[...]
    lg = jnp.minimum(dl, 0.0) - jnp.log1p(jnp.exp(-jnp.abs(dl)))
    col = lax.broadcasted_iota(I32, dl.shape, 1)
    lgh = jnp.sum(jnp.where(col == h, lg, 0.0), axis=1, keepdims=True)
    lgf = lgh[0:1, :]
    lgb = lgh[1:2, :]

    ri = lax.broadcasted_iota(I32, (chunk, chunk), 0)
    ci = lax.broadcasted_iota(I32, (chunk, chunk), 1)
    diff = (ri - ci).astype(F32)
    dmat = jnp.exp(jnp.where(diff >= 0, diff * lgf, -diff * lgb))
    pos = lax.broadcasted_iota(I32, (chunk, 1), 0).astype(F32)
    qdec_f = jnp.exp((pos + 1.0) * lgf)
    kdec_f = jnp.exp((chunk - 1.0 - pos) * lgf)
    qdec_b = jnp.exp((chunk - pos) * lgb)
    kdec_b = jnp.exp(pos * lgb)
    cdec_f = jnp.exp(chunk * lgf)
    cdec_b = jnp.exp(chunk * lgb)
    nt = (((1,), (1,)), ((), ()))
    tn = (((0,), (0,)), ((), ()))

    stf_ref[...] = jnp.zeros_like(stf_ref)
    stb_ref[...] = jnp.zeros_like(stb_ref)

    def chunk_slice(c):
        return pl.ds(pl.multiple_of(c * chunk, chunk), chunk)

    def scaled(x, dec):
        return (x.astype(F32) * dec).astype(BF16)

    def step(t):
        slf = chunk_slice(t)
        slb = chunk_slice(nc - 1 - t)
        qf, kf, vf = q_ref[slf, :], k_ref[slf, :], v_ref[slf, :]
        qb, kb, vb = q_ref[slb, :], k_ref[slb, :], v_ref[slb, :]
        s = lax.dot_general(qf, kf, nt, preferred_element_type=F32) * dmat
        cross_b = jnp.dot(scaled(qb, qdec_b), stb_ref[...].astype(BF16), preferred_element_type=F32)
        inner = jnp.dot(s.astype(BF16), vf, preferred_element_type=F32)
        cross_f = jnp.dot(scaled(qf, qdec_f), stf_ref[...].astype(BF16), preferred_element_type=F32)
        stb_ref[...] = stb_ref[...] * cdec_b + lax.dot_general(scaled(kb, kdec_b), vb, tn, preferred_element_type=F32)
        stf_ref[...] = stf_ref[...] * cdec_f + lax.dot_general(scaled(kf, kdec_f), vf, tn, preferred_element_type=F32)
        return slf, inner + cross_f, slb, cross_b

    def finish(sl, o):
        mu = jnp.mean(o, axis=-1, keepdims=True)
        oc = o - mu
        var = jnp.mean(oc * oc, axis=-1, keepdims=True)
        on = oc * lax.rsqrt(var + NORM_EPS) * gnw_ref[...]
        gate = g_ref[sl, :].astype(F32)
        o_ref[sl, :] = (gate * jax.nn.sigmoid(gate) * on).astype(o_ref.dtype)

    def first_half(t, carry):
        slf, of, slb, ob = step(t)
        acc_ref[slf, :] = of
        acc_ref[slb, :] = ob
        return carry

    def second_half(t, carry):
        slf, of, slb, ob = step(t)
        finish(slf, acc_ref[slf, :] + of)
        finish(slb, acc_ref[slb, :] + ob)
        return carry

    lax.fori_loop(0, nc // 2, first_half, 0)
    lax.fori_loop(nc // 2, nc, second_half, 0)


def _retention(proj, decay_logit, gn_w, *, batch, seq, d_model):
    heads = RET_HEADS
    dk = d_model // heads
    dv = 2 * d_model // heads
    n = batch * seq
    assert seq % (2 * RET_CHUNK) == 0, "the two scans are paired chunk by chunk"
    kern = functools.partial(_retention_kernel, seq=seq, chunk=RET_CHUNK)
    return pl.pallas_call(
        kern,
        grid=(batch, heads),
        in_specs=[
            pl.BlockSpec((2, heads), lambda b, h: (0, 0)),
            pl.BlockSpec((seq, dk), lambda b, h: (b, h)),
            pl.BlockSpec((seq, dk), lambda b, h: (b, heads + h)),
            pl.BlockSpec((seq, dv), lambda b, h: (b, heads + h)),
            pl.BlockSpec((seq, dv), lambda b, h: (b, 2 * heads + h)),
            pl.BlockSpec((1, dv), lambda b, h: (0, h)),
        ],
        out_specs=pl.BlockSpec((seq, dv), lambda b, h: (b, h)),
        out_shape=jax.ShapeDtypeStruct((n, 2 * d_model), BF16),
        scratch_shapes=[pltpu.VMEM((seq, dv), F32), pltpu.VMEM((dk, dv), F32), pltpu.VMEM((dk, dv), F32)],
        compiler_params=_cparams(("parallel", "parallel")),
        name="retention",
    )(decay_logit, proj, proj, proj, proj, gn_w.reshape(1, -1))


def _na_kernel(bias_ref, q_ref, k_ref, v_ref, o_ref, *, rows):
    gw = GRID_W
    kr = min(NA_MAX_ROWS, rows)
    dh = NA_HEAD_DIM
    lane_q = lax.broadcasted_iota(I32, (gw, 2 * dh), 1)
    nt = (((1,), (1,)), ((), ()))

    group = min(NA_ROW_UNROLL, rows)

    def body(it, carry):
        rr = [it * group + i for i in range(group)]
        rs = [jnp.clip(r - kr // 2, 0, rows - kr) for r in rr]
        s = []
        for r, r0 in zip(rr, rs):
            q2 = q_ref[pl.ds(pl.multiple_of(r * gw, gw), gw), :]
            zero = jnp.zeros_like(q2)
            qq = jnp.concatenate([jnp.where(lane_q < dh, q2, zero), jnp.where(lane_q >= dh, q2, zero)], axis=0)
            k2 = k_ref[pl.ds(pl.multiple_of(r0 * gw, gw), kr * gw), :]
            s.append(lax.dot_general(qq, k2, nt, preferred_element_type=F32) + bias_ref[0, r - r0])
        m = [jnp.max(si, axis=-1, keepdims=True) for si in s]
        p = [jnp.exp(si - mi) for si, mi in zip(s, m)]
        inv = [1.0 / jnp.sum(pi, axis=-1, keepdims=True) for pi in p]
        o = [jnp.dot(pi.astype(BF16), v_ref[pl.ds(pl.multiple_of(r0 * gw, gw), kr * gw), :],
                     preferred_element_type=F32) * ii for pi, ii, r0 in zip(p, inv, rs)]
        for r, oi in zip(rr, o):
            out = jnp.where(lane_q < dh, oi[0:gw, :], oi[gw:2 * gw, :])
            o_ref[pl.ds(pl.multiple_of(r * gw, gw), gw), :] = out.astype(o_ref.dtype)
        return carry

    lax.fori_loop(0, rows // group, body, 0)


def _na_bias_table(rpb, rows):
    kr = min(NA_MAX_ROWS, rows)
    c = np.arange(GRID_W)
    kc = np.arange(GRID_W)
    win_start = np.clip(c - NA_WIN_COLS // 2, 0, GRID_W - NA_WIN_COLS)
    valid = (kc[None, :] >= win_start[:, None]) & (kc[None, :] < win_start[:, None] + NA_WIN_COLS)
    dc_idx = np.clip(kc[None, :] - c[:, None] + NA_WIN_COLS - 1, 0, 2 * NA_WIN_COLS - 2)
    delta = np.arange(kr)
    a = np.arange(kr)
    dr_idx = a[None, :] - delta[:, None] + NA_MAX_ROWS - 1
    t = rpb[:, dr_idx][:, :, :, dc_idx]
    t = jnp.where(jnp.asarray(valid)[None, None, None], t.astype(F32), NEG_BIG)
    t = t.transpose(0, 1, 3, 2, 4)
    h = rpb.shape[0]
    t = t.reshape(h // 2, 2, kr, GRID_W, kr * GRID_W).transpose(0, 2, 1, 3, 4)
    return t.reshape(h // 2, kr, 2 * GRID_W, kr * GRID_W)


def _na_attention(qkv, bias, *, batch, seq, d_model):
    n = batch * seq
    rows = seq // GRID_W
    kr = min(NA_MAX_ROWS, rows)
    pairs = NA_HEADS // 2
    pw = 2 * NA_HEAD_DIM
    assert rows % min(NA_ROW_UNROLL, rows) == 0
    kern = functools.partial(_na_kernel, rows=rows)
    return pl.pallas_call(
        kern,
        grid=(batch, pairs),
        in_specs=[
            pl.BlockSpec((1, kr, 2 * GRID_W, kr * GRID_W), lambda b, hp: (hp, 0, 0, 0)),
            pl.BlockSpec((seq, pw), lambda b, hp: (b, hp)),
            pl.BlockSpec((seq, pw), lambda b, hp: (b, pairs + hp)),
            pl.BlockSpec((seq, pw), lambda b, hp: (b, 2 * pairs + hp)),
        ],
        out_specs=pl.BlockSpec((seq, pw), lambda b, hp: (b, hp)),
        out_shape=jax.ShapeDtypeStruct((n, d_model), BF16),
        compiler_params=_cparams(("parallel", "parallel")),
        name="na_attention",
    )(bias, qkv, qkv, qkv)


def _out_router_kernel(a_ref, w_ref, h_ref, g_ref, rw_ref, h1_ref, xn_ref, aff_ref):
    y = jnp.dot(a_ref[...], w_ref[...], preferred_element_type=F32) + h_ref[...]
    h1_ref[...] = y
    xn = _rms(y, g_ref[...])
    xn_ref[...] = xn.astype(xn_ref.dtype)
    nt = (((1,), (1,)), ((), ()))
    ne = rw_ref.shape[0]
    xh = xn.astype(BF16)
    xl = (xn - xh.astype(F32)).astype(BF16)
    rw = rw_ref[...]
    rh = rw.astype(BF16)
    rl = (rw - rh.astype(F32)).astype(BF16)
    t1 = lax.dot_general(jnp.concatenate([rh, rl], axis=0), xh, nt, preferred_element_type=F32)
    t2 = lax.dot_general(rh, xl, nt, preferred_element_type=F32)
    logits = t1[0:ne, :] + (t1[ne:2 * ne, :] + t2)
    m = jnp.max(logits, axis=0, keepdims=True)
    e = jnp.exp(logits - m)
    aff = e / jnp.sum(e, axis=0, keepdims=True)
    for t in range(aff_ref.shape[0]):
        aff_ref[t] = aff[:, t * MOE_TILE:(t + 1) * MOE_TILE]


def _out_router(a, w_bf16, h, g, router_w, *, tm=512):
    n, kdim = a.shape
    d = h.shape[1]
    ne = router_w.shape[1]
    tpb = tm // MOE_TILE
    return pl.pallas_call(
        _out_router_kernel,
        grid=(n // tm,),
        in_specs=[
            pl.BlockSpec((tm, kdim), lambda i: (i, 0)),
            pl.BlockSpec((kdim, d), lambda i: (0, 0)),
            pl.BlockSpec((tm, d), lambda i: (i, 0)),
            pl.BlockSpec((1, d), lambda i: (0, 0)),
            pl.BlockSpec((ne, d), lambda i: (0, 0)),
        ],
        out_specs=[
            pl.BlockSpec((tm, d), lambda i: (i, 0)),
            pl.BlockSpec((tm, d), lambda i: (i, 0)),
            pl.BlockSpec((tpb, ne, MOE_TILE), lambda i: (i, 0, 0)),
        ],
        out_shape=[
            jax.ShapeDtypeStruct((n, d), F32),
            jax.ShapeDtypeStruct((n, d), BF16),
            jax.ShapeDtypeStruct((n // MOE_TILE, ne, MOE_TILE), F32),
        ],
        compiler_params=_cparams(("parallel",)),
        name="out_router",
    )(a, w_bf16, h, g.reshape(1, d), router_w.T)


def _select_kernel(aff_ref, pos_ref, off_ref, *, cap):
    ntile, ne, tt = aff_ref.shape
    nsub = tt // LANES

    def count(pred_fn):
        def body(c, acc):
            x = pltpu.bitcast(aff_ref[c], I32)
            return acc + pred_fn(x).astype(I32)
        acc = lax.fori_loop(0, ntile, body, jnp.zeros((ne, tt), I32))
        return jnp.sum(acc, axis=1, keepdims=True)

    def bit_step(i, t):
        cand = t | jnp.left_shift(jnp.int32(1), 30 - i)
        return jnp.where(count(lambda x: x >= cand) >= cap, cand, t)

    thr = lax.fori_loop(0, 31, bit_step, jnp.zeros((ne, 1), I32))
    need_eq = (cap - count(lambda x: x > thr)).astype(F32)

    li = lax.broadcasted_iota(I32, (LANES, LANES), 0)
    lj = lax.broadcasted_iota(I32, (LANES, LANES), 1)
    upper = (li < lj).astype(BF16)

    def tile_body(c, carry):
        eq_carry, pos_carry = carry
        off_ref[c] = jnp.broadcast_to(pos_carry, (ne, LANES)).astype(I32)
        xt = pltpu.bitcast(aff_ref[c], I32)
        parts = []
        for j in range(nsub):
            x = xt[:, j * LANES:(j + 1) * LANES]
            gt = x > thr
            eq = x == thr
            eqf = eq.astype(F32)
            eq_rank = eq_carry + jnp.dot(eqf.astype(BF16), upper, preferred_element_type=F32)
            sel = gt | (eq & (eq_rank < need_eq))
            eq_carry = eq_carry + jnp.sum(eqf, axis=1, keepdims=True)
            self_ = sel.astype(F32)
            cexc = jnp.dot(self_.astype(BF16), upper, preferred_element_type=F32)
            parts.append(jnp.where(sel, (pos_carry + cexc).astype(I32), -1))
            pos_carry = pos_carry + jnp.sum(self_, axis=1, keepdims=True)
        pos_ref[c] = jnp.concatenate(parts, axis=1)
        return eq_carry, pos_carry

    lax.fori_loop(0, ntile, tile_body, (jnp.zeros((ne, 1), F32), jnp.zeros((ne, 1), F32)))


def _select(aff3, *, cap):
    ntile, ne, tt = aff3.shape
    kern = functools.partial(_select_kernel, cap=cap)
    pos3, off3 = pl.pallas_call(
        kern,
        out_shape=[
            jax.ShapeDtypeStruct((ntile, ne, tt), I32),
            jax.ShapeDtypeStruct((ntile, ne, LANES), I32),
        ],
        compiler_params=pltpu.CompilerParams(vmem_limit_bytes=_VMEM_LIMIT),
        name="ec_select",
    )(aff3)
    offs = jnp.concatenate([off3[:, :, 0], jnp.full((1, ne), cap, I32)], axis=0).reshape(-1)
    return pos3, offs


def _tile_windows(off_ref, s, ne):
    starts, used = [], []
    for e in range(ne):
        off = off_ref[s * ne + e]
        nxt = off_ref[(s + 1) * ne + e]
        st = (off // BF16_ROWS) * BF16_ROWS
        starts.append(st)
        used.append(nxt - st)
    return starts, used


def _num_rounds(used):
    m = used[0]
    for u in used[1:]:
        m = jnp.maximum(m, u)
    return (m + MOE_SLOTS - 1) // MOE_SLOTS


def _dispatch_kernel(off_ref, x_ref, pos_ref, xe_ref, wins_ref, carry_ref, sem, nout_ref, *, cap):
    s = pl.program_id(0)
    ne = pos_ref.shape[1]
    tt = x_ref.shape[0]
    gr = BF16_ROWS
    gpr = MOE_SLOTS // gr

    @pl.when(s == 0)
    def _():
        carry_ref[...] = jnp.zeros_like(carry_ref)
        nout_ref[0] = 0
        nout_ref[1] = 0

    starts, used = _tile_windows(off_ref, s, ne)
    pos = pos_ref[0]
    x = x_ref[...]
    slot = lax.broadcasted_iota(I32, (MOE_SLOTS, tt), 0)

    def drain(b):
        def wait_one(i, c):
            pltpu.make_async_copy(wins_ref.at[b, pl.ds(0, gr)], xe_ref.at[pl.ds(0, gr)], sem.at[b]).wait()
            return c
        lax.fori_loop(0, nout_ref[b], wait_one, 0)
        nout_ref[b] = 0

    def round_body(w, c):
        b = (s + w) % 2
        drain(b)
        pieces = []
        for e in range(ne):
            key = pos[e:e + 1, :] - (starts[e] + w * MOE_SLOTS)
            pieces.append((jnp.broadcast_to(key, (MOE_SLOTS, tt)) == slot).astype(F32))
        onehot = jnp.concatenate(pieces, axis=0).astype(BF16)
        wins_ref[b] = jnp.dot(onehot, x, preferred_element_type=F32).astype(BF16)

        @pl.when(w == 0)
        def _():
            for e in range(ne):
                rows = pl.ds(e * MOE_SLOTS, gr)
                wins_ref[b, rows, :] = (wins_ref[b, rows, :].astype(F32)
                                        + carry_ref[pl.ds(e * gr, gr), :].astype(F32)).astype(BF16)

        issued = 0
        for e in range(ne):
            ngc = used[e] // gr
            rem = used[e] % gr
            ng_w = jnp.clip(ngc - w * gpr, 0, gpr)

            def issue(j, c2, e=e):
                src = wins_ref.at[b, pl.ds(e * MOE_SLOTS + j * gr, gr)]
                dst = xe_ref.at[pl.ds(pl.multiple_of(e * cap + starts[e] + w * MOE_SLOTS + j * gr, gr), gr)]
                pltpu.make_async_copy(src, dst, sem.at[b]).start()
                return c2

            lax.fori_loop(0, ng_w, issue, 0)
            issued = issued + ng_w

            @pl.when((rem > 0) & (ngc // gpr == w))
            def _(e=e, ngc=ngc):
                carry_ref[pl.ds(e * gr, gr), :] = wins_ref[b, pl.ds(e * MOE_SLOTS + (ngc % gpr) * gr, gr), :]

            @pl.when((rem == 0) & (w == 0))
            def _(e=e):
                carry_ref[pl.ds(e * gr, gr), :] = jnp.zeros((gr, carry_ref.shape[1]), carry_ref.dtype)

        nout_ref[b] = issued
        return c

    lax.fori_loop(0, _num_rounds(used), round_body, 0)

    @pl.when(s == pl.num_programs(0) - 1)
    def _():
        drain(0)
        drain(1)


def _dispatch(offs, xn, pos3, *, cap):
    n, d = xn.shape
    ntile, ne, tt = pos3.shape
    kern = functools.partial(_dispatch_kernel, cap=cap)
    return pl.pallas_call(
        kern,
        grid_spec=pltpu.PrefetchScalarGridSpec(
            num_scalar_prefetch=1,
            grid=(ntile,),
            in_specs=[
                pl.BlockSpec((tt, d), lambda i, off: (i, 0)),
                pl.BlockSpec((1, ne, tt), lambda i, off: (i, 0, 0)),
            ],
            out_specs=pl.BlockSpec(memory_space=pl.ANY),
            scratch_shapes=[
                pltpu.VMEM((2, ne * MOE_SLOTS, d), BF16),
                pltpu.VMEM((ne * BF16_ROWS, d), BF16),
                pltpu.SemaphoreType.DMA((2,)),
                pltpu.SMEM((2,), I32),
            ],
        ),
        out_shape=jax.ShapeDtypeStruct((ne * cap, d), BF16),
        compiler_params=_cparams(("arbitrary",)),
        name="ec_dispatch",
    )(offs, xn, pos3)


def _ffn_kernel(x_ref, wg_ref, wu_ref, wd_ref, o_ref):
    x = x_ref[...]
    a = jnp.dot(x, wg_ref[0], preferred_element_type=F32)
    u = jnp.dot(x, wu_ref[0], preferred_element_type=F32)
    hid = (a * jax.nn.sigmoid(a) * u).astype(BF16)
    o_ref[...] = jnp.dot(hid, wd_ref[0], preferred_element_type=F32).astype(o_ref.dtype)


def _expert_ffn(xe, wg, wu, wd, *, cap, tr=512):
    ne, d, f = wg.shape
    tr = min(tr, cap)
    nt = cap // tr
    return pl.pallas_call(
        _ffn_kernel,
        grid=(ne, nt),
        in_specs=[
            pl.BlockSpec((tr, d), lambda e, t: (e * nt + t, 0)),
            pl.BlockSpec((1, d, f), lambda e, t: (e, 0, 0)),
            pl.BlockSpec((1, d, f), lambda e, t: (e, 0, 0)),
            pl.BlockSpec((1, f, d), lambda e, t: (e, 0, 0)),
        ],
        out_specs=pl.BlockSpec((tr, d), lambda e, t: (e * nt + t, 0)),
        out_shape=jax.ShapeDtypeStruct((ne * cap, d), BF16),
        compiler_params=_cparams(("parallel", "parallel")),
        name="expert_ffn",
    )(xe, wg, wu, wd)


def _combine_kernel(off_ref, h_ref, aff_ref, pos_ref, p_ref, ye_ref, png_ref, wpg_ref, wpu_ref, fng_ref,
                    o_ref, wins_ref, acc_ref, sem, *, cap, final_norm):
    s = pl.program_id(0)
    ntile = pl.num_programs(0)
    ne = pos_ref.shape[1]
    tt = h_ref.shape[0]
    spill_buf = 2

    def window_copies(starts_t, w, buf):
        copies, wstarts = [], []
        for e in range(ne):
            ws = pl.multiple_of(jnp.minimum(starts_t[e] + w * MOE_SLOTS, cap - MOE_SLOTS), BF16_ROWS)
            copies.append(pltpu.make_async_copy(
                ye_ref.at[pl.ds(pl.multiple_of(e * cap + ws, BF16_ROWS), MOE_SLOTS)],
                wins_ref.at[buf, pl.ds(e * MOE_SLOTS, MOE_SLOTS)], sem.at[buf]))
            wstarts.append(ws)
        return copies, wstarts

    starts, used = _tile_windows(off_ref, s, ne)
    cur = s % 2

    @pl.when(s == 0)
    def _():
        for cp in window_copies(starts, 0, cur)[0]:
            cp.start()

    @pl.when(s + 1 < ntile)
    def _():
        nxt_starts, _ = _tile_windows(off_ref, s + 1, ne)
        for cp in window_copies(nxt_starts, 0, 1 - cur)[0]:
            cp.start()

    pos = pos_ref[0]
    aff = aff_ref[0]
    slot = lax.broadcasted_iota(I32, (MOE_SLOTS, tt), 0)

    def gates(w, wstarts):
        pieces = []
        for e in range(ne):
            base = starts[e] + w * MOE_SLOTS
            pe = pos[e:e + 1, :]
            in_round = (pe >= base) & (pe < base + MOE_SLOTS)
            hit = (jnp.broadcast_to(pe - wstarts[e], (MOE_SLOTS, tt)) == slot) & jnp.broadcast_to(in_round, (MOE_SLOTS, tt))
            pieces.append(jnp.where(hit, jnp.broadcast_to(aff[e:e + 1, :], (MOE_SLOTS, tt)), 0.0))
        return jnp.concatenate(pieces, axis=0).astype(BF16)

    tn = (((0,), (0,)), ((), ()))
    copies0, wstarts0 = window_copies(starts, 0, cur)
    a0 = gates(0, wstarts0)
    for cp in copies0:
        cp.wait()
    acc_ref[...] = lax.dot_general(a0, wins_ref[cur], tn, preferred_element_type=F32)

    def round_body(w, c):
        copies, wstarts = window_copies(starts, w, spill_buf)
        for cp in copies:
            cp.start()
        a_w = gates(w, wstarts)
        for cp in copies:
            cp.wait()
        acc_ref[...] += lax.dot_general(a_w, wins_ref[spill_buf], tn, preferred_element_type=F32)
        return c

    lax.fori_loop(1, _num_rounds(used), round_body, 0)
    h2 = h_ref[...] + acc_ref[...]
    gate = jax.nn.sigmoid(jnp.dot(_rms(h2, png_ref[...]).astype(BF16), wpg_ref[...], preferred_element_type=F32))
    up = jnp.dot(p_ref[0].astype(BF16), wpu_ref[...], preferred_element_type=F32)
    h3 = h2 + up * gate
    if final_norm:
        h3 = _rms(h3, fng_ref[...])
    o_ref[...] = h3


def _combine(offs, h1, aff3, pos3, p_all, ye, ple_norm, wpg, wpu, final_g, *, layer, cap, final_norm):
    n, d = h1.shape
    ntile, ne, tt = pos3.shape
    pd = p_all.shape[2]
    kern = functools.partial(_combine_kernel, cap=cap, final_norm=final_norm)
    return pl.pallas_call(
        kern,
        grid_spec=pltpu.PrefetchScalarGridSpec(
            num_scalar_prefetch=1,
            grid=(ntile,),
            in_specs=[
                pl.BlockSpec((tt, d), lambda i, off: (i, 0)),
                pl.BlockSpec((1, ne, tt), lambda i, off: (i, 0, 0)),
                pl.BlockSpec((1, ne, tt), lambda i, off: (i, 0, 0)),
                pl.BlockSpec((1, tt, pd), lambda i, off: (layer, i, 0)),
                pl.BlockSpec(memory_space=pl.ANY),
                pl.BlockSpec((1, d), lambda i, off: (0, 0)),
                pl.BlockSpec((d, d), lambda i, off: (0, 0)),
                pl.BlockSpec((pd, d), lambda i, off: (0, 0)),
                pl.BlockSpec((1, d), lambda i, off: (0, 0)),
            ],
            out_specs=pl.BlockSpec((tt, d), lambda i, off: (i, 0)),
            scratch_shapes=[pltpu.VMEM((3, ne * MOE_SLOTS, d), BF16), pltpu.VMEM((tt, d), F32),
                            pltpu.SemaphoreType.DMA((3,))],
        ),
        out_shape=jax.ShapeDtypeStruct((n, d), F32),
        compiler_params=_cparams(("arbitrary",)),
        name="combine_ple",
    )(offs, h1, aff3, pos3, p_all, ye, ple_norm.reshape(1, d), wpg, wpu, final_g.reshape(1, d))


def _moe_block(h1, xn, aff3, p_all, w, i, *, final_norm):
    n, d = h1.shape
    cap = EC_CAPACITY * n // N_EXPERTS
    pos3, offs = _select(aff3, cap=cap)
    xe = _dispatch(offs, xn, pos3, cap=cap)
    ye = _expert_ffn(xe, w["exp_w_gate"][i], w["exp_w_up"][i], w["exp_w_down"][i], cap=cap)
    return _combine(offs, h1, aff3, pos3, p_all, ye, w["ple_norm"][i], w["ple_w_gate"][i], w["ple_w_up"][i],
                    w["final_norm"], layer=i, cap=cap, final_norm=final_norm)


def _rope_tables(seq, dim):
    inv = ROPE_THETA ** (-jnp.arange(0, dim, 2, dtype=F32) / dim)
    ang = jnp.arange(seq, dtype=F32)[:, None] * inv[None, :]
    return jnp.cos(ang), jnp.sin(ang)


def _trunk(x, p, w):
    batch, seq, d = x.shape
    n = batch * seq
    h = x.reshape(n, d)
    dk = d // RET_HEADS
    cos, sin = _rope_tables(seq, dk)

    proj = _in_proj(h, w["mix_norm"][0], w["ret_w_in"][0], cos, sin, seq=seq, rope_tiles=2,
                    tile_scales=(1.0, float(dk) ** -0.5), tn=d)
    a = _retention(proj, w["ret_decay_logit"][0], w["ret_gn_w"][0], batch=batch, seq=seq, d_model=d)
    h1, xn, aff3 = _out_router(a, w["ret_w_out"][0], h, w["ffn_norm"][0], w["router_w"][0])
    p_all = p.reshape(p.shape[0], n, p.shape[-1])
    h = _moe_block(h1, xn, aff3, p_all, w, 0, final_norm=False)

    qkv = _in_proj(h, w["mix_norm"][1], w["na_w_in"][0], cos, sin, seq=seq, rope_tiles=0,
                   tile_scales=(float(NA_HEAD_DIM) ** -0.5,), tn=d)
    bias = _na_bias_table(w["na_rpb"][0], seq // GRID_W)
    a = _na_attention(qkv, bias, batch=batch, seq=seq, d_model=d)
    h1, xn, aff3 = _out_router(a, w["na_w_out"][0], h, w["ffn_norm"][1], w["router_w"][1])
    y = _moe_block(h1, xn, aff3, p_all, w, 1, final_norm=True)
    return y.reshape(batch, seq, d)


def kernel(x_prompt, x_sample, p_prompt, p_sample, ret_w_in, ret_decay_logit, ret_gn_w, ret_w_out, na_w_in, na_rpb, na_w_out, mix_norm, ffn_norm, ple_norm, router_w, exp_w_gate, exp_w_up, exp_w_down, ple_w_up, ple_w_gate, final_norm):
    w = dict(
        ret_w_in=ret_w_in.astype(BF16), ret_decay_logit=ret_decay_logit, ret_gn_w=ret_gn_w,
        ret_w_out=ret_w_out.astype(BF16), na_w_in=na_w_in.astype(BF16), na_rpb=na_rpb,
        na_w_out=na_w_out.astype(BF16), mix_norm=mix_norm, ffn_norm=ffn_norm, ple_norm=ple_norm,
        router_w=router_w, exp_w_gate=exp_w_gate.astype(BF16), exp_w_up=exp_w_up.astype(BF16),
        exp_w_down=exp_w_down.astype(BF16), ple_w_up=ple_w_up.astype(BF16), ple_w_gate=ple_w_gate.astype(BF16),
        final_norm=final_norm,
    )
    return _trunk(x_prompt, p_prompt, w), _trunk(x_sample, p_sample, w)
```

```python
import functools

import numpy as np
import jax
import jax.numpy as jnp
from jax import lax
from jax.experimental import pallas as pl
from jax.experimental.pallas import tpu as pltpu

F32 = jnp.float32
BF16 = jnp.bfloat16
I32 = jnp.int32

NORM_EPS = 1e-6
ROPE_THETA = 10000.0
GRID_W = 64
RET_HEADS = 4
RET_CHUNK = 256
NA_HEADS = 16
NA_HEAD_DIM = 64
NA_MAX_ROWS = 8
NA_WIN_COLS = 16
NA_ROW_UNROLL = 8
N_EXPERTS = 16
EC_CAPACITY = 2
LANES = 128
BF16_ROWS = 16
MOE_TILE = 256
MOE_SLOTS = 64
COMBINE_TILES = 2
NEG_BIG = -1e30

_VMEM_LIMIT = 56 * 1024 * 1024


def _cparams(sem):
    return pltpu.CompilerParams(dimension_semantics=sem, vmem_limit_bytes=_VMEM_LIMIT)


def _rms(x, g):
    return x * lax.rsqrt(jnp.mean(x * x, axis=-1, keepdims=True) + NORM_EPS) * g


def _in_proj_kernel(x_ref, g_ref, w_ref, cos_ref, sin_ref, o_ref, xn_ref, *, nj, unit_w, rope_units, unit_scales,
                    rope_dim):
    j = pl.program_id(1)

    @pl.when(j == 0)
    def _():
        xn_ref[...] = _rms(x_ref[...], g_ref[...]).astype(BF16)

    upb = w_ref.shape[1] // unit_w
    half = rope_dim // 2

    def emit(y, c, u):
        base = c * unit_w
        if unit_scales[u] != 1.0:
            y = y * unit_scales[u]
        if u < rope_units:
            cos = cos_ref[...]
            sin = sin_ref[...]
            for hh in range(unit_w // rope_dim):
                lo = hh * rope_dim
                x1 = y[:, lo:lo + half]
                x2 = y[:, lo + half:lo + rope_dim]
                o_ref[:, base + lo:base + lo + half] = (x1 * cos - x2 * sin).astype(o_ref.dtype)
                o_ref[:, base + lo + half:base + lo + rope_dim] = (x1 * sin + x2 * cos).astype(o_ref.dtype)
        else:
            o_ref[:, base:base + unit_w] = y.astype(o_ref.dtype)

    for c in range(upb):
        y = jnp.dot(xn_ref[...], w_ref[:, c * unit_w:(c + 1) * unit_w], preferred_element_type=F32)
        kinds = [(jj * upb + c < rope_units, unit_scales[jj * upb + c]) for jj in range(nj)]
        if all(k == kinds[0] for k in kinds):
            emit(y, c, c)
        else:
            for jj in range(nj):
                pl.when(j == jj)(functools.partial(emit, y, c, jj * upb + c))


def _in_proj(x, g, w_bf16, cos, sin, *, seq, rope_units, unit_scales, tm=1024, units_per_step=3):
    n, d = x.shape
    tm = min(tm, seq)
    ncol = w_bf16.shape[1]
    tn = units_per_step * d
    nj = ncol // tn
    assert len(unit_scales) == ncol // d
    rope_dim = 2 * cos.shape[1]
    nseq = seq // tm
    kern = functools.partial(_in_proj_kernel, nj=nj, unit_w=d, rope_units=rope_units, unit_scales=unit_scales,
                             rope_dim=rope_dim)
    return pl.pallas_call(
        kern,
        grid=(n // tm, ncol // tn),
        in_specs=[
            pl.BlockSpec((tm, d), lambda i, j: (i, 0)),
            pl.BlockSpec((1, d), lambda i, j: (0, 0)),
            pl.BlockSpec((d, tn), lambda i, j: (0, j)),
            pl.BlockSpec((tm, cos.shape[1]), lambda i, j: (i % nseq, 0)),
            pl.BlockSpec((tm, cos.shape[1]), lambda i, j: (i % nseq, 0)),
        ],
        out_specs=pl.BlockSpec((tm, tn), lambda i, j: (i, j)),
        out_shape=jax.ShapeDtypeStruct((n, ncol), BF16),
        scratch_shapes=[pltpu.VMEM((tm, d), BF16)],
        compiler_params=_cparams(("parallel", "arbitrary")),
        name="in_proj",
    )(x, g.reshape(1, d), w_bf16, cos, sin)


def _retention_kernel(dl_ref, q_ref, k_ref, v_ref, g_ref, gnw_ref, o_ref, acc_ref, stf_ref, stb_ref, *, seq, chunk):
    h = pl.program_id(1)
    nc = seq // chunk
    dl = dl_ref[...]
    lg = jnp.minimum(dl, 0.0) - jnp.log1p(jnp.exp(-jnp.abs(dl)))
    col = lax.broadcasted_iota(I32, dl.shape, 1)
    lgh = jnp.sum(jnp.where(col == h, lg, 0.0), axis=1, keepdims=True)
    lgf = lgh[0:1, :]
    lgb = lgh[1:2, :]

    ri = lax.broadcasted_iota(I32, (chunk, chunk), 0)
    ci = lax.broadcasted_iota(I32, (chunk, chunk), 1)
    diff = (ri - ci).astype(F32)
    dmat = jnp.exp(jnp.where(diff >= 0, diff * lgf, -diff * lgb))
    pos = lax.broadcasted_iota(I32, (chunk, 1), 0).astype(F32)
    qdec_f = jnp.exp((pos + 1.0) * lgf)
    kdec_f = jnp.exp((chunk - 1.0 - pos) * lgf)
    qdec_b = jnp.exp((chunk - pos) * lgb)
    kdec_b = jnp.exp(pos * lgb)
    cdec_f = jnp.exp(chunk * lgf)
    cdec_b = jnp.exp(chunk * lgb)
    nt = (((1,), (1,)), ((), ()))
    tn = (((0,), (0,)), ((), ()))

    stf_ref[...] = jnp.zeros_like(stf_ref)
    stb_ref[...] = jnp.zeros_like(stb_ref)

    def chunk_slice(c):
        return pl.ds(pl.multiple_of(c * chunk, chunk), chunk)

    def scaled(x, dec):
        return (x.astype(F32) * dec).astype(BF16)

    def step(t):
        slf = chunk_slice(t)
        slb = chunk_slice(nc - 1 - t)
        qf, kf, vf = q_ref[slf, :], k_ref[slf, :], v_ref[slf, :]
        qb, kb, vb = q_ref[slb, :], k_ref[slb, :], v_ref[slb, :]
        s = lax.dot_general(qf, kf, nt, preferred_element_type=F32) * dmat
        cross_b = jnp.dot(scaled(qb, qdec_b), stb_ref[...].astype(BF16), preferred_element_type=F32)
        inner = jnp.dot(s.astype(BF16), vf, preferred_element_type=F32)
        cross_f = jnp.dot(scaled(qf, qdec_f), stf_ref[...].astype(BF16), preferred_element_type=F32)
        stb_ref[...] = stb_ref[...] * cdec_b + lax.dot_general(scaled(kb, kdec_b), vb, tn, preferred_element_type=F32)
        stf_ref[...] = stf_ref[...] * cdec_f + lax.dot_general(scaled(kf, kdec_f), vf, tn, preferred_element_type=F32)
        return slf, inner + cross_f, slb, cross_b

    def finish(sl, o):
        mu = jnp.mean(o, axis=-1, keepdims=True)
        oc = o - mu
        var = jnp.mean(oc * oc, axis=-1, keepdims=True)
        on = oc * lax.rsqrt(var + NORM_EPS) * gnw_ref[...]
        gate = g_ref[sl, :].astype(F32)
        o_ref[sl, :] = (gate * jax.nn.sigmoid(gate) * on).astype(o_ref.dtype)

    def first_half(t, carry):
        slf, of, slb, ob = step(t)
        acc_ref[slf, :] = of
        acc_ref[slb, :] = ob
        return carry

    def second_half(t, carry):
        slf, of, slb, ob = step(t)
        finish(slf, acc_ref[slf, :] + of)
        finish(slb, acc_ref[slb, :] + ob)
        return carry

    lax.fori_loop(0, nc // 2, first_half, 0)
    lax.fori_loop(nc // 2, nc, second_half, 0)


def _retention(proj, decay_logit, gn_w, *, batch, seq, d_model):
    heads = RET_HEADS
    dk = d_model // heads
    dv = 2 * d_model // heads
    n = batch * seq
    assert seq % (2 * RET_CHUNK) == 0, "the two scans are paired chunk by chunk"
    kern = functools.partial(_retention_kernel, seq=seq, chunk=RET_CHUNK)
    return pl.pallas_call(
        kern,
        grid=(batch, heads),
        in_specs=[
            pl.BlockSpec((2, heads), lambda b, h: (0, 0)),
            pl.BlockSpec((seq, dk), lambda b, h: (b, h)),
            pl.BlockSpec((seq, dk), lambda b, h: (b, heads + h)),
            pl.BlockSpec((seq, dv), lambda b, h: (b, heads + h)),
            pl.BlockSpec((seq, dv), lambda b, h: (b, 2 * heads + h)),
            pl.BlockSpec((1, dv), lambda b, h: (0, h)),
        ],
        out_specs=pl.BlockSpec((seq, dv), lambda b, h: (b, h)),
        out_shape=jax.ShapeDtypeStruct((n, 2 * d_model), BF16),
        scratch_shapes=[pltpu.VMEM((seq, dv), F32), pltpu.VMEM((dk, dv), F32), pltpu.VMEM((dk, dv), F32)],
        compiler_params=_cparams(("parallel", "parallel")),
        name="retention",
    )(decay_logit, proj, proj, proj, proj, gn_w.reshape(1, -1))


def _na_kernel(bias_ref, q_ref, k_ref, v_ref, o_ref, *, rows):
    gw = GRID_W
    kr = min(NA_MAX_ROWS, rows)
    dh = NA_HEAD_DIM
    lane_q = lax.broadcasted_iota(I32, (gw, 2 * dh), 1)
    nt = (((1,), (1,)), ((), ()))

    group = min(NA_ROW_UNROLL, rows)

    def body(it, carry):
        rr = [it * group + i for i in range(group)]
        rs = [jnp.clip(r - kr // 2, 0, rows - kr) for r in rr]
        s = []
        for r, r0 in zip(rr, rs):
            q2 = q_ref[pl.ds(pl.multiple_of(r * gw, gw), gw), :]
            zero = jnp.zeros_like(q2)
            qq = jnp.concatenate([jnp.where(lane_q < dh, q2, zero), jnp.where(lane_q >= dh, q2, zero)], axis=0)
            k2 = k_ref[pl.ds(pl.multiple_of(r0 * gw, gw), kr * gw), :]
            s.append(lax.dot_general(qq, k2, nt, preferred_element_type=F32) + bias_ref[0, r - r0])
        m = [jnp.max(si, axis=-1, keepdims=True) for si in s]
        p = [jnp.exp(si - mi) for si, mi in zip(s, m)]
        inv = [1.0 / jnp.sum(pi, axis=-1, keepdims=True) for pi in p]
        o = [jnp.dot(pi.astype(BF16), v_ref[pl.ds(pl.multiple_of(r0 * gw, gw), kr * gw), :],
                     preferred_element_type=F32) * ii for pi, ii, r0 in zip(p, inv, rs)]
        for r, oi in zip(rr, o):
            out = jnp.where(lane_q < dh, oi[0:gw, :], oi[gw:2 * gw, :])
            o_ref[pl.ds(pl.multiple_of(r * gw, gw), gw), :] = out.astype(o_ref.dtype)
        return carry

    lax.fori_loop(0, rows // group, body, 0)


def _na_bias_table(rpb, rows):
    kr = min(NA_MAX_ROWS, rows)
    c = np.arange(GRID_W)
    kc = np.arange(GRID_W)
    win_start = np.clip(c - NA_WIN_COLS // 2, 0, GRID_W - NA_WIN_COLS)
    valid = (kc[None, :] >= win_start[:, None]) & (kc[None, :] < win_start[:, None] + NA_WIN_COLS)
    dc_idx = np.clip(kc[None, :] - c[:, None] + NA_WIN_COLS - 1, 0, 2 * NA_WIN_COLS - 2)
    delta = np.arange(kr)
    a = np.arange(kr)
    dr_idx = a[None, :] - delta[:, None] + NA_MAX_ROWS - 1
    t = rpb[:, dr_idx][:, :, :, dc_idx]
    t = jnp.where(jnp.asarray(valid)[None, None, None], t.astype(F32), NEG_BIG)
    t = t.transpose(0, 1, 3, 2, 4)
    h = rpb.shape[0]
    t = t.reshape(h // 2, 2, kr, GRID_W, kr * GRID_W).transpose(0, 2, 1, 3, 4)
    return t.reshape(h // 2, kr, 2 * GRID_W, kr * GRID_W)


def _na_attention(qkv, bias, *, batch, seq, d_model):
    n = batch * seq
    rows = seq // GRID_W
    kr = min(NA_MAX_ROWS, rows)
    pairs = NA_HEADS // 2
    pw = 2 * NA_HEAD_DIM
    assert rows % min(NA_ROW_UNROLL, rows) == 0
    kern = functools.partial(_na_kernel, rows=rows)
    return pl.pallas_call(
        kern,
        grid=(batch, pairs),
        in_specs=[
            pl.BlockSpec((1, kr, 2 * GRID_W, kr * GRID_W), lambda b, hp: (hp, 0, 0, 0)),
            pl.BlockSpec((seq, pw), lambda b, hp: (b, hp)),
            pl.BlockSpec((seq, pw), lambda b, hp: (b, pairs + hp)),
            pl.BlockSpec((seq, pw), lambda b, hp: (b, 2 * pairs + hp)),
        ],
        out_specs=pl.BlockSpec((seq, pw), lambda b, hp: (b, hp)),
        out_shape=jax.ShapeDtypeStruct((n, d_model), BF16),
        compiler_params=_cparams(("parallel", "parallel")),
        name="na_attention",
    )(bias, qkv, qkv, qkv)


def _out_router_kernel(a_ref, w_ref, h_ref, g_ref, rw_ref, h1_ref, xn_ref, aff_ref):
    y = jnp.dot(a_ref[...], w_ref[...], preferred_element_type=F32) + h_ref[...]
    h1_ref[...] = y
    xn = _rms(y, g_ref[...])
    xn_ref[...] = xn.astype(xn_ref.dtype)
    nt = (((1,), (1,)), ((), ()))
    ne = rw_ref.shape[0]
    xh = xn.astype(BF16)
    xl = (xn - xh.astype(F32)).astype(BF16)
    rw = rw_ref[...]
    rh = rw.astype(BF16)
    rl = (rw - rh.astype(F32)).astype(BF16)
    t1 = lax.dot_general(jnp.concatenate([rh, rl], axis=0), xh, nt, preferred_element_type=F32)
    t2 = lax.dot_general(rh, xl, nt, preferred_element_type=F32)
    logits = t1[0:ne, :] + (t1[ne:2 * ne, :] + t2)
    m = jnp.max(logits, axis=0, keepdims=True)
    e = jnp.exp(logits - m)
    aff = e / jnp.sum(e, axis=0, keepdims=True)
    for t in range(aff_ref.shape[0]):
        aff_ref[t] = aff[:, t * MOE_TILE:(t + 1) * MOE_TILE]


def _out_router(a, w_bf16, h, g, router_w, *, tm=1024):
    n, kdim = a.shape
    tm = min(tm, n)
    d = h.shape[1]
    ne = router_w.shape[1]
    tpb = tm // MOE_TILE
    return pl.pallas_call(
        _out_router_kernel,
        grid=(n // tm,),
        in_specs=[
            pl.BlockSpec((tm, kdim), lambda i: (i, 0)),
            pl.BlockSpec((kdim, d), lambda i: (0, 0)),
            pl.BlockSpec((tm, d), lambda i: (i, 0)),
            pl.BlockSpec((1, d), lambda i: (0, 0)),
            pl.BlockSpec((ne, d), lambda i: (0, 0)),
        ],
        out_specs=[
            pl.BlockSpec((tm, d), lambda i: (i, 0)),
            pl.BlockSpec((tm, d), lambda i: (i, 0)),
            pl.BlockSpec((tpb, ne, MOE_TILE), lambda i: (i, 0, 0)),
        ],
        out_shape=[
            jax.ShapeDtypeStruct((n, d), F32),
            jax.ShapeDtypeStruct((n, d), BF16),
            jax.ShapeDtypeStruct((n // MOE_TILE, ne, MOE_TILE), F32),
        ],
        compiler_params=_cparams(("parallel",)),
        name="out_router",
    )(a, w_bf16, h, g.reshape(1, d), router_w.T)


def _select_kernel(aff_ref, pos_ref, off_ref, *, cap):
    ntile, ne, tt = aff_ref.shape
    nsub = tt // LANES

    def count(pred_fn):
        def body(c, acc):
            x = pltpu.bitcast(aff_ref[c], I32)
            return acc + pred_fn(x).astype(I32)
        acc = lax.fori_loop(0, ntile, body, jnp.zeros((ne, tt), I32))
        return jnp.sum(acc, axis=1, keepdims=True)

    def bit_step(i, t):
        cand = t | jnp.left_shift(jnp.int32(1), 30 - i)
        return jnp.where(count(lambda x: x >= cand) >= cap, cand, t)

    thr = lax.fori_loop(0, 31, bit_step, jnp.zeros((ne, 1), I32))
    need_eq = (cap - count(lambda x: x > thr)).astype(F32)

    li = lax.broadcasted_iota(I32, (LANES, LANES), 0)
    lj = lax.broadcasted_iota(I32, (LANES, LANES), 1)
    upper = (li < lj).astype(BF16)

    def tile_body(c, carry):
        eq_carry, pos_carry = carry
        off_ref[c] = jnp.broadcast_to(pos_carry, (ne, LANES)).astype(I32)
        xt = pltpu.bitcast(aff_ref[c], I32)
        parts = []
        for j in range(nsub):
            x = xt[:, j * LANES:(j + 1) * LANES]
            gt = x > thr
            eq = x == thr
            eqf = eq.astype(F32)
            eq_rank = eq_carry + jnp.dot(eqf.astype(BF16), upper, preferred_element_type=F32)
            sel = gt | (eq & (eq_rank < need_eq))
            eq_carry = eq_carry + jnp.sum(eqf, axis=1, keepdims=True)
            self_ = sel.astype(F32)
            cexc = jnp.dot(self_.astype(BF16), upper, preferred_element_type=F32)
            parts.append(jnp.where(sel, (pos_carry + cexc).astype(I32), -1))
            pos_carry = pos_carry + jnp.sum(self_, axis=1, keepdims=True)
        pos_ref[c] = jnp.concatenate(parts, axis=1)
        return eq_carry, pos_carry

    lax.fori_loop(0, ntile, tile_body, (jnp.zeros((ne, 1), F32), jnp.zeros((ne, 1), F32)))


def _select(aff3, *, cap):
    ntile, ne, tt = aff3.shape
    kern = functools.partial(_select_kernel, cap=cap)
    pos3, off3 = pl.pallas_call(
        kern,
        out_shape=[
            jax.ShapeDtypeStruct((ntile, ne, tt), I32),
            jax.ShapeDtypeStruct((ntile, ne, LANES), I32),
        ],
        compiler_params=pltpu.CompilerParams(vmem_limit_bytes=_VMEM_LIMIT),
        name="ec_select",
    )(aff3)
    offs = jnp.concatenate([off3[:, :, 0], jnp.full((1, ne), cap, I32)], axis=0).reshape(-1)
    return pos3, offs


def _tile_windows(off_ref, s, ne):
    starts, used = [], []
    for e in range(ne):
        off = off_ref[s * ne + e]
        nxt = off_ref[(s + 1) * ne + e]
        st = (off // BF16_ROWS) * BF16_ROWS
        starts.append(st)
        used.append(nxt - st)
    return starts, used


def _num_rounds(used):
    m = used[0]
    for u in used[1:]:
        m = jnp.maximum(m, u)
    return (m + MOE_SLOTS - 1) // MOE_SLOTS


def _dispatch_kernel(off_ref, x_ref, pos_ref, xe_ref, wins_ref, carry_ref, sem, nout_ref, *, cap):
    s = pl.program_id(0)
    ne = pos_ref.shape[1]
    tt = x_ref.shape[0]
    gr = BF16_ROWS
    gpr = MOE_SLOTS // gr

    @pl.when(s == 0)
    def _():
        carry_ref[...] = jnp.zeros_like(carry_ref)
        nout_ref[0] = 0
        nout_ref[1] = 0

    starts, used = _tile_windows(off_ref, s, ne)
    pos = pos_ref[0]
    x = x_ref[...]
    slot = lax.broadcasted_iota(I32, (MOE_SLOTS, tt), 0)

    def drain(b):
        def wait_one(i, c):
            pltpu.make_async_copy(wins_ref.at[b, pl.ds(0, gr)], xe_ref.at[pl.ds(0, gr)], sem.at[b]).wait()
            return c
        lax.fori_loop(0, nout_ref[b], wait_one, 0)
        nout_ref[b] = 0

    def round_body(w, c):
        b = (s + w) % 2
        drain(b)
        pieces = []
        for e in range(ne):
            key = pos[e:e + 1, :] - (starts[e] + w * MOE_SLOTS)
            pieces.append((jnp.broadcast_to(key, (MOE_SLOTS, tt)) == slot).astype(F32))
        onehot = jnp.concatenate(pieces, axis=0).astype(BF16)
        wins_ref[b] = jnp.dot(onehot, x, preferred_element_type=F32).astype(BF16)

        @pl.when(w == 0)
        def _():
            for e in range(ne):
                rows = pl.ds(e * MOE_SLOTS, gr)
                wins_ref[b, rows, :] = (wins_ref[b, rows, :].astype(F32)
                                        + carry_ref[pl.ds(e * gr, gr), :].astype(F32)).astype(BF16)

        issued = 0
        for e in range(ne):
            ngc = used[e] // gr
            rem = used[e] % gr
            ng_w = jnp.clip(ngc - w * gpr, 0, gpr)

            def issue(j, c2, e=e):
                src = wins_ref.at[b, pl.ds(e * MOE_SLOTS + j * gr, gr)]
                dst = xe_ref.at[pl.ds(pl.multiple_of(e * cap + starts[e] + w * MOE_SLOTS + j * gr, gr), gr)]
                pltpu.make_async_copy(src, dst, sem.at[b]).start()
                return c2

            lax.fori_loop(0, ng_w, issue, 0)
            issued = issued + ng_w

            @pl.when((rem > 0) & (ngc // gpr == w))
            def _(e=e, ngc=ngc):
                carry_ref[pl.ds(e * gr, gr), :] = wins_ref[b, pl.ds(e * MOE_SLOTS + (ngc % gpr) * gr, gr), :]

            @pl.when((rem == 0) & (w == 0))
            def _(e=e):
                carry_ref[pl.ds(e * gr, gr), :] = jnp.zeros((gr, carry_ref.shape[1]), carry_ref.dtype)

        nout_ref[b] = issued
        return c

    lax.fori_loop(0, _num_rounds(used), round_body, 0)

    @pl.when(s == pl.num_programs(0) - 1)
    def _():
        drain(0)
        drain(1)


def _dispatch(offs, xn, pos3, *, cap):
    n, d = xn.shape
    ntile, ne, tt = pos3.shape
    kern = functools.partial(_dispatch_kernel, cap=cap)
    return pl.pallas_call(
        kern,
        grid_spec=pltpu.PrefetchScalarGridSpec(
            num_scalar_prefetch=1,
            grid=(ntile,),
            in_specs=[
                pl.BlockSpec((tt, d), lambda i, off: (i, 0)),
                pl.BlockSpec((1, ne, tt), lambda i, off: (i, 0, 0)),
            ],
            out_specs=pl.BlockSpec(memory_space=pl.ANY),
            scratch_shapes=[
                pltpu.VMEM((2, ne * MOE_SLOTS, d), BF16),
                pltpu.VMEM((ne * BF16_ROWS, d), BF16),
                pltpu.SemaphoreType.DMA((2,)),
                pltpu.SMEM((2,), I32),
            ],
        ),
        out_shape=jax.ShapeDtypeStruct((ne * cap, d), BF16),
        compiler_params=_cparams(("arbitrary",)),
        name="ec_dispatch",
    )(offs, xn, pos3)


def _ffn_kernel(x_ref, wg_ref, wu_ref, wd_ref, o_ref):
    x = x_ref[...]
    a = jnp.dot(x, wg_ref[0], preferred_element_type=F32)
    u = jnp.dot(x, wu_ref[0], preferred_element_type=F32)
    hid = (a * jax.nn.sigmoid(a) * u).astype(BF16)
    o_ref[...] = jnp.dot(hid, wd_ref[0], preferred_element_type=F32).astype(o_ref.dtype)


def _expert_ffn(xe, wg, wu, wd, *, cap, tr=1024):
    ne, d, f = wg.shape
    tr = min(tr, cap)
    nt = cap // tr
    return pl.pallas_call(
        _ffn_kernel,
        grid=(ne, nt),
        in_specs=[
            pl.BlockSpec((tr, d), lambda e, t: (e * nt + t, 0)),
            pl.BlockSpec((1, d, f), lambda e, t: (e, 0, 0)),
            pl.BlockSpec((1, d, f), lambda e, t: (e, 0, 0)),
            pl.BlockSpec((1, f, d), lambda e, t: (e, 0, 0)),
        ],
        out_specs=pl.BlockSpec((tr, d), lambda e, t: (e * nt + t, 0)),
        out_shape=jax.ShapeDtypeStruct((ne * cap, d), BF16),
        compiler_params=_cparams(("parallel", "parallel")),
        name="expert_ffn",
    )(xe, wg, wu, wd)


def _combine_kernel(off_ref, h_ref, aff_ref, pos_ref, p_ref, ye_ref, png_ref, wpg_ref, wpu_ref, fng_ref,
                    o_ref, wins_ref, acc_ref, sem, *, cap, final_norm):
    s = pl.program_id(0)
    ntile = pl.num_programs(0)
    nsub, ne, tt = pos_ref.shape
    spill_buf = 2 * nsub

    def window_copies(starts_t, w, buf):
        copies, wstarts = [], []
        for e in range(ne):
            ws = pl.multiple_of(jnp.minimum(starts_t[e] + w * MOE_SLOTS, cap - MOE_SLOTS), BF16_ROWS)
            copies.append(pltpu.make_async_copy(
                ye_ref.at[pl.ds(pl.multiple_of(e * cap + ws, BF16_ROWS), MOE_SLOTS)],
                wins_ref.at[buf, pl.ds(e * MOE_SLOTS, MOE_SLOTS)], sem.at[buf]))
            wstarts.append(ws)
        return copies, wstarts

    cur = s % 2
    tiles = [_tile_windows(off_ref, s * nsub + k, ne) for k in range(nsub)]

    @pl.when(s == 0)
    def _():
        for k in range(nsub):
            for cp in window_copies(tiles[k][0], 0, cur * nsub + k)[0]:
                cp.start()

    @pl.when(s + 1 < ntile)
    def _():
        for k in range(nsub):
            nxt_starts, _ = _tile_windows(off_ref, (s + 1) * nsub + k, ne)
            for cp in window_copies(nxt_starts, 0, (1 - cur) * nsub + k)[0]:
                cp.start()

    slot = lax.broadcasted_iota(I32, (MOE_SLOTS, tt), 0)
    tn = (((0,), (0,)), ((), ()))

    def gates(k, w, wstarts):
        starts = tiles[k][0]
        pos = pos_ref[k]
        aff = aff_ref[k]
        pieces = []
        for e in range(ne):
            base = starts[e] + w * MOE_SLOTS
            pe = pos[e:e + 1, :]
            in_round = (pe >= base) & (pe < base + MOE_SLOTS)
            hit = (jnp.broadcast_to(pe - wstarts[e], (MOE_SLOTS, tt)) == slot) & jnp.broadcast_to(in_round, (MOE_SLOTS, tt))
            pieces.append(jnp.where(hit, jnp.broadcast_to(aff[e:e + 1, :], (MOE_SLOTS, tt)), 0.0))
        return jnp.concatenate(pieces, axis=0).astype(BF16)

    for k in range(nsub):
        starts, used = tiles[k]
        rows = pl.ds(k * tt, tt)
        buf0 = cur * nsub + k
        copies0, wstarts0 = window_copies(starts, 0, buf0)
        a0 = gates(k, 0, wstarts0)
        for cp in copies0:
            cp.wait()
        acc_ref[rows, :] = lax.dot_general(a0, wins_ref[buf0], tn, preferred_element_type=F32)

        def round_body(w, c, k=k, starts=starts, rows=rows):
            copies, wstarts = window_copies(starts, w, spill_buf)
            for cp in copies:
                cp.start()
            a_w = gates(k, w, wstarts)
            for cp in copies:
                cp.wait()
            acc_ref[rows, :] += lax.dot_general(a_w, wins_ref[spill_buf], tn, preferred_element_type=F32)
            return c

        lax.fori_loop(1, _num_rounds(used), round_body, 0)
    h2 = h_ref[...] + acc_ref[...]
    gate = jax.nn.sigmoid(jnp.dot(_rms(h2, png_ref[...]).astype(BF16), wpg_ref[...], preferred_element_type=F32))
    up = jnp.dot(p_ref[0].astype(BF16), wpu_ref[...], preferred_element_type=F32)
    h3 = h2 + up * gate
    if final_norm:
        h3 = _rms(h3, fng_ref[...])
    o_ref[...] = h3


def _combine(offs, h1, aff3, pos3, p_all, ye, ple_norm, wpg, wpu, final_g, *, layer, cap, final_norm):
    n, d = h1.shape
    ntile, ne, tt = pos3.shape
    pd = p_all.shape[2]
    nsub = min(COMBINE_TILES, ntile)
    assert ntile % nsub == 0
    rows = nsub * tt
    nbuf = 2 * nsub + 1
    kern = functools.partial(_combine_kernel, cap=cap, final_norm=final_norm)
    return pl.pallas_call(
        kern,
        grid_spec=pltpu.PrefetchScalarGridSpec(
            num_scalar_prefetch=1,
            grid=(ntile // nsub,),
            in_specs=[
                pl.BlockSpec((rows, d), lambda i, off: (i, 0)),
                pl.BlockSpec((nsub, ne, tt), lambda i, off: (i, 0, 0)),
                pl.BlockSpec((nsub, ne, tt), lambda i, off: (i, 0, 0)),
                pl.BlockSpec((1, rows, pd), lambda i, off: (layer, i, 0)),
                pl.BlockSpec(memory_space=pl.ANY),
                pl.BlockSpec((1, d), lambda i, off: (0, 0)),
                pl.BlockSpec((d, d), lambda i, off: (0, 0)),
                pl.BlockSpec((pd, d), lambda i, off: (0, 0)),
                pl.BlockSpec((1, d), lambda i, off: (0, 0)),
            ],
            out_specs=pl.BlockSpec((rows, d), lambda i, off: (i, 0)),
            scratch_shapes=[pltpu.VMEM((nbuf, ne * MOE_SLOTS, d), BF16), pltpu.VMEM((rows, d), F32),
                            pltpu.SemaphoreType.DMA((nbuf,))],
        ),
        out_shape=jax.ShapeDtypeStruct((n, d), F32),
        compiler_params=_cparams(("arbitrary",)),
        name="combine_ple",
    )(offs, h1, aff3, pos3, p_all, ye, ple_norm.reshape(1, d), wpg, wpu, final_g.reshape(1, d))


def _moe_block(h1, xn, aff3, p_all, w, i, *, final_norm):
    n, d = h1.shape
    cap = EC_CAPACITY * n // N_EXPERTS
    pos3, offs = _select(aff3, cap=cap)
    xe = _dispatch(offs, xn, pos3, cap=cap)
    ye = _expert_ffn(xe, w["exp_w_gate"][i], w["exp_w_up"][i], w["exp_w_down"][i], cap=cap)
    return _combine(offs, h1, aff3, pos3, p_all, ye, w["ple_norm"][i], w["ple_w_gate"][i], w["ple_w_up"][i],
                    w["final_norm"], layer=i, cap=cap, final_norm=final_norm)


def _rope_tables(seq, dim):
    inv = ROPE_THETA ** (-jnp.arange(0, dim, 2, dtype=F32) / dim)
    ang = jnp.arange(seq, dtype=F32)[:, None] * inv[None, :]
    return jnp.cos(ang), jnp.sin(ang)


def _trunk(x, p, w):
    batch, seq, d = x.shape
    n = batch * seq
    h = x.reshape(n, d)
    dk = d // RET_HEADS
    cos, sin = _rope_tables(seq, dk)

    proj = _in_proj(h, w["mix_norm"][0], w["ret_w_in"][0], cos, sin, seq=seq, rope_units=2,
                    unit_scales=(1.0, float(dk) ** -0.5, 1.0, 1.0, 1.0, 1.0))
    a = _retention(proj, w["ret_decay_logit"][0], w["ret_gn_w"][0], batch=batch, seq=seq, d_model=d)
    h1, xn, aff3 = _out_router(a, w["ret_w_out"][0], h, w["ffn_norm"][0], w["router_w"][0])
    p_all = p.reshape(p.shape[0], n, p.shape[-1])
    h = _moe_block(h1, xn, aff3, p_all, w, 0, final_norm=False)

    qkv = _in_proj(h, w["mix_norm"][1], w["na_w_in"][0], cos, sin, seq=seq, rope_units=0,
                   unit_scales=(float(NA_HEAD_DIM) ** -0.5, 1.0, 1.0))
    bias = _na_bias_table(w["na_rpb"][0], seq // GRID_W)
    a = _na_attention(qkv, bias, batch=batch, seq=seq, d_model=d)
    h1, xn, aff3 = _out_router(a, w["na_w_out"][0], h, w["ffn_norm"][1], w["router_w"][1])
    y = _moe_block(h1, xn, aff3, p_all, w, 1, final_norm=True)
    return y.reshape(batch, seq, d)


def kernel(x_prompt, x_sample, p_prompt, p_sample, ret_w_in, ret_decay_logit, ret_gn_w, ret_w_out, na_w_in, na_rpb, na_w_out, mix_norm, ffn_norm, ple_norm, router_w, exp_w_gate, exp_w_up, exp_w_down, ple_w_up, ple_w_gate, final_norm):
    w = dict(
        ret_w_in=ret_w_in.astype(BF16), ret_decay_logit=ret_decay_logit, ret_gn_w=ret_gn_w,
        ret_w_out=ret_w_out.astype(BF16), na_w_in=na_w_in.astype(BF16), na_rpb=na_rpb,
        na_w_out=na_w_out.astype(BF16), mix_norm=mix_norm, ffn_norm=ffn_norm, ple_norm=ple_norm,
        router_w=router_w, exp_w_gate=exp_w_gate.astype(BF16), exp_w_up=exp_w_up.astype(BF16),
        exp_w_down=exp_w_down.astype(BF16), ple_w_up=ple_w_up.astype(BF16), ple_w_gate=ple_w_gate.astype(BF16),
        final_norm=final_norm,
    )
    return _trunk(x_prompt, p_prompt, w), _trunk(x_sample, p_sample, w)
```

```python
import functools

import numpy as np
import jax
import jax.numpy as jnp
from jax import lax
from jax.experimental import pallas as pl
from jax.experimental.pallas import tpu as pltpu

F32 = jnp.float32
BF16 = jnp.bfloat16
I32 = jnp.int32

NORM_EPS = 1e-6
ROPE_THETA = 10000.0
GRID_W = 64
RET_HEADS = 4
RET_CHUNK = 256
NA_HEADS = 16
NA_HEAD_DIM = 64
NA_MAX_ROWS = 8
NA_WIN_COLS = 16
NA_ROW_UNROLL = 8
N_EXPERTS = 16
EC_CAPACITY = 2
LANES = 128
BF16_ROWS = 16
MOE_TILE = 256
MOE_SLOTS = 64
ROUTE_TILES_PER_STEP = 2
NEG_BIG = -1e30

_VMEM_LIMIT = 56 * 1024 * 1024


def _cparams(sem):
    return pltpu.CompilerParams(dimension_semantics=sem, vmem_limit_bytes=_VMEM_LIMIT)


def _rms(x, g):
    return x * lax.rsqrt(jnp.mean(x * x, axis=-1, keepdims=True) + NORM_EPS) * g


def _in_proj_kernel(x_ref, g_ref, w_ref, cos_ref, sin_ref, o_ref, xn_ref, *, nj, unit_w, rope_units, unit_scales,
                    rope_dim):
    j = pl.program_id(1)

    @pl.when(j == 0)
    def _():
        xn_ref[...] = _rms(x_ref[...], g_ref[...]).astype(BF16)

    upb = w_ref.shape[1] // unit_w
    half = rope_dim // 2

    def emit(y, c, u):
        base = c * unit_w
        if unit_scales[u] != 1.0:
            y = y * unit_scales[u]
        if u < rope_units:
            cos = cos_ref[...]
            sin = sin_ref[...]
            for hh in range(unit_w // rope_dim):
                lo = hh * rope_dim
                x1 = y[:, lo:lo + half]
                x2 = y[:, lo + half:lo + rope_dim]
                o_ref[:, base + lo:base + lo + half] = (x1 * cos - x2 * sin).astype(o_ref.dtype)
                o_ref[:, base + lo + half:base + lo + rope_dim] = (x1 * sin + x2 * cos).astype(o_ref.dtype)
        else:
            o_ref[:, base:base + unit_w] = y.astype(o_ref.dtype)

    for c in range(upb):
        y = jnp.dot(xn_ref[...], w_ref[:, c * unit_w:(c + 1) * unit_w], preferred_element_type=F32)
        kinds = [(jj * upb + c < rope_units, unit_scales[jj * upb + c]) for jj in range(nj)]
        if all(k == kinds[0] for k in kinds):
            emit(y, c, c)
        else:
            for jj in range(nj):
                pl.when(j == jj)(functools.partial(emit, y, c, jj * upb + c))


def _in_proj(x, g, w_bf16, cos, sin, *, seq, rope_units, unit_scales, tm=1024, units_per_step=3):
    n, d = x.shape
    tm = min(tm, seq)
    ncol = w_bf16.shape[1]
    tn = units_per_step * d
    nj = ncol // tn
    assert len(unit_scales) == ncol // d
    rope_dim = 2 * cos.shape[1]
    nseq = seq // tm
    kern = functools.partial(_in_proj_kernel, nj=nj, unit_w=d, rope_units=rope_units, unit_scales=unit_scales,
                             rope_dim=rope_dim)
    return pl.pallas_call(
        kern,
        grid=(n // tm, ncol // tn),
        in_specs=[
            pl.BlockSpec((tm, d), lambda i, j: (i, 0)),
            pl.BlockSpec((1, d), lambda i, j: (0, 0)),
            pl.BlockSpec((d, tn), lambda i, j: (0, j)),
            pl.BlockSpec((tm, cos.shape[1]), lambda i, j: (i % nseq, 0)),
            pl.BlockSpec((tm, cos.shape[1]), lambda i, j: (i % nseq, 0)),
        ],
        out_specs=pl.BlockSpec((tm, tn), lambda i, j: (i, j)),
        out_shape=jax.ShapeDtypeStruct((n, ncol), BF16),
        scratch_shapes=[pltpu.VMEM((tm, d), BF16)],
        compiler_params=_cparams(("parallel", "arbitrary")),
        name="in_proj",
    )(x, g.reshape(1, d), w_bf16, cos, sin)


def _retention_kernel(dl_ref, q_ref, k_ref, v_ref, g_ref, gnw_ref, o_ref, acc_ref, stf_ref, stb_ref, *, seq, chunk):
    h = pl.program_id(1)
    nc = seq // chunk
    dl = dl_ref[...]
    lg = jnp.minimum(dl, 0.0) - jnp.log1p(jnp.exp(-jnp.abs(dl)))
    col = lax.broadcasted_iota(I32, dl.shape, 1)
    lgh = jnp.sum(jnp.where(col == h, lg, 0.0), axis=1, keepdims=True)
    lgf = lgh[0:1, :]
    lgb = lgh[1:2, :]

    ri = lax.broadcasted_iota(I32, (chunk, chunk), 0)
    ci = lax.broadcasted_iota(I32, (chunk, chunk), 1)
    diff = (ri - ci).astype(F32)
    dmat = jnp.exp(jnp.where(diff >= 0, diff * lgf, -diff * lgb))
    pos = lax.broadcasted_iota(I32, (chunk, 1), 0).astype(F32)
    qdec_f = jnp.exp((pos + 1.0) * lgf)
    kdec_f = jnp.exp((chunk - 1.0 - pos) * lgf)
    qdec_b = jnp.exp((chunk - pos) * lgb)
    kdec_b = jnp.exp(pos * lgb)
    cdec_f = jnp.exp(chunk * lgf)
    cdec_b = jnp.exp(chunk * lgb)
    nt = (((1,), (1,)), ((), ()))
    tn = (((0,), (0,)), ((), ()))

    stf_ref[...] = jnp.zeros_like(stf_ref)
    stb_ref[...] = jnp.zeros_like(stb_ref)

    def chunk_slice(c):
        return pl.ds(pl.multiple_of(c * chunk, chunk), chunk)

    def scaled(x, dec):
        return (x.astype(F32) * dec).astype(BF16)

    def step(t):
        slf = chunk_slice(t)
        slb = chunk_slice(nc - 1 - t)
        qf, kf, vf = q_ref[slf, :], k_ref[slf, :], v_ref[slf, :]
        qb, kb, vb = q_ref[slb, :], k_ref[slb, :], v_ref[slb, :]
        s = lax.dot_general(qf, kf, nt, preferred_element_type=F32) * dmat
        cross_b = jnp.dot(scaled(qb, qdec_b), stb_ref[...].astype(BF16), preferred_element_type=F32)
        inner = jnp.dot(s.astype(BF16), vf, preferred_element_type=F32)
        cross_f = jnp.dot(scaled(qf, qdec_f), stf_ref[...].astype(BF16), preferred_element_type=F32)
        stb_ref[...] = stb_ref[...] * cdec_b + lax.dot_general(scaled(kb, kdec_b), vb, tn, preferred_element_type=F32)
        stf_ref[...] = stf_ref[...] * cdec_f + lax.dot_general(scaled(kf, kdec_f), vf, tn, preferred_element_type=F32)
        return slf, inner + cross_f, slb, cross_b

    def finish(sl, o):
        mu = jnp.mean(o, axis=-1, keepdims=True)
        oc = o - mu
        var = jnp.mean(oc * oc, axis=-1, keepdims=True)
        on = oc * lax.rsqrt(var + NORM_EPS) * gnw_ref[...]
        gate = g_ref[sl, :].astype(F32)
        o_ref[sl, :] = (gate * jax.nn.sigmoid(gate) * on).astype(o_ref.dtype)

    def first_half(t, carry):
        slf, of, slb, ob = step(t)
        acc_ref[slf, :] = of
        acc_ref[slb, :] = ob
        return carry

    def second_half(t, carry):
        slf, of, slb, ob = step(t)
        finish(slf, acc_ref[slf, :] + of)
        finish(slb, acc_ref[slb, :] + ob)
        return carry

    lax.fori_loop(0, nc // 2, first_half, 0)
    lax.fori_loop(nc // 2, nc, second_half, 0)


def _retention(proj, decay_logit, gn_w, *, batch, seq, d_model):
    heads = RET_HEADS
    dk = d_model // heads
    dv = 2 * d_model // heads
    n = batch * seq
    assert seq % (2 * RET_CHUNK) == 0, "the two scans are paired chunk by chunk"
    kern = functools.partial(_retention_kernel, seq=seq, chunk=RET_CHUNK)
    return pl.pallas_call(
        kern,
        grid=(batch, heads),
        in_specs=[
            pl.BlockSpec((2, heads), lambda b, h: (0, 0)),
            pl.BlockSpec((seq, dk), lambda b, h: (b, h)),
            pl.BlockSpec((seq, dk), lambda b, h: (b, heads + h)),
            pl.BlockSpec((seq, dv), lambda b, h: (b, heads + h)),
            pl.BlockSpec((seq, dv), lambda b, h: (b, 2 * heads + h)),
            pl.BlockSpec((1, dv), lambda b, h: (0, h)),
        ],
        out_specs=pl.BlockSpec((seq, dv), lambda b, h: (b, h)),
        out_shape=jax.ShapeDtypeStruct((n, 2 * d_model), BF16),
        scratch_shapes=[pltpu.VMEM((seq, dv), F32), pltpu.VMEM((dk, dv), F32), pltpu.VMEM((dk, dv), F32)],
        compiler_params=_cparams(("parallel", "parallel")),
        name="retention",
    )(decay_logit, proj, proj, proj, proj, gn_w.reshape(1, -1))


def _na_kernel(bias_ref, q_ref, k_ref, v_ref, o_ref, *, rows):
    gw = GRID_W
    kr = min(NA_MAX_ROWS, rows)
    dh = NA_HEAD_DIM
    lane_q = lax.broadcasted_iota(I32, (gw, 2 * dh), 1)
    nt = (((1,), (1,)), ((), ()))

    group = min(NA_ROW_UNROLL, rows)

    def body(it, carry):
        rr = [it * group + i for i in range(group)]
        rs = [jnp.clip(r - kr // 2, 0, rows - kr) for r in rr]
        s = []
        for r, r0 in zip(rr, rs):
            q2 = q_ref[pl.ds(pl.multiple_of(r * gw, gw), gw), :]
            zero = jnp.zeros_like(q2)
            qq = jnp.concatenate([jnp.where(lane_q < dh, q2, zero), jnp.where(lane_q >= dh, q2, zero)], axis=0)
            k2 = k_ref[pl.ds(pl.multiple_of(r0 * gw, gw), kr * gw), :]
            s.append(lax.dot_general(qq, k2, nt, preferred_element_type=F32) + bias_ref[0, r - r0])
        m = [jnp.max(si, axis=-1, keepdims=True) for si in s]
        p = [jnp.exp(si - mi) for si, mi in zip(s, m)]
        inv = [1.0 / jnp.sum(pi, axis=-1, keepdims=True) for pi in p]
        o = [jnp.dot(pi.astype(BF16), v_ref[pl.ds(pl.multiple_of(r0 * gw, gw), kr * gw), :],
                     preferred_element_type=F32) * ii for pi, ii, r0 in zip(p, inv, rs)]
        for r, oi in zip(rr, o):
            out = jnp.where(lane_q < dh, oi[0:gw, :], oi[gw:2 * gw, :])
            o_ref[pl.ds(pl.multiple_of(r * gw, gw), gw), :] = out.astype(o_ref.dtype)
        return carry

    lax.fori_loop(0, rows // group, body, 0)


def _na_bias_table(rpb, rows):
    kr = min(NA_MAX_ROWS, rows)
    c = np.arange(GRID_W)
    kc = np.arange(GRID_W)
    win_start = np.clip(c - NA_WIN_COLS // 2, 0, GRID_W - NA_WIN_COLS)
    valid = (kc[None, :] >= win_start[:, None]) & (kc[None, :] < win_start[:, None] + NA_WIN_COLS)
    dc_idx = np.clip(kc[None, :] - c[:, None] + NA_WIN_COLS - 1, 0, 2 * NA_WIN_COLS - 2)
    delta = np.arange(kr)
    a = np.arange(kr)
    dr_idx = a[None, :] - delta[:, None] + NA_MAX_ROWS - 1
    row_sel = (dr_idx[:, :, None] == np.arange(rpb.shape[1])).astype(np.float32)
    col_sel = (dc_idx[None, :, :] == np.arange(rpb.shape[2])[:, None, None]).astype(np.float32)
    t = jnp.einsum("dar,hrs,sck->hdcak", row_sel, rpb.astype(F32), col_sel, precision=lax.Precision.HIGHEST)
    t = jnp.where(jnp.asarray(valid)[None, None, :, None, :], t, NEG_BIG)
    h = rpb.shape[0]
    t = t.reshape(h // 2, 2, kr, GRID_W, kr * GRID_W).transpose(0, 2, 1, 3, 4)
    return t.reshape(h // 2, kr, 2 * GRID_W, kr * GRID_W)


def _na_attention(qkv, bias, *, batch, seq, d_model):
    n = batch * seq
    rows = seq // GRID_W
    kr = min(NA_MAX_ROWS, rows)
    pairs = NA_HEADS // 2
    pw = 2 * NA_HEAD_DIM
    assert rows % min(NA_ROW_UNROLL, rows) == 0
    kern = functools.partial(_na_kernel, rows=rows)
    return pl.pallas_call(
        kern,
        grid=(batch, pairs),
        in_specs=[
            pl.BlockSpec((1, kr, 2 * GRID_W, kr * GRID_W), lambda b, hp: (hp, 0, 0, 0)),
            pl.BlockSpec((seq, pw), lambda b, hp: (b, hp)),
            pl.BlockSpec((seq, pw), lambda b, hp: (b, pairs + hp)),
            pl.BlockSpec((seq, pw), lambda b, hp: (b, 2 * pairs + hp)),
        ],
        out_specs=pl.BlockSpec((seq, pw), lambda b, hp: (b, hp)),
        out_shape=jax.ShapeDtypeStruct((n, d_model), BF16),
        compiler_params=_cparams(("parallel", "parallel")),
        name="na_attention",
    )(bias, qkv, qkv, qkv)


def _out_router_kernel(a_ref, w_ref, h_ref, g_ref, rw_ref, h1_ref, xn_ref, aff_ref):
    y = jnp.dot(a_ref[...], w_ref[...], preferred_element_type=F32) + h_ref[...]
    h1_ref[...] = y
    xn = _rms(y, g_ref[...])
    xn_ref[...] = xn.astype(xn_ref.dtype)
    nt = (((1,), (1,)), ((), ()))
    ne = rw_ref.shape[0]
    xh = xn.astype(BF16)
    xl = (xn - xh.astype(F32)).astype(BF16)
    rw = rw_ref[...]
    rh = rw.astype(BF16)
    rl = (rw - rh.astype(F32)).astype(BF16)
    t1 = lax.dot_general(jnp.concatenate([rh, rl], axis=0), xh, nt, preferred_element_type=F32)
    t2 = lax.dot_general(rh, xl, nt, preferred_element_type=F32)
    logits = t1[0:ne, :] + (t1[ne:2 * ne, :] + t2)
    m = jnp.max(logits, axis=0, keepdims=True)
    e = jnp.exp(logits - m)
    aff = e / jnp.sum(e, axis=0, keepdims=True)
    for t in range(aff_ref.shape[0]):
        aff_ref[t] = aff[:, t * MOE_TILE:(t + 1) * MOE_TILE]


def _out_router(a, w_bf16, h, g, router_w, *, tm=1024):
    n, kdim = a.shape
    tm = min(tm, n)
    d = h.shape[1]
    ne = router_w.shape[1]
    tpb = tm // MOE_TILE
    return pl.pallas_call(
        _out_router_kernel,
        grid=(n // tm,),
        in_specs=[
            pl.BlockSpec((tm, kdim), lambda i: (i, 0)),
            pl.BlockSpec((kdim, d), lambda i: (0, 0)),
            pl.BlockSpec((tm, d), lambda i: (i, 0)),
            pl.BlockSpec((1, d), lambda i: (0, 0)),
            pl.BlockSpec((ne, d), lambda i: (0, 0)),
        ],
        out_specs=[
            pl.BlockSpec((tm, d), lambda i: (i, 0)),
            pl.BlockSpec((tm, d), lambda i: (i, 0)),
            pl.BlockSpec((tpb, ne, MOE_TILE), lambda i: (i, 0, 0)),
        ],
        out_shape=[
            jax.ShapeDtypeStruct((n, d), F32),
            jax.ShapeDtypeStruct((n, d), BF16),
            jax.ShapeDtypeStruct((n // MOE_TILE, ne, MOE_TILE), F32),
        ],
        compiler_params=_cparams(("parallel",)),
        name="out_router",
    )(a, w_bf16, h, g.reshape(1, d), router_w.T)


def _select_kernel(aff_ref, pos_ref, off_ref, *, cap):
    ntile, ne, tt = aff_ref.shape
    nsub = tt // LANES

    def count(pred_fn):
        def body(c, acc):
            x = pltpu.bitcast(aff_ref[c], I32)
            return acc + pred_fn(x).astype(I32)
        acc = lax.fori_loop(0, ntile, body, jnp.zeros((ne, tt), I32))
        return jnp.sum(acc, axis=1, keepdims=True)

    def bit_step(i, t):
        cand = t | jnp.left_shift(jnp.int32(1), 30 - i)
        return jnp.where(count(lambda x: x >= cand) >= cap, cand, t)

    thr = lax.fori_loop(0, 31, bit_step, jnp.zeros((ne, 1), I32))
    need_eq = (cap - count(lambda x: x > thr)).astype(F32)

    li = lax.broadcasted_iota(I32, (LANES, LANES), 0)
    lj = lax.broadcasted_iota(I32, (LANES, LANES), 1)
    upper = (li < lj).astype(BF16)

    def tile_body(c, carry):
        eq_carry, pos_carry = carry
        off_ref[c] = jnp.broadcast_to(pos_carry, (ne, LANES)).astype(I32)
        xt = pltpu.bitcast(aff_ref[c], I32)
        parts = []
        for j in range(nsub):
            x = xt[:, j * LANES:(j + 1) * LANES]
            gt = x > thr
            eq = x == thr
            eqf = eq.astype(F32)
            eq_rank = eq_carry + jnp.dot(eqf.astype(BF16), upper, preferred_element_type=F32)
            sel = gt | (eq & (eq_rank < need_eq))
            eq_carry = eq_carry + jnp.sum(eqf, axis=1, keepdims=True)
            self_ = sel.astype(F32)
            cexc = jnp.dot(self_.astype(BF16), upper, preferred_element_type=F32)
            parts.append(jnp.where(sel, (pos_carry + cexc).astype(I32), -1))
            pos_carry = pos_carry + jnp.sum(self_, axis=1, keepdims=True)
        pos_ref[c] = jnp.concatenate(parts, axis=1)
        return eq_carry, pos_carry

    lax.fori_loop(0, ntile, tile_body, (jnp.zeros((ne, 1), F32), jnp.zeros((ne, 1), F32)))


def _select(aff3, *, cap):
    ntile, ne, tt = aff3.shape
    kern = functools.partial(_select_kernel, cap=cap)
    pos3, off3 = pl.pallas_call(
        kern,
        out_shape=[
            jax.ShapeDtypeStruct((ntile, ne, tt), I32),
            jax.ShapeDtypeStruct((ntile, ne, LANES), I32),
        ],
        compiler_params=pltpu.CompilerParams(vmem_limit_bytes=_VMEM_LIMIT),
        name="ec_select",
    )(aff3)
    offs = jnp.concatenate([off3[:, :, 0], jnp.full((1, ne), cap, I32)], axis=0).reshape(-1)
    return pos3, offs


def _tile_windows(off_ref, s, ne):
    starts, used = [], []
    for e in range(ne):
        off = off_ref[s * ne + e]
        nxt = off_ref[(s + 1) * ne + e]
        st = (off // BF16_ROWS) * BF16_ROWS
        starts.append(st)
        used.append(nxt - st)
    return starts, used


def _num_rounds(used):
    m = used[0]
    for u in used[1:]:
        m = jnp.maximum(m, u)
    return (m + MOE_SLOTS - 1) // MOE_SLOTS


def _dispatch_kernel(off_ref, x_ref, pos_ref, xe_ref, wins_ref, carry_ref, sem, nout_ref, *, cap):
    s = pl.program_id(0)
    nsub, ne, tt = pos_ref.shape
    gr = BF16_ROWS
    gpr = MOE_SLOTS // gr

    @pl.when(s == 0)
    def _():
        carry_ref[...] = jnp.zeros_like(carry_ref)
        nout_ref[0] = 0
        nout_ref[1] = 0

    slot = lax.broadcasted_iota(I32, (MOE_SLOTS, tt), 0)

    def drain(b):
        def wait_one(i, c):
            pltpu.make_async_copy(wins_ref.at[b, pl.ds(0, gr)], xe_ref.at[pl.ds(0, gr)], sem.at[b]).wait()
            return c
        lax.fori_loop(0, nout_ref[b], wait_one, 0)
        nout_ref[b] = 0

    for k in range(nsub):
        tile = s * nsub + k
        starts, used = _tile_windows(off_ref, tile, ne)
        pos = pos_ref[k]
        x = x_ref[pl.ds(k * tt, tt), :]

        def round_body(w, c, tile=tile, starts=starts, used=used, pos=pos, x=x):
            b = (tile + w) % 2
            drain(b)
            pieces = []
            for e in range(ne):
                key = pos[e:e + 1, :] - (starts[e] + w * MOE_SLOTS)
                pieces.append((jnp.broadcast_to(key, (MOE_SLOTS, tt)) == slot).astype(F32))
            onehot = jnp.concatenate(pieces, axis=0).astype(BF16)
            wins_ref[b] = jnp.dot(onehot, x, preferred_element_type=F32).astype(BF16)
            first = w == 0
            issued = 0
            for e in range(ne):
                head = pl.ds(e * MOE_SLOTS, gr)
                crows = pl.ds(e * gr, gr)
                carry = carry_ref[crows, :]
                merged = (wins_ref[b, head, :].astype(F32) + carry.astype(F32)).astype(BF16)
                wins_ref[b, head, :] = jnp.where(first, merged, wins_ref[b, head, :])

                ngc = used[e] // gr
                rem = used[e] % gr
                ng_w = jnp.clip(ngc - w * gpr, 0, gpr)

                def issue(j, c2, e=e):
                    src = wins_ref.at[b, pl.ds(e * MOE_SLOTS + j * gr, gr)]
                    dst = xe_ref.at[pl.ds(pl.multiple_of(e * cap + starts[e] + w * MOE_SLOTS + j * gr, gr), gr)]
                    pltpu.make_async_copy(src, dst, sem.at[b]).start()
                    return c2

                lax.fori_loop(0, ng_w, issue, 0)
                issued = issued + ng_w

                part = wins_ref[b, pl.ds(e * MOE_SLOTS + jnp.clip(ngc - w * gpr, 0, gpr - 1) * gr, gr), :]
                keep_part = (rem > 0) & (ngc // gpr == w)
                clear = (rem == 0) & first
                carry_ref[crows, :] = jnp.where(keep_part, part, jnp.where(clear, jnp.zeros_like(carry), carry))

            nout_ref[b] = issued
            return c

        lax.fori_loop(0, _num_rounds(used), round_body, 0)

    @pl.when(s == pl.num_programs(0) - 1)
    def _():
        drain(0)
        drain(1)


def _dispatch(offs, xn, pos3, *, cap):
    n, d = xn.shape
    ntile, ne, tt = pos3.shape
    nsub = min(ROUTE_TILES_PER_STEP, ntile)
    assert ntile % nsub == 0
    kern = functools.partial(_dispatch_kernel, cap=cap)
    return pl.pallas_call(
        kern,
        grid_spec=pltpu.PrefetchScalarGridSpec(
            num_scalar_prefetch=1,
            grid=(ntile // nsub,),
            in_specs=[
                pl.BlockSpec((nsub * tt, d), lambda i, off: (i, 0)),
                pl.BlockSpec((nsub, ne, tt), lambda i, off: (i, 0, 0)),
            ],
            out_specs=pl.BlockSpec(memory_space=pl.ANY),
            scratch_shapes=[
                pltpu.VMEM((2, ne * MOE_SLOTS, d), BF16),
                pltpu.VMEM((ne * BF16_ROWS, d), BF16),
                pltpu.SemaphoreType.DMA((2,)),
                pltpu.SMEM((2,), I32),
            ],
        ),
        out_shape=jax.ShapeDtypeStruct((ne * cap, d), BF16),
        compiler_params=_cparams(("arbitrary",)),
        name="ec_dispatch",
    )(offs, xn, pos3)


def _ffn_kernel(x_ref, wg_ref, wu_ref, wd_ref, o_ref):
    x = x_ref[...]
    a = jnp.dot(x, wg_ref[0], preferred_element_type=F32)
    u = jnp.dot(x, wu_ref[0], preferred_element_type=F32)
    hid = (a * jax.nn.sigmoid(a) * u).astype(BF16)
    o_ref[...] = jnp.dot(hid, wd_ref[0], preferred_element_type=F32).astype(o_ref.dtype)


def _expert_ffn(xe, wg, wu, wd, *, cap, tr=1024):
    ne, d, f = wg.shape
    tr = min(tr, cap)
    nt = cap // tr
    return pl.pallas_call(
        _ffn_kernel,
        grid=(ne, nt),
        in_specs=[
            pl.BlockSpec((tr, d), lambda e, t: (e * nt + t, 0)),
            pl.BlockSpec((1, d, f), lambda e, t: (e, 0, 0)),
            pl.BlockSpec((1, d, f), lambda e, t: (e, 0, 0)),
            pl.BlockSpec((1, f, d), lambda e, t: (e, 0, 0)),
        ],
        out_specs=pl.BlockSpec((tr, d), lambda e, t: (e * nt + t, 0)),
        out_shape=jax.ShapeDtypeStruct((ne * cap, d), BF16),
        compiler_params=_cparams(("parallel", "parallel")),
        name="expert_ffn",
    )(xe, wg, wu, wd)


def _combine_kernel(off_ref, h_ref, aff_ref, pos_ref, p_ref, ye_ref, png_ref, wpg_ref, wpu_ref, fng_ref,
                    o_ref, wins_ref, acc_ref, sem, *, cap, final_norm):
    s = pl.program_id(0)
    ntile = pl.num_programs(0)
    nsub, ne, tt = pos_ref.shape
    spill_buf = 2 * nsub

    def window_copies(starts_t, w, buf):
        copies, wstarts = [], []
        for e in range(ne):
            ws = pl.multiple_of(jnp.minimum(starts_t[e] + w * MOE_SLOTS, cap - MOE_SLOTS), BF16_ROWS)
            copies.append(pltpu.make_async_copy(
                ye_ref.at[pl.ds(pl.multiple_of(e * cap + ws, BF16_ROWS), MOE_SLOTS)],
                wins_ref.at[buf, pl.ds(e * MOE_SLOTS, MOE_SLOTS)], sem.at[buf]))
            wstarts.append(ws)
        return copies, wstarts

    cur = s % 2
    tiles = [_tile_windows(off_ref, s * nsub + k, ne) for k in range(nsub)]

    @pl.when(s == 0)
    def _():
        for k in range(nsub):
            for cp in window_copies(tiles[k][0], 0, cur * nsub + k)[0]:
                cp.start()

    @pl.when(s + 1 < ntile)
    def _():
        for k in range(nsub):
            nxt_starts, _ = _tile_windows(off_ref, (s + 1) * nsub + k, ne)
            for cp in window_copies(nxt_starts, 0, (1 - cur) * nsub + k)[0]:
                cp.start()

    slot = lax.broadcasted_iota(I32, (MOE_SLOTS, tt), 0)
    tn = (((0,), (0,)), ((), ()))
    up = jnp.dot(p_ref[0].astype(BF16), wpu_ref[...], preferred_element_type=F32)

    def gates(k, w, wstarts):
        starts = tiles[k][0]
        pos = pos_ref[k]
        aff = aff_ref[k]
        pieces = []
        for e in range(ne):
            base = starts[e] + w * MOE_SLOTS
            pe = pos[e:e + 1, :]
            in_round = (pe >= base) & (pe < base + MOE_SLOTS)
            hit = (jnp.broadcast_to(pe - wstarts[e], (MOE_SLOTS, tt)) == slot) & jnp.broadcast_to(in_round, (MOE_SLOTS, tt))
            pieces.append(jnp.where(hit, jnp.broadcast_to(aff[e:e + 1, :], (MOE_SLOTS, tt)), 0.0))
        return jnp.concatenate(pieces, axis=0).astype(BF16)

    for k in range(nsub):
        starts, used = tiles[k]
        rows = pl.ds(k * tt, tt)
        buf0 = cur * nsub + k
        copies0, wstarts0 = window_copies(starts, 0, buf0)
        a0 = gates(k, 0, wstarts0)
        for cp in copies0:
            cp.wait()
        acc_ref[rows, :] = lax.dot_general(a0, wins_ref[buf0], tn, preferred_element_type=F32)

        def round_body(w, c, k=k, starts=starts, rows=rows):
            copies, wstarts = window_copies(starts, w, spill_buf)
            for cp in copies:
                cp.start()
            a_w = gates(k, w, wstarts)
            for cp in copies:
                cp.wait()
            acc_ref[rows, :] += lax.dot_general(a_w, wins_ref[spill_buf], tn, preferred_element_type=F32)
            return c

        lax.fori_loop(1, _num_rounds(used), round_body, 0)
    h2 = h_ref[...] + acc_ref[...]
    gate = jax.nn.sigmoid(jnp.dot(_rms(h2, png_ref[...]).astype(BF16), wpg_ref[...], preferred_element_type=F32))
    h3 = h2 + up * gate
    if final_norm:
        h3 = _rms(h3, fng_ref[...])
    o_ref[...] = h3


def _combine(offs, h1, aff3, pos3, p_all, ye, ple_norm, wpg, wpu, final_g, *, layer, cap, final_norm):
    n, d = h1.shape
    ntile, ne, tt = pos3.shape
    pd = p_all.shape[2]
    nsub = min(ROUTE_TILES_PER_STEP, ntile)
    assert ntile % nsub == 0
    rows = nsub * tt
    nbuf = 2 * nsub + 1
    kern = functools.partial(_combine_kernel, cap=cap, final_norm=final_norm)
    return pl.pallas_call(
        kern,
        grid_spec=pltpu.PrefetchScalarGridSpec(
            num_scalar_prefetch=1,
            grid=(ntile // nsub,),
            in_specs=[
                pl.BlockSpec((rows, d), lambda i, off: (i, 0)),
                pl.BlockSpec((nsub, ne, tt), lambda i, off: (i, 0, 0)),
                pl.BlockSpec((nsub, ne, tt), lambda i, off: (i, 0, 0)),
                pl.BlockSpec((1, rows, pd), lambda i, off: (layer, i, 0)),
                pl.BlockSpec(memory_space=pl.ANY),
                pl.BlockSpec((1, d), lambda i, off: (0, 0)),
                pl.BlockSpec((d, d), lambda i, off: (0, 0)),
                pl.BlockSpec((pd, d), lambda i, off: (0, 0)),
                pl.BlockSpec((1, d), lambda i, off: (0, 0)),
            ],
            out_specs=pl.BlockSpec((rows, d), lambda i, off: (i, 0)),
            scratch_shapes=[pltpu.VMEM((nbuf, ne * MOE_SLOTS, d), BF16), pltpu.VMEM((rows, d), F32),
                            pltpu.SemaphoreType.DMA((nbuf,))],
        ),
        out_shape=jax.ShapeDtypeStruct((n, d), F32),
        compiler_params=_cparams(("arbitrary",)),
        name="combine_ple",
    )(offs, h1, aff3, pos3, p_all, ye, ple_norm.reshape(1, d), wpg, wpu, final_g.reshape(1, d))


def _moe_block(h1, xn, aff3, p_all, w, i, *, final_norm):
    n, d = h1.shape
    cap = EC_CAPACITY * n // N_EXPERTS
    pos3, offs = _select(aff3, cap=cap)
    xe = _dispatch(offs, xn, pos3, cap=cap)
    ye = _expert_ffn(xe, w["exp_w_gate"][i], w["exp_w_up"][i], w["exp_w_down"][i], cap=cap)
    return _combine(offs, h1, aff3, pos3, p_all, ye, w["ple_norm"][i], w["ple_w_gate"][i], w["ple_w_up"][i],
                    w["final_norm"], layer=i, cap=cap, final_norm=final_norm)


def _rope_tables(seq, dim):
    inv = ROPE_THETA ** (-jnp.arange(0, dim, 2, dtype=F32) / dim)
    ang = jnp.arange(seq, dtype=F32)[:, None] * inv[None, :]
    return jnp.cos(ang), jnp.sin(ang)


def _trunk(x, p, w):
    batch, seq, d = x.shape
    n = batch * seq
    h = x.reshape(n, d)
    dk = d // RET_HEADS
    cos, sin = _rope_tables(seq, dk)

    proj = _in_proj(h, w["mix_norm"][0], w["ret_w_in"][0], cos, sin, seq=seq, rope_units=2,
                    unit_scales=(1.0, float(dk) ** -0.5, 1.0, 1.0, 1.0, 1.0))
    a = _retention(proj, w["ret_decay_logit"][0], w["ret_gn_w"][0], batch=batch, seq=seq, d_model=d)
    h1, xn, aff3 = _out_router(a, w["ret_w_out"][0], h, w["ffn_norm"][0], w["router_w"][0])
    p_all = p.reshape(p.shape[0], n, p.shape[-1])
    h = _moe_block(h1, xn, aff3, p_all, w, 0, final_norm=False)

    qkv = _in_proj(h, w["mix_norm"][1], w["na_w_in"][0], cos, sin, seq=seq, rope_units=0,
                   unit_scales=(float(NA_HEAD_DIM) ** -0.5, 1.0, 1.0))
    a = _na_attention(qkv, _na_bias_table(w["na_rpb"][0], seq // GRID_W), batch=batch, seq=seq, d_model=d)
    h1, xn, aff3 = _out_router(a, w["na_w_out"][0], h, w["ffn_norm"][1], w["router_w"][1])
    y = _moe_block(h1, xn, aff3, p_all, w, 1, final_norm=True)
    return y.reshape(batch, seq, d)


def kernel(x_prompt, x_sample, p_prompt, p_sample, ret_w_in, ret_decay_logit, ret_gn_w, ret_w_out, na_w_in, na_rpb, na_w_out, mix_norm, ffn_norm, ple_norm, router_w, exp_w_gate, exp_w_up, exp_w_down, ple_w_up, ple_w_gate, final_norm):
    w = dict(
        ret_w_in=ret_w_in.astype(BF16), ret_decay_logit=ret_decay_logit, ret_gn_w=ret_gn_w,
        ret_w_out=ret_w_out.astype(BF16), na_w_in=na_w_in.astype(BF16), na_rpb=na_rpb,
        na_w_out=na_w_out.astype(BF16), mix_norm=mix_norm, ffn_norm=ffn_norm, ple_norm=ple_norm,
        router_w=router_w, exp_w_gate=exp_w_gate.astype(BF16), exp_w_up=exp_w_up.astype(BF16),
        exp_w_down=exp_w_down.astype(BF16), ple_w_up=ple_w_up.astype(BF16), ple_w_gate=ple_w_gate.astype(BF16),
        final_norm=final_norm,
    )
    return _trunk(x_prompt, p_prompt, w), _trunk(x_sample, p_sample, w)
```

```python
import functools

import numpy as np
import jax
import jax.numpy as jnp
from jax import lax
from jax.experimental import pallas as pl
from jax.experimental.pallas import tpu as pltpu

F32 = jnp.float32
BF16 = jnp.bfloat16
I32 = jnp.int32

NORM_EPS = 1e-6
ROPE_THETA = 10000.0
GRID_W = 64
RET_HEADS = 4
RET_CHUNK = 256
NA_HEADS = 16
NA_HEAD_DIM = 64
NA_MAX_ROWS = 8
NA_WIN_COLS = 16
NA_ROW_UNROLL = 8
N_EXPERTS = 16
EC_CAPACITY = 2
LANES = 128
BF16_ROWS = 16
MOE_TILE = 256
MOE_SLOTS = 64
ROUTE_TILES_PER_STEP = 2
NEG_BIG = -1e30

_VMEM_LIMIT = 56 * 1024 * 1024


def _cparams(sem):
    return pltpu.CompilerParams(dimension_semantics=sem, vmem_limit_bytes=_VMEM_LIMIT)


def _rms(x, g):
    return x * lax.rsqrt(jnp.mean(x * x, axis=-1, keepdims=True) + NORM_EPS) * g


def _in_proj_kernel(x_ref, g_ref, w_ref, cos_ref, sin_ref, o_ref, xn_ref, *, nj, unit_w, rope_units, unit_scales,
                    rope_dim):
    j = pl.program_id(1)

    @pl.when(j == 0)
    def _():
        xn_ref[...] = _rms(x_ref[...], g_ref[...]).astype(BF16)

    upb = w_ref.shape[1] // unit_w
    half = rope_dim // 2

    def rotate(y, base, cos, sin):
        for hh in range(unit_w // rope_dim):
            lo = hh * rope_dim
            x1 = y[:, lo:lo + half]
            x2 = y[:, lo + half:lo + rope_dim]
            o_ref[:, base + lo:base + lo + half] = (x1 * cos - x2 * sin).astype(o_ref.dtype)
            o_ref[:, base + lo + half:base + lo + rope_dim] = (x1 * sin + x2 * cos).astype(o_ref.dtype)

    for c in range(upb):
        base = c * unit_w
        y = jnp.dot(xn_ref[...], w_ref[:, base:base + unit_w], preferred_element_type=F32)
        kinds = [(jj * upb + c < rope_units, unit_scales[jj * upb + c]) for jj in range(nj)]
        if all(k == kinds[0] for k in kinds):
            is_rope, scale = kinds[0]
            if is_rope:
                rotate(y, base, cos_ref[...] * scale, sin_ref[...] * scale)
            else:
                o_ref[:, base:base + unit_w] = (y if scale == 1.0 else y * scale).astype(o_ref.dtype)
        else:
            rope_here = jnp.bool_(False)
            scale = jnp.float32(1.0)
            for jj, (is_rope, sc) in enumerate(kinds):
                rope_here = jnp.where(j == jj, is_rope, rope_here)
                scale = jnp.where(j == jj, jnp.float32(sc), scale)
            rotate(y, base, jnp.where(rope_here, cos_ref[...], 1.0) * scale, jnp.where(rope_here, sin_ref[...], 0.0) * scale)


def _in_proj(x, g, w_bf16, cos, sin, *, seq, rope_units, unit_scales, tm=1024, units_per_step=3):
    n, d = x.shape
    tm = min(tm, seq)
    ncol = w_bf16.shape[1]
    tn = units_per_step * d
    nj = ncol // tn
    assert len(unit_scales) == ncol // d
    rope_dim = 2 * cos.shape[1]
    nseq = seq // tm
    kern = functools.partial(_in_proj_kernel, nj=nj, unit_w=d, rope_units=rope_units, unit_scales=unit_scales,
                             rope_dim=rope_dim)
    return pl.pallas_call(
        kern,
        grid=(n // tm, ncol // tn),
        in_specs=[
            pl.BlockSpec((tm, d), lambda i, j: (i, 0)),
            pl.BlockSpec((1, d), lambda i, j: (0, 0)),
            pl.BlockSpec((d, tn), lambda i, j: (0, j)),
            pl.BlockSpec((tm, cos.shape[1]), lambda i, j: (i % nseq, 0)),
            pl.BlockSpec((tm, cos.shape[1]), lambda i, j: (i % nseq, 0)),
        ],
        out_specs=pl.BlockSpec((tm, tn), lambda i, j: (i, j)),
        out_shape=jax.ShapeDtypeStruct((n, ncol), BF16),
        scratch_shapes=[pltpu.VMEM((tm, d), BF16)],
        compiler_params=_cparams(("parallel", "arbitrary")),
        name="in_proj",
    )(x, g.reshape(1, d), w_bf16, cos, sin)


def _retention_kernel(dl_ref, q_ref, k_ref, v_ref, g_ref, gnw_ref, o_ref, acc_ref, stf_ref, stb_ref, *, seq, chunk):
    h = pl.program_id(1)
    nc = seq // chunk
    dl = dl_ref[...]
    lg = jnp.minimum(dl, 0.0) - jnp.log1p(jnp.exp(-jnp.abs(dl)))
    col = lax.broadcasted_iota(I32, dl.shape, 1)
    lgh = jnp.sum(jnp.where(col == h, lg, 0.0), axis=1, keepdims=True)
    lgf = lgh[0:1, :]
    lgb = lgh[1:2, :]

    ri = lax.broadcasted_iota(I32, (chunk, chunk), 0)
    ci = lax.broadcasted_iota(I32, (chunk, chunk), 1)
    diff = (ri - ci).astype(F32)
    dmat = jnp.exp(jnp.where(diff >= 0, diff * lgf, -diff * lgb))
    pos = lax.broadcasted_iota(I32, (chunk, 1), 0).astype(F32)
    qdec_f = jnp.exp((pos + 1.0) * lgf)
    kdec_f = jnp.exp((chunk - 1.0 - pos) * lgf)
    qdec_b = jnp.exp((chunk - pos) * lgb)
    kdec_b = jnp.exp(pos * lgb)
    cdec_f = jnp.exp(chunk * lgf)
    cdec_b = jnp.exp(chunk * lgb)
    nt = (((1,), (1,)), ((), ()))
    tn = (((0,), (0,)), ((), ()))

    stf_ref[...] = jnp.zeros_like(stf_ref)
    stb_ref[...] = jnp.zeros_like(stb_ref)

    def chunk_slice(c):
        return pl.ds(pl.multiple_of(c * chunk, chunk), chunk)

    def scaled(x, dec):
        return (x.astype(F32) * dec).astype(BF16)

    def step(t):
        slf = chunk_slice(t)
        slb = chunk_slice(nc - 1 - t)
        qf, kf, vf = q_ref[slf, :], k_ref[slf, :], v_ref[slf, :]
        qb, kb, vb = q_ref[slb, :], k_ref[slb, :], v_ref[slb, :]
        s = lax.dot_general(qf, kf, nt, preferred_element_type=F32) * dmat
        cross_b = jnp.dot(scaled(qb, qdec_b), stb_ref[...].astype(BF16), preferred_element_type=F32)
        inner = jnp.dot(s.astype(BF16), vf, preferred_element_type=F32)
        cross_f = jnp.dot(scaled(qf, qdec_f), stf_ref[...].astype(BF16), preferred_element_type=F32)
        stb_ref[...] = stb_ref[...] * cdec_b + lax.dot_general(scaled(kb, kdec_b), vb, tn, preferred_element_type=F32)
        stf_ref[...] = stf_ref[...] * cdec_f + lax.dot_general(scaled(kf, kdec_f), vf, tn, preferred_element_type=F32)
        return slf, inner + cross_f, slb, cross_b

    def finish(sl, o):
        mu = jnp.mean(o, axis=-1, keepdims=True)
        oc = o - mu
        var = jnp.mean(oc * oc, axis=-1, keepdims=True)
        on = oc * lax.rsqrt(var + NORM_EPS) * gnw_ref[...]
        gate = g_ref[sl, :].astype(F32)
        o_ref[sl, :] = (gate * jax.nn.sigmoid(gate) * on).astype(o_ref.dtype)

    def first_half(t, carry):
        slf, of, slb, ob = step(t)
        acc_ref[slf, :] = of
        acc_ref[slb, :] = ob
        return carry

    def second_half(t, carry):
        slf, of, slb, ob = step(t)
        finish(slf, acc_ref[slf, :] + of)
        finish(slb, acc_ref[slb, :] + ob)
        return carry

    lax.fori_loop(0, nc // 2, first_half, 0)
    lax.fori_loop(nc // 2, nc, second_half, 0)


def _retention(proj, decay_logit, gn_w, *, batch, seq, d_model):
    heads = RET_HEADS
    dk = d_model // heads
    dv = 2 * d_model // heads
    n = batch * seq
    assert seq % (2 * RET_CHUNK) == 0, "the two scans are paired chunk by chunk"
    kern = functools.partial(_retention_kernel, seq=seq, chunk=RET_CHUNK)
    return pl.pallas_call(
        kern,
        grid=(batch, heads),
        in_specs=[
            pl.BlockSpec((2, heads), lambda b, h: (0, 0)),
            pl.BlockSpec((seq, dk), lambda b, h: (b, h)),
            pl.BlockSpec((seq, dk), lambda b, h: (b, heads + h)),
            pl.BlockSpec((seq, dv), lambda b, h: (b, heads + h)),
            pl.BlockSpec((seq, dv), lambda b, h: (b, 2 * heads + h)),
            pl.BlockSpec((1, dv), lambda b, h: (0, h)),
        ],
        out_specs=pl.BlockSpec((seq, dv), lambda b, h: (b, h)),
        out_shape=jax.ShapeDtypeStruct((n, 2 * d_model), BF16),
        scratch_shapes=[pltpu.VMEM((seq, dv), F32), pltpu.VMEM((dk, dv), F32), pltpu.VMEM((dk, dv), F32)],
        compiler_params=_cparams(("parallel", "parallel")),
        name="retention",
    )(decay_logit, proj, proj, proj, proj, gn_w.reshape(1, -1))


def _na_kernel(bias_ref, q_ref, k_ref, v_ref, o_ref, *, rows):
    gw = GRID_W
    kr = min(NA_MAX_ROWS, rows)
    dh = NA_HEAD_DIM
    lane_q = lax.broadcasted_iota(I32, (gw, 2 * dh), 1)
    nt = (((1,), (1,)), ((), ()))

    group = min(NA_ROW_UNROLL, rows)

    def body(it, carry):
        rr = [it * group + i for i in range(group)]
        rs = [jnp.clip(r - kr // 2, 0, rows - kr) for r in rr]
        s = []
        for r, r0 in zip(rr, rs):
            q2 = q_ref[pl.ds(pl.multiple_of(r * gw, gw), gw), :]
            zero = jnp.zeros_like(q2)
            qq = jnp.concatenate([jnp.where(lane_q < dh, q2, zero), jnp.where(lane_q >= dh, q2, zero)], axis=0)
            k2 = k_ref[pl.ds(pl.multiple_of(r0 * gw, gw), kr * gw), :]
            s.append(lax.dot_general(qq, k2, nt, preferred_element_type=F32) + bias_ref[0, r - r0])
        m = [jnp.max(si, axis=-1, keepdims=True) for si in s]
        p = [jnp.exp(si - mi) for si, mi in zip(s, m)]
        inv = [1.0 / jnp.sum(pi, axis=-1, keepdims=True) for pi in p]
        o = [jnp.dot(pi.astype(BF16), v_ref[pl.ds(pl.multiple_of(r0 * gw, gw), kr * gw), :],
                     preferred_element_type=F32) * ii for pi, ii, r0 in zip(p, inv, rs)]
        for r, oi in zip(rr, o):
            out = jnp.where(lane_q < dh, oi[0:gw, :], oi[gw:2 * gw, :])
            o_ref[pl.ds(pl.multiple_of(r * gw, gw), gw), :] = out.astype(o_ref.dtype)
        return carry

    lax.fori_loop(0, rows // group, body, 0)


def _na_bias_table(rpb, rows):
    kr = min(NA_MAX_ROWS, rows)
    c = np.arange(GRID_W)
    kc = np.arange(GRID_W)
    win_start = np.clip(c - NA_WIN_COLS // 2, 0, GRID_W - NA_WIN_COLS)
    valid = (kc[None, :] >= win_start[:, None]) & (kc[None, :] < win_start[:, None] + NA_WIN_COLS)
    dc_idx = np.clip(kc[None, :] - c[:, None] + NA_WIN_COLS - 1, 0, 2 * NA_WIN_COLS - 2)
    delta = np.arange(kr)
    a = np.arange(kr)
    dr_idx = a[None, :] - delta[:, None] + NA_MAX_ROWS - 1
    row_sel = (dr_idx[:, :, None] == np.arange(rpb.shape[1])).astype(np.float32)
    col_sel = (dc_idx[None, :, :] == np.arange(rpb.shape[2])[:, None, None]).astype(np.float32)
    t = jnp.einsum("dar,hrs,sck->hdcak", row_sel, rpb.astype(F32), col_sel, precision=lax.Precision.HIGHEST)
    t = jnp.where(jnp.asarray(valid)[None, None, :, None, :], t, NEG_BIG)
    h = rpb.shape[0]
    t = t.reshape(h // 2, 2, kr, GRID_W, kr * GRID_W).transpose(0, 2, 1, 3, 4)
    return t.reshape(h // 2, kr, 2 * GRID_W, kr * GRID_W)


def _na_attention(qkv, bias, *, batch, seq, d_model):
    n = batch * seq
    rows = seq // GRID_W
    kr = min(NA_MAX_ROWS, rows)
    pairs = NA_HEADS // 2
    pw = 2 * NA_HEAD_DIM
    assert rows % min(NA_ROW_UNROLL, rows) == 0
    kern = functools.partial(_na_kernel, rows=rows)
    return pl.pallas_call(
        kern,
        grid=(batch, pairs),
        in_specs=[
            pl.BlockSpec((1, kr, 2 * GRID_W, kr * GRID_W), lambda b, hp: (hp, 0, 0, 0)),
            pl.BlockSpec((seq, pw), lambda b, hp: (b, hp)),
            pl.BlockSpec((seq, pw), lambda b, hp: (b, pairs + hp)),
            pl.BlockSpec((seq, pw), lambda b, hp: (b, 2 * pairs + hp)),
        ],
        out_specs=pl.BlockSpec((seq, pw), lambda b, hp: (b, hp)),
        out_shape=jax.ShapeDtypeStruct((n, d_model), BF16),
        compiler_params=_cparams(("parallel", "parallel")),
        name="na_attention",
    )(bias, qkv, qkv, qkv)


def _out_router_kernel(a_ref, w_ref, h_ref, g_ref, rw_ref, h1_ref, xn_ref, aff_ref):
    y = jnp.dot(a_ref[...], w_ref[...], preferred_element_type=F32) + h_ref[...]
    h1_ref[...] = y
    xn = _rms(y, g_ref[...])
    xn_ref[...] = xn.astype(xn_ref.dtype)
    nt = (((1,), (1,)), ((), ()))
    ne = rw_ref.shape[0]
    xh = xn.astype(BF16)
    xl = (xn - xh.astype(F32)).astype(BF16)
    rw = rw_ref[...]
    rh = rw.astype(BF16)
    rl = (rw - rh.astype(F32)).astype(BF16)
    t1 = lax.dot_general(jnp.concatenate([rh, rl], axis=0), xh, nt, preferred_element_type=F32)
    t2 = lax.dot_general(rh, xl, nt, preferred_element_type=F32)
    logits = t1[0:ne, :] + (t1[ne:2 * ne, :] + t2)
    m = jnp.max(logits, axis=0, keepdims=True)
    e = jnp.exp(logits - m)
    aff = e / jnp.sum(e, axis=0, keepdims=True)
    for t in range(aff_ref.shape[0]):
        aff_ref[t] = aff[:, t * MOE_TILE:(t + 1) * MOE_TILE]


def _out_router(a, w_bf16, h, g, router_w, *, tm=1024):
    n, kdim = a.shape
    tm = min(tm, n)
    d = h.shape[1]
    ne = router_w.shape[1]
    tpb = tm // MOE_TILE
    return pl.pallas_call(
        _out_router_kernel,
        grid=(n // tm,),
        in_specs=[
            pl.BlockSpec((tm, kdim), lambda i: (i, 0)),
            pl.BlockSpec((kdim, d), lambda i: (0, 0)),
            pl.BlockSpec((tm, d), lambda i: (i, 0)),
            pl.BlockSpec((1, d), lambda i: (0, 0)),
            pl.BlockSpec((ne, d), lambda i: (0, 0)),
        ],
        out_specs=[
            pl.BlockSpec((tm, d), lambda i: (i, 0)),
            pl.BlockSpec((tm, d), lambda i: (i, 0)),
            pl.BlockSpec((tpb, ne, MOE_TILE), lambda i: (i, 0, 0)),
        ],
        out_shape=[
            jax.ShapeDtypeStruct((n, d), F32),
            jax.ShapeDtypeStruct((n, d), BF16),
            jax.ShapeDtypeStruct((n // MOE_TILE, ne, MOE_TILE), F32),
        ],
        compiler_params=_cparams(("parallel",)),
        name="out_router",
    )(a, w_bf16, h, g.reshape(1, d), router_w.T)


def _select_kernel(aff_ref, pos_ref, off_ref, *, cap):
    ntile, ne, tt = aff_ref.shape

    def count(pred_fn):
        def body(c, acc):
            x = pltpu.bitcast(aff_ref[c], I32)
            return acc + pred_fn(x).astype(I32)
        acc = lax.fori_loop(0, ntile, body, jnp.zeros((ne, tt), I32), unroll=4)
        return jnp.sum(acc, axis=1, keepdims=True)

    def bit_step(i, t):
        cand = t | jnp.left_shift(jnp.int32(1), 30 - i)
        return jnp.where(count(lambda x: x >= cand) >= cap, cand, t)

    thr = lax.fori_loop(0, 31, bit_step, jnp.zeros((ne, 1), I32))
    need_eq = (cap - count(lambda x: x > thr)).astype(F32)

    li = lax.broadcasted_iota(I32, (tt, tt), 0)
    lj = lax.broadcasted_iota(I32, (tt, tt), 1)
    upper = (li < lj).astype(BF16)

    def tile_body(c, carry):
        eq_carry, pos_carry = carry
        off_ref[c] = jnp.broadcast_to(pos_carry, (ne, LANES)).astype(I32)
        x = pltpu.bitcast(aff_ref[c], I32)
        gt = x > thr
        eq = x == thr
        gtf = gt.astype(F32)
        eqf = eq.astype(F32)
        pre = jnp.dot(jnp.concatenate([gtf, eqf], axis=0).astype(BF16), upper, preferred_element_type=F32)
        eq_rank = eq_carry + pre[ne:2 * ne, :]
        taken = eq & (eq_rank < need_eq)
        taken_before = jnp.minimum(eq_rank, need_eq) - jnp.minimum(eq_carry, need_eq)
        pos = pos_carry + pre[0:ne, :] + taken_before
        pos_ref[c] = jnp.where(gt | taken, pos.astype(I32), -1)
        eq_next = eq_carry + jnp.sum(eqf, axis=1, keepdims=True)
        taken_total = jnp.minimum(eq_next, need_eq) - jnp.minimum(eq_carry, need_eq)
        return eq_next, pos_carry + jnp.sum(gtf, axis=1, keepdims=True) + taken_total

    lax.fori_loop(0, ntile, tile_body, (jnp.zeros((ne, 1), F32), jnp.zeros((ne, 1), F32)), unroll=2)


def _select(aff3, *, cap):
    ntile, ne, tt = aff3.shape
    kern = functools.partial(_select_kernel, cap=cap)
    pos3, off3 = pl.pallas_call(
        kern,
        out_shape=[
            jax.ShapeDtypeStruct((ntile, ne, tt), I32),
            jax.ShapeDtypeStruct((ntile, ne, LANES), I32),
        ],
        compiler_params=pltpu.CompilerParams(vmem_limit_bytes=_VMEM_LIMIT),
        name="ec_select",
    )(aff3)
    offs = jnp.concatenate([off3[:, :, 0], jnp.full((1, ne), cap, I32)], axis=0).reshape(-1)
    return pos3, offs


def _tile_windows(off_ref, s, ne):
    starts, used = [], []
    for e in range(ne):
        off = off_ref[s * ne + e]
        nxt = off_ref[(s + 1) * ne + e]
        st = (off // BF16_ROWS) * BF16_ROWS
        starts.append(st)
        used.append(nxt - st)
    return starts, used


def _num_rounds(used):
    m = used[0]
    for u in used[1:]:
        m = jnp.maximum(m, u)
    return (m + MOE_SLOTS - 1) // MOE_SLOTS


def _dispatch_kernel(off_ref, x_ref, pos_ref, xe_ref, wins_ref, carry_ref, sem, nout_ref, *, cap):
    s = pl.program_id(0)
    nsub, ne, tt = pos_ref.shape
    gr = BF16_ROWS
    gpr = MOE_SLOTS // gr

    @pl.when(s == 0)
    def _():
        carry_ref[...] = jnp.zeros_like(carry_ref)
        nout_ref[0] = 0
        nout_ref[1] = 0

    slot = lax.broadcasted_iota(I32, (MOE_SLOTS, tt), 0)

    def drain(b):
        def wait_one(i, c):
            pltpu.make_async_copy(wins_ref.at[b, pl.ds(0, gr)], xe_ref.at[pl.ds(0, gr)], sem.at[b]).wait()
            return c
        lax.fori_loop(0, nout_ref[b], wait_one, 0)
        nout_ref[b] = 0

    for k in range(nsub):
        tile = s * nsub + k
        starts, used = _tile_windows(off_ref, tile, ne)
        pos = pos_ref[k]
        x = x_ref[pl.ds(k * tt, tt), :]

        def round_body(w, c, tile=tile, starts=starts, used=used, pos=pos, x=x):
            b = (tile + w) % 2
            drain(b)
            pieces = []
            for e in range(ne):
                key = pos[e:e + 1, :] - (starts[e] + w * MOE_SLOTS)
                pieces.append((jnp.broadcast_to(key, (MOE_SLOTS, tt)) == slot).astype(F32))
            onehot = jnp.concatenate(pieces, axis=0).astype(BF16)
            wins_ref[b] = jnp.dot(onehot, x, preferred_element_type=F32).astype(BF16)
            first = w == 0
            issued = 0
            for e in range(ne):
                head = pl.ds(e * MOE_SLOTS, gr)
                crows = pl.ds(e * gr, gr)
                carry = carry_ref[crows, :]
                merged = (wins_ref[b, head, :].astype(F32) + carry.astype(F32)).astype(BF16)
                wins_ref[b, head, :] = jnp.where(first, merged, wins_ref[b, head, :])

                ngc = used[e] // gr
                rem = used[e] % gr
                ng_w = jnp.clip(ngc - w * gpr, 0, gpr)

                def issue(j, c2, e=e):
                    src = wins_ref.at[b, pl.ds(e * MOE_SLOTS + j * gr, gr)]
                    dst = xe_ref.at[pl.ds(pl.multiple_of(e * cap + starts[e] + w * MOE_SLOTS + j * gr, gr), gr)]
                    pltpu.make_async_copy(src, dst, sem.at[b]).start()
                    return c2

                lax.fori_loop(0, ng_w, issue, 0)
                issued = issued + ng_w

                part = wins_ref[b, pl.ds(e * MOE_SLOTS + jnp.clip(ngc - w * gpr, 0, gpr - 1) * gr, gr), :]
                keep_part = (rem > 0) & (ngc // gpr == w)
                clear = (rem == 0) & first
                carry_ref[crows, :] = jnp.where(keep_part, part, jnp.where(clear, jnp.zeros_like(carry), carry))

            nout_ref[b] = issued
            return c

        lax.fori_loop(0, _num_rounds(used), round_body, 0)

    @pl.when(s == pl.num_programs(0) - 1)
    def _():
        drain(0)
        drain(1)


def _dispatch(offs, xn, pos3, *, cap):
    n, d = xn.shape
    ntile, ne, tt = pos3.shape
    nsub = min(ROUTE_TILES_PER_STEP, ntile)
    assert ntile % nsub == 0
    kern = functools.partial(_dispatch_kernel, cap=cap)
    return pl.pallas_call(
        kern,
        grid_spec=pltpu.PrefetchScalarGridSpec(
            num_scalar_prefetch=1,
            grid=(ntile // nsub,),
            in_specs=[
                pl.BlockSpec((nsub * tt, d), lambda i, off: (i, 0)),
                pl.BlockSpec((nsub, ne, tt), lambda i, off: (i, 0, 0)),
            ],
            out_specs=pl.BlockSpec(memory_space=pl.ANY),
            scratch_shapes=[
                pltpu.VMEM((2, ne * MOE_SLOTS, d), BF16),
                pltpu.VMEM((ne * BF16_ROWS, d), BF16),
                pltpu.SemaphoreType.DMA((2,)),
                pltpu.SMEM((2,), I32),
            ],
        ),
        out_shape=jax.ShapeDtypeStruct((ne * cap, d), BF16),
        compiler_params=_cparams(("arbitrary",)),
        name="ec_dispatch",
    )(offs, xn, pos3)


def _ffn_kernel(x_ref, wg_ref, wu_ref, wd_ref, o_ref):
    x = x_ref[...]
    a = jnp.dot(x, wg_ref[0], preferred_element_type=F32)
    u = jnp.dot(x, wu_ref[0], preferred_element_type=F32)
    hid = (a * jax.nn.sigmoid(a) * u).astype(BF16)
    o_ref[...] = jnp.dot(hid, wd_ref[0], preferred_element_type=F32).astype(o_ref.dtype)


def _expert_ffn(xe, wg, wu, wd, *, cap, tr=1024):
    ne, d, f = wg.shape
    tr = min(tr, cap)
    nt = cap // tr
    return pl.pallas_call(
        _ffn_kernel,
        grid=(ne, nt),
        in_specs=[
            pl.BlockSpec((tr, d), lambda e, t: (e * nt + t, 0)),
            pl.BlockSpec((1, d, f), lambda e, t: (e, 0, 0)),
            pl.BlockSpec((1, d, f), lambda e, t: (e, 0, 0)),
            pl.BlockSpec((1, f, d), lambda e, t: (e, 0, 0)),
        ],
        out_specs=pl.BlockSpec((tr, d), lambda e, t: (e * nt + t, 0)),
        out_shape=jax.ShapeDtypeStruct((ne * cap, d), BF16),
        compiler_params=_cparams(("parallel", "parallel")),
        name="expert_ffn",
    )(xe, wg, wu, wd)


def _combine_kernel(off_ref, h_ref, aff_ref, pos_ref, p_ref, ye_ref, png_ref, wpg_ref, wpu_ref, fng_ref,
                    o_ref, wins_ref, acc_ref, sem, *, cap, final_norm):
    s = pl.program_id(0)
    ntile = pl.num_programs(0)
    nsub, ne, tt = pos_ref.shape
    spill_buf = 2 * nsub

    def window_copies(starts_t, w, buf):
        copies, wstarts = [], []
        for e in range(ne):
            ws = pl.multiple_of(jnp.minimum(starts_t[e] + w * MOE_SLOTS, cap - MOE_SLOTS), BF16_ROWS)
            copies.append(pltpu.make_async_copy(
                ye_ref.at[pl.ds(pl.multiple_of(e * cap + ws, BF16_ROWS), MOE_SLOTS)],
                wins_ref.at[buf, pl.ds(e * MOE_SLOTS, MOE_SLOTS)], sem.at[buf]))
            wstarts.append(ws)
        return copies, wstarts

    cur = s % 2
    tiles = [_tile_windows(off_ref, s * nsub + k, ne) for k in range(nsub)]

    @pl.when(s == 0)
    def _():
        for k in range(nsub):
            for cp in window_copies(tiles[k][0], 0, cur * nsub + k)[0]:
                cp.start()

    @pl.when(s + 1 < ntile)
    def _():
        for k in range(nsub):
            nxt_starts, _ = _tile_windows(off_ref, (s + 1) * nsub + k, ne)
            for cp in window_copies(nxt_starts, 0, (1 - cur) * nsub + k)[0]:
                cp.start()

    slot = lax.broadcasted_iota(I32, (MOE_SLOTS, tt), 0)
    tn = (((0,), (0,)), ((), ()))

    def gates(k, w, wstarts):
        starts = tiles[k][0]
        pos = pos_ref[k]
        aff = aff_ref[k]
        pieces = []
        for e in range(ne):
            base = starts[e] + w * MOE_SLOTS
            pe = pos[e:e + 1, :]
            in_round = (pe >= base) & (pe < base + MOE_SLOTS)
            hit = (jnp.broadcast_to(pe - wstarts[e], (MOE_SLOTS, tt)) == slot) & jnp.broadcast_to(in_round, (MOE_SLOTS, tt))
            pieces.append(jnp.where(hit, jnp.broadcast_to(aff[e:e + 1, :], (MOE_SLOTS, tt)), 0.0))
        return jnp.concatenate(pieces, axis=0).astype(BF16)

    for k in range(nsub):
        starts, used = tiles[k]
        rows = pl.ds(k * tt, tt)
        buf0 = cur * nsub + k
        copies0, wstarts0 = window_copies(starts, 0, buf0)
        a0 = gates(k, 0, wstarts0)
        for cp in copies0:
            cp.wait()
        acc_ref[rows, :] = lax.dot_general(a0, wins_ref[buf0], tn, preferred_element_type=F32)

        def round_body(w, c, k=k, starts=starts, rows=rows):
            copies, wstarts = window_copies(starts, w, spill_buf)
            for cp in copies:
                cp.start()
            a_w = gates(k, w, wstarts)
            for cp in copies:
                cp.wait()
            acc_ref[rows, :] += lax.dot_general(a_w, wins_ref[spill_buf], tn, preferred_element_type=F32)
            return c

        lax.fori_loop(1, _num_rounds(used), round_body, 0)
    h2 = h_ref[...] + acc_ref[...]
    gate = jax.nn.sigmoid(jnp.dot(_rms(h2, png_ref[...]).astype(BF16), wpg_ref[...], preferred_element_type=F32))
    up = jnp.dot(p_ref[0].astype(BF16), wpu_ref[...], preferred_element_type=F32)
    h3 = h2 + up * gate
    if final_norm:
        h3 = _rms(h3, fng_ref[...])
    o_ref[...] = h3


def _combine(offs, h1, aff3, pos3, p_all, ye, ple_norm, wpg, wpu, final_g, *, layer, cap, final_norm):
    n, d = h1.shape
    ntile, ne, tt = pos3.shape
    pd = p_all.shape[2]
    nsub = min(ROUTE_TILES_PER_STEP, ntile)
    assert ntile % nsub == 0
    rows = nsub * tt
    nbuf = 2 * nsub + 1
    kern = functools.partial(_combine_kernel, cap=cap, final_norm=final_norm)
    return pl.pallas_call(
        kern,
        grid_spec=pltpu.PrefetchScalarGridSpec(
            num_scalar_prefetch=1,
            grid=(ntile // nsub,),
            in_specs=[
                pl.BlockSpec((rows, d), lambda i, off: (i, 0)),
                pl.BlockSpec((nsub, ne, tt), lambda i, off: (i, 0, 0)),
                pl.BlockSpec((nsub, ne, tt), lambda i, off: (i, 0, 0)),
                pl.BlockSpec((1, rows, pd), lambda i, off: (layer, i, 0)),
                pl.BlockSpec(memory_space=pl.ANY),
                pl.BlockSpec((1, d), lambda i, off: (0, 0)),
                pl.BlockSpec((d, d), lambda i, off: (0, 0)),
                pl.BlockSpec((pd, d), lambda i, off: (0, 0)),
                pl.BlockSpec((1, d), lambda i, off: (0, 0)),
            ],
            out_specs=pl.BlockSpec((rows, d), lambda i, off: (i, 0)),
            scratch_shapes=[pltpu.VMEM((nbuf, ne * MOE_SLOTS, d), BF16), pltpu.VMEM((rows, d), F32),
                            pltpu.SemaphoreType.DMA((nbuf,))],
        ),
        out_shape=jax.ShapeDtypeStruct((n, d), F32),
        compiler_params=_cparams(("arbitrary",)),
        name="combine_ple",
    )(offs, h1, aff3, pos3, p_all, ye, ple_norm.reshape(1, d), wpg, wpu, final_g.reshape(1, d))


def _moe_block(h1, xn, aff3, p_all, w, i, *, final_norm):
    n, d = h1.shape
    cap = EC_CAPACITY * n // N_EXPERTS
    pos3, offs = _select(aff3, cap=cap)
    xe = _dispatch(offs, xn, pos3, cap=cap)
    ye = _expert_ffn(xe, w["exp_w_gate"][i], w["exp_w_up"][i], w["exp_w_down"][i], cap=cap)
    return _combine(offs, h1, aff3, pos3, p_all, ye, w["ple_norm"][i], w["ple_w_gate"][i], w["ple_w_up"][i],
                    w["final_norm"], layer=i, cap=cap, final_norm=final_norm)


def _rope_tables(seq, dim):
    inv = ROPE_THETA ** (-jnp.arange(0, dim, 2, dtype=F32) / dim)
    ang = jnp.arange(seq, dtype=F32)[:, None] * inv[None, :]
    return jnp.cos(ang), jnp.sin(ang)


def _trunk(x, p, w):
    batch, seq, d = x.shape
    n = batch * seq
    h = x.reshape(n, d)
    dk = d // RET_HEADS
    cos, sin = _rope_tables(seq, dk)

    proj = _in_proj(h, w["mix_norm"][0], w["ret_w_in"][0], cos, sin, seq=seq, rope_units=2,
                    unit_scales=(1.0, float(dk) ** -0.5, 1.0, 1.0, 1.0, 1.0))
    a = _retention(proj, w["ret_decay_logit"][0], w["ret_gn_w"][0], batch=batch, seq=seq, d_model=d)
    h1, xn, aff3 = _out_router(a, w["ret_w_out"][0], h, w["ffn_norm"][0], w["router_w"][0])
    p_all = p.reshape(p.shape[0], n, p.shape[-1])
    h = _moe_block(h1, xn, aff3, p_all, w, 0, final_norm=False)

    qkv = _in_proj(h, w["mix_norm"][1], w["na_w_in"][0], cos, sin, seq=seq, rope_units=0,
                   unit_scales=(float(NA_HEAD_DIM) ** -0.5, 1.0, 1.0))
    a = _na_attention(qkv, _na_bias_table(w["na_rpb"][0], seq // GRID_W), batch=batch, seq=seq, d_model=d)
    h1, xn, aff3 = _out_router(a, w["na_w_out"][0], h, w["ffn_norm"][1], w["router_w"][1])
    y = _moe_block(h1, xn, aff3, p_all, w, 1, final_norm=True)
    return y.reshape(batch, seq, d)


def kernel(x_prompt, x_sample, p_prompt, p_sample, ret_w_in, ret_decay_logit, ret_gn_w, ret_w_out, na_w_in, na_rpb, na_w_out, mix_norm, ffn_norm, ple_norm, router_w, exp_w_gate, exp_w_up, exp_w_down, ple_w_up, ple_w_gate, final_norm):
    w = dict(
        ret_w_in=ret_w_in.astype(BF16), ret_decay_logit=ret_decay_logit, ret_gn_w=ret_gn_w,
        ret_w_out=ret_w_out.astype(BF16), na_w_in=na_w_in.astype(BF16), na_rpb=na_rpb,
        na_w_out=na_w_out.astype(BF16), mix_norm=mix_norm, ffn_norm=ffn_norm, ple_norm=ple_norm,
        router_w=router_w, exp_w_gate=exp_w_gate.astype(BF16), exp_w_up=exp_w_up.astype(BF16),
        exp_w_down=exp_w_down.astype(BF16), ple_w_up=ple_w_up.astype(BF16), ple_w_gate=ple_w_gate.astype(BF16),
        final_norm=final_norm,
    )
    return _trunk(x_prompt, p_prompt, w), _trunk(x_sample, p_sample, w)
```

```python
import functools

import numpy as np
import jax
import jax.numpy as jnp
from jax import lax
from jax.experimental import pallas as pl
from jax.experimental.pallas import tpu as pltpu

F32 = jnp.float32
BF16 = jnp.bfloat16
I32 = jnp.int32

NORM_EPS = 1e-6
ROPE_THETA = 10000.0
GRID_W = 64
RET_HEADS = 4
RET_CHUNK = 256
NA_HEADS = 16
NA_HEAD_DIM = 64
NA_MAX_ROWS = 8
NA_WIN_COLS = 16
NA_ROW_UNROLL = 8
N_EXPERTS = 16
EC_CAPACITY = 2
LANES = 128
BF16_ROWS = 16
MOE_TILE = 256
MOE_SLOTS = 64
DISPATCH_TILES_PER_STEP = 4
COMBINE_TILES_PER_STEP = 2
NEG_BIG = -1e30

_VMEM_LIMIT = 56 * 1024 * 1024


def _cparams(sem):
    return pltpu.CompilerParams(dimension_semantics=sem, vmem_limit_bytes=_VMEM_LIMIT)


def _rms(x, g):
    return x * lax.rsqrt(jnp.mean(x * x, axis=-1, keepdims=True) + NORM_EPS) * g


def _in_proj_kernel(x_ref, g_ref, w_ref, cos_ref, sin_ref, o_ref, xn_ref, *, nj, unit_w, rope_units, unit_scales,
                    rope_dim):
    j = pl.program_id(1)

    @pl.when(j == 0)
    def _():
        xn_ref[...] = _rms(x_ref[...], g_ref[...]).astype(BF16)

    upb = w_ref.shape[1] // unit_w
    half = rope_dim // 2

    def rotate(y, base, cos, sin):
        for hh in range(unit_w // rope_dim):
            lo = hh * rope_dim
            x1 = y[:, lo:lo + half]
            x2 = y[:, lo + half:lo + rope_dim]
            o_ref[:, base + lo:base + lo + half] = (x1 * cos - x2 * sin).astype(o_ref.dtype)
            o_ref[:, base + lo + half:base + lo + rope_dim] = (x1 * sin + x2 * cos).astype(o_ref.dtype)

    for c in range(upb):
        base = c * unit_w
        y = jnp.dot(xn_ref[...], w_ref[:, base:base + unit_w], preferred_element_type=F32)
        kinds = [(jj * upb + c < rope_units, unit_scales[jj * upb + c]) for jj in range(nj)]
        if all(k == kinds[0] for k in kinds):
            is_rope, scale = kinds[0]
            if is_rope:
                rotate(y, base, cos_ref[...] * scale, sin_ref[...] * scale)
            else:
                o_ref[:, base:base + unit_w] = (y if scale == 1.0 else y * scale).astype(o_ref.dtype)
        else:
            rope_here = jnp.bool_(False)
            scale = jnp.float32(1.0)
            for jj, (is_rope, sc) in enumerate(kinds):
                rope_here = jnp.where(j == jj, is_rope, rope_here)
                scale = jnp.where(j == jj, jnp.float32(sc), scale)
            rotate(y, base, jnp.where(rope_here, cos_ref[...], 1.0) * scale, jnp.where(rope_here, sin_ref[...], 0.0) * scale)


def _in_proj(x, g, w_bf16, cos, sin, *, seq, rope_units, unit_scales, tm=1024, units_per_step=3):
    n, d = x.shape
    tm = min(tm, seq)
    ncol = w_bf16.shape[1]
    tn = units_per_step * d
    nj = ncol // tn
    assert len(unit_scales) == ncol // d
    rope_dim = 2 * cos.shape[1]
    nseq = seq // tm
    kern = functools.partial(_in_proj_kernel, nj=nj, unit_w=d, rope_units=rope_units, unit_scales=unit_scales,
                             rope_dim=rope_dim)
    return pl.pallas_call(
        kern,
        grid=(n // tm, ncol // tn),
        in_specs=[
            pl.BlockSpec((tm, d), lambda i, j: (i, 0)),
            pl.BlockSpec((1, d), lambda i, j: (0, 0)),
            pl.BlockSpec((d, tn), lambda i, j: (0, j)),
            pl.BlockSpec((tm, cos.shape[1]), lambda i, j: (i % nseq, 0)),
            pl.BlockSpec((tm, cos.shape[1]), lambda i, j: (i % nseq, 0)),
        ],
        out_specs=pl.BlockSpec((tm, tn), lambda i, j: (i, j)),
        out_shape=jax.ShapeDtypeStruct((n, ncol), BF16),
        scratch_shapes=[pltpu.VMEM((tm, d), BF16)],
        compiler_params=_cparams(("parallel", "arbitrary")),
        name="in_proj",
    )(x, g.reshape(1, d), w_bf16, cos, sin)


def _retention_kernel(dl_ref, q_ref, k_ref, v_ref, g_ref, gnw_ref, o_ref, acc_ref, stf_ref, stb_ref, *, seq, chunk):
    h = pl.program_id(1)
    nc = seq // chunk
    dl = dl_ref[...]
    lg = jnp.minimum(dl, 0.0) - jnp.log1p(jnp.exp(-jnp.abs(dl)))
    col = lax.broadcasted_iota(I32, dl.shape, 1)
    lgh = jnp.sum(jnp.where(col == h, lg, 0.0), axis=1, keepdims=True)
    lgf = lgh[0:1, :]
    lgb = lgh[1:2, :]

    ri = lax.broadcasted_iota(I32, (chunk, chunk), 0)
    ci = lax.broadcasted_iota(I32, (chunk, chunk), 1)
    diff = (ri - ci).astype(F32)
    dmat = jnp.exp(jnp.where(diff >= 0, diff * lgf, -diff * lgb))
    pos = lax.broadcasted_iota(I32, (chunk, 1), 0).astype(F32)
    qdec_f = jnp.exp((pos + 1.0) * lgf)
    kdec_f = jnp.exp((chunk - 1.0 - pos) * lgf)
    qdec_b = jnp.exp((chunk - pos) * lgb)
    kdec_b = jnp.exp(pos * lgb)
    cdec_f = jnp.exp(chunk * lgf)
    cdec_b = jnp.exp(chunk * lgb)
    nt = (((1,), (1,)), ((), ()))
    tn = (((0,), (0,)), ((), ()))

    stf_ref[...] = jnp.zeros_like(stf_ref)
    stb_ref[...] = jnp.zeros_like(stb_ref)

    def chunk_slice(c):
        return pl.ds(pl.multiple_of(c * chunk, chunk), chunk)

    def scaled(x, dec):
        return (x.astype(F32) * dec).astype(BF16)

    def step(t):
        slf = chunk_slice(t)
        slb = chunk_slice(nc - 1 - t)
        qf, kf, vf = q_ref[slf, :], k_ref[slf, :], v_ref[slf, :]
        qb, kb, vb = q_ref[slb, :], k_ref[slb, :], v_ref[slb, :]
        s = lax.dot_general(qf, kf, nt, preferred_element_type=F32) * dmat
        cross_b = jnp.dot(scaled(qb, qdec_b), stb_ref[...].astype(BF16), preferred_element_type=F32)
        inner = jnp.dot(s.astype(BF16), vf, preferred_element_type=F32)
        cross_f = jnp.dot(scaled(qf, qdec_f), stf_ref[...].astype(BF16), preferred_element_type=F32)
        stb_ref[...] = stb_ref[...] * cdec_b + lax.dot_general(scaled(kb, kdec_b), vb, tn, preferred_element_type=F32)
        stf_ref[...] = stf_ref[...] * cdec_f + lax.dot_general(scaled(kf, kdec_f), vf, tn, preferred_element_type=F32)
        return slf, inner + cross_f, slb, cross_b

    def finish(sl, o):
        mu = jnp.mean(o, axis=-1, keepdims=True)
        oc = o - mu
        var = jnp.mean(oc * oc, axis=-1, keepdims=True)
        on = oc * lax.rsqrt(var + NORM_EPS) * gnw_ref[...]
        gate = g_ref[sl, :].astype(F32)
        o_ref[sl, :] = (gate * jax.nn.sigmoid(gate) * on).astype(o_ref.dtype)

    def first_half(t, carry):
        slf, of, slb, ob = step(t)
        acc_ref[slf, :] = of
        acc_ref[slb, :] = ob
        return carry

    def second_half(t, carry):
        slf, of, slb, ob = step(t)
        finish(slf, acc_ref[slf, :] + of)
        finish(slb, acc_ref[slb, :] + ob)
        return carry

    lax.fori_loop(0, nc // 2, first_half, 0)
    lax.fori_loop(nc // 2, nc, second_half, 0)


def _retention(proj, decay_logit, gn_w, *, batch, seq, d_model):
    heads = RET_HEADS
    dk = d_model // heads
    dv = 2 * d_model // heads
    n = batch * seq
    assert seq % (2 * RET_CHUNK) == 0, "the two scans are paired chunk by chunk"
    kern = functools.partial(_retention_kernel, seq=seq, chunk=RET_CHUNK)
    return pl.pallas_call(
        kern,
        grid=(batch, heads),
        in_specs=[
            pl.BlockSpec((2, heads), lambda b, h: (0, 0)),
            pl.BlockSpec((seq, dk), lambda b, h: (b, h)),
            pl.BlockSpec((seq, dk), lambda b, h: (b, heads + h)),
            pl.BlockSpec((seq, dv), lambda b, h: (b, heads + h)),
            pl.BlockSpec((seq, dv), lambda b, h: (b, 2 * heads + h)),
            pl.BlockSpec((1, dv), lambda b, h: (0, h)),
        ],
        out_specs=pl.BlockSpec((seq, dv), lambda b, h: (b, h)),
        out_shape=jax.ShapeDtypeStruct((n, 2 * d_model), BF16),
        scratch_shapes=[pltpu.VMEM((seq, dv), F32), pltpu.VMEM((dk, dv), F32), pltpu.VMEM((dk, dv), F32)],
        compiler_params=_cparams(("parallel", "parallel")),
        name="retention",
    )(decay_logit, proj, proj, proj, proj, gn_w.reshape(1, -1))


def _na_kernel(bias_ref, q_ref, k_ref, v_ref, o_ref, *, rows):
    gw = GRID_W
    kr = min(NA_MAX_ROWS, rows)
    dh = NA_HEAD_DIM
    lane_q = lax.broadcasted_iota(I32, (gw, 2 * dh), 1)
    nt = (((1,), (1,)), ((), ()))

    group = min(NA_ROW_UNROLL, rows)

    def body(it, carry):
        rr = [it * group + i for i in range(group)]
        rs = [jnp.clip(r - kr // 2, 0, rows - kr) for r in rr]
        s = []
        for r, r0 in zip(rr, rs):
            q2 = q_ref[pl.ds(pl.multiple_of(r * gw, gw), gw), :]
            zero = jnp.zeros_like(q2)
            qq = jnp.concatenate([jnp.where(lane_q < dh, q2, zero), jnp.where(lane_q >= dh, q2, zero)], axis=0)
            k2 = k_ref[pl.ds(pl.multiple_of(r0 * gw, gw), kr * gw), :]
            s.append(lax.dot_general(qq, k2, nt, preferred_element_type=F32) + bias_ref[0, r - r0])
        m = [jnp.max(si, axis=-1, keepdims=True) for si in s]
        p = [jnp.exp(si - mi) for si, mi in zip(s, m)]
        inv = [1.0 / jnp.sum(pi, axis=-1, keepdims=True) for pi in p]
        o = [jnp.dot(pi.astype(BF16), v_ref[pl.ds(pl.multiple_of(r0 * gw, gw), kr * gw), :],
                     preferred_element_type=F32) * ii for pi, ii, r0 in zip(p, inv, rs)]
        for r, oi in zip(rr, o):
            out = jnp.where(lane_q < dh, oi[0:gw, :], oi[gw:2 * gw, :])
            o_ref[pl.ds(pl.multiple_of(r * gw, gw), gw), :] = out.astype(o_ref.dtype)
        return carry

    lax.fori_loop(0, rows // group, body, 0)


def _na_bias_table(rpb, rows):
    kr = min(NA_MAX_ROWS, rows)
    c = np.arange(GRID_W)
    kc = np.arange(GRID_W)
    win_start = np.clip(c - NA_WIN_COLS // 2, 0, GRID_W - NA_WIN_COLS)
    valid = (kc[None, :] >= win_start[:, None]) & (kc[None, :] < win_start[:, None] + NA_WIN_COLS)
    dc_idx = np.clip(kc[None, :] - c[:, None] + NA_WIN_COLS - 1, 0, 2 * NA_WIN_COLS - 2)
    delta = np.arange(kr)
    a = np.arange(kr)
    dr_idx = a[None, :] - delta[:, None] + NA_MAX_ROWS - 1
    row_sel = (dr_idx[:, :, None] == np.arange(rpb.shape[1])).astype(np.float32)
    col_sel = (dc_idx[None, :, :] == np.arange(rpb.shape[2])[:, None, None]).astype(np.float32)
    t = jnp.einsum("dar,hrs,sck->hdcak", row_sel, rpb.astype(F32), col_sel, precision=lax.Precision.HIGHEST)
    t = jnp.where(jnp.asarray(valid)[None, None, :, None, :], t, NEG_BIG)
    h = rpb.shape[0]
    t = t.reshape(h // 2, 2, kr, GRID_W, kr * GRID_W).transpose(0, 2, 1, 3, 4)
    return t.reshape(h // 2, kr, 2 * GRID_W, kr * GRID_W)


def _na_attention(qkv, bias, *, batch, seq, d_model):
    n = batch * seq
    rows = seq // GRID_W
    kr = min(NA_MAX_ROWS, rows)
    pairs = NA_HEADS // 2
    pw = 2 * NA_HEAD_DIM
    assert rows % min(NA_ROW_UNROLL, rows) == 0
    kern = functools.partial(_na_kernel, rows=rows)
    return pl.pallas_call(
        kern,
        grid=(batch, pairs),
        in_specs=[
            pl.BlockSpec((1, kr, 2 * GRID_W, kr * GRID_W), lambda b, hp: (hp, 0, 0, 0)),
            pl.BlockSpec((seq, pw), lambda b, hp: (b, hp)),
            pl.BlockSpec((seq, pw), lambda b, hp: (b, pairs + hp)),
            pl.BlockSpec((seq, pw), lambda b, hp: (b, 2 * pairs + hp)),
        ],
        out_specs=pl.BlockSpec((seq, pw), lambda b, hp: (b, hp)),
        out_shape=jax.ShapeDtypeStruct((n, d_model), BF16),
        compiler_params=_cparams(("parallel", "parallel")),
        name="na_attention",
    )(bias, qkv, qkv, qkv)


def _out_router_kernel(a_ref, w_ref, h_ref, g_ref, rw_ref, h1_ref, xn_ref, aff_ref):
    y = jnp.dot(a_ref[...], w_ref[...], preferred_element_type=F32) + h_ref[...]
    h1_ref[...] = y
    xn = _rms(y, g_ref[...])
    xn_ref[...] = xn.astype(xn_ref.dtype)
    nt = (((1,), (1,)), ((), ()))
    ne = rw_ref.shape[0]
    xh = xn.astype(BF16)
    xl = (xn - xh.astype(F32)).astype(BF16)
    rw = rw_ref[...]
    rh = rw.astype(BF16)
    rl = (rw - rh.astype(F32)).astype(BF16)
    t1 = lax.dot_general(jnp.concatenate([rh, rl], axis=0), xh, nt, preferred_element_type=F32)
    t2 = lax.dot_general(rh, xl, nt, preferred_element_type=F32)
    logits = t1[0:ne, :] + (t1[ne:2 * ne, :] + t2)
    m = jnp.max(logits, axis=0, keepdims=True)
    e = jnp.exp(logits - m)
    aff = e / jnp.sum(e, axis=0, keepdims=True)
    for t in range(aff_ref.shape[0]):
        aff_ref[t] = aff[:, t * MOE_TILE:(t + 1) * MOE_TILE]


def _out_router(a, w_bf16, h, g, router_w, *, tm=1024):
    n, kdim = a.shape
    tm = min(tm, n)
    d = h.shape[1]
    ne = router_w.shape[1]
    tpb = tm // MOE_TILE
    return pl.pallas_call(
        _out_router_kernel,
        grid=(n // tm,),
        in_specs=[
            pl.BlockSpec((tm, kdim), lambda i: (i, 0)),
            pl.BlockSpec((kdim, d), lambda i: (0, 0)),
            pl.BlockSpec((tm, d), lambda i: (i, 0)),
            pl.BlockSpec((1, d), lambda i: (0, 0)),
            pl.BlockSpec((ne, d), lambda i: (0, 0)),
        ],
        out_specs=[
            pl.BlockSpec((tm, d), lambda i: (i, 0)),
            pl.BlockSpec((tm, d), lambda i: (i, 0)),
            pl.BlockSpec((tpb, ne, MOE_TILE), lambda i: (i, 0, 0)),
        ],
        out_shape=[
            jax.ShapeDtypeStruct((n, d), F32),
            jax.ShapeDtypeStruct((n, d), BF16),
            jax.ShapeDtypeStruct((n // MOE_TILE, ne, MOE_TILE), F32),
        ],
        compiler_params=_cparams(("parallel",)),
        name="out_router",
    )(a, w_bf16, h, g.reshape(1, d), router_w.T)


def _select_kernel(aff_ref, pos_ref, off_ref, *, cap):
    ntile, ne, tt = aff_ref.shape

    def count(pred_fn):
        def body(c, acc):
            x = pltpu.bitcast(aff_ref[c], I32)
            return acc + pred_fn(x).astype(I32)
        acc = lax.fori_loop(0, ntile, body, jnp.zeros((ne, tt), I32), unroll=4)
        return jnp.sum(acc, axis=1, keepdims=True)

    def bit_step(i, t):
        cand = t | jnp.left_shift(jnp.int32(1), 30 - i)
        return jnp.where(count(lambda x: x >= cand) >= cap, cand, t)

    thr = lax.fori_loop(0, 31, bit_step, jnp.zeros((ne, 1), I32))
    need_eq = (cap - count(lambda x: x > thr)).astype(F32)

    li = lax.broadcasted_iota(I32, (tt, tt), 0)
    lj = lax.broadcasted_iota(I32, (tt, tt), 1)
    upper = (li < lj).astype(BF16)

    def tile_body(c, carry):
        eq_carry, pos_carry = carry
        off_ref[c] = jnp.broadcast_to(pos_carry, (ne, LANES)).astype(I32)
        x = pltpu.bitcast(aff_ref[c], I32)
        gt = x > thr
        eq = x == thr
        gtf = gt.astype(F32)
        eqf = eq.astype(F32)
        pre = jnp.dot(jnp.concatenate([gtf, eqf], axis=0).astype(BF16), upper, preferred_element_type=F32)
        eq_rank = eq_carry + pre[ne:2 * ne, :]
        taken = eq & (eq_rank < need_eq)
        taken_before = jnp.minimum(eq_rank, need_eq) - jnp.minimum(eq_carry, need_eq)
        pos = pos_carry + pre[0:ne, :] + taken_before
        pos_ref[c] = jnp.where(gt | taken, pos.astype(I32), -1)
        eq_next = eq_carry + jnp.sum(eqf, axis=1, keepdims=True)
        taken_total = jnp.minimum(eq_next, need_eq) - jnp.minimum(eq_carry, need_eq)
        return eq_next, pos_carry + jnp.sum(gtf, axis=1, keepdims=True) + taken_total

    lax.fori_loop(0, ntile, tile_body, (jnp.zeros((ne, 1), F32), jnp.zeros((ne, 1), F32)), unroll=2)


def _select(aff3, *, cap):
    ntile, ne, tt = aff3.shape
    kern = functools.partial(_select_kernel, cap=cap)
    pos3, off3 = pl.pallas_call(
        kern,
        out_shape=[
            jax.ShapeDtypeStruct((ntile, ne, tt), I32),
            jax.ShapeDtypeStruct((ntile, ne, LANES), I32),
        ],
        compiler_params=pltpu.CompilerParams(vmem_limit_bytes=_VMEM_LIMIT),
        name="ec_select",
    )(aff3)
    offs = jnp.concatenate([off3[:, :, 0], jnp.full((1, ne), cap, I32)], axis=0).reshape(-1)
    return pos3, offs


def _tile_windows(off_ref, s, ne):
    starts, used = [], []
    for e in range(ne):
        off = off_ref[s * ne + e]
        nxt = off_ref[(s + 1) * ne + e]
        st = (off // BF16_ROWS) * BF16_ROWS
        starts.append(st)
        used.append(nxt - st)
    return starts, used


def _num_rounds(used):
    m = used[0]
    for u in used[1:]:
        m = jnp.maximum(m, u)
    return (m + MOE_SLOTS - 1) // MOE_SLOTS


def _dispatch_kernel(off_ref, x_ref, pos_ref, xe_ref, wins_ref, carry_ref, sem, nout_ref, *, cap):
    s = pl.program_id(0)
    nsub, ne, tt = pos_ref.shape
    gr = BF16_ROWS
    gpr = MOE_SLOTS // gr

    @pl.when(s == 0)
    def _():
        carry_ref[...] = jnp.zeros_like(carry_ref)
        nout_ref[0] = 0
        nout_ref[1] = 0

    slot = lax.broadcasted_iota(I32, (MOE_SLOTS, tt), 0)

    def drain(b):
        def wait_one(i, c):
            pltpu.make_async_copy(wins_ref.at[b, pl.ds(0, gr)], xe_ref.at[pl.ds(0, gr)], sem.at[b]).wait()
            return c
        lax.fori_loop(0, nout_ref[b], wait_one, 0)
        nout_ref[b] = 0

    for k in range(nsub):
        tile = s * nsub + k
        starts, used = _tile_windows(off_ref, tile, ne)
        pos = pos_ref[k]
        x = x_ref[pl.ds(k * tt, tt), :]

        def round_body(w, c, tile=tile, starts=starts, used=used, pos=pos, x=x):
            b = (tile + w) % 2
            drain(b)
            pieces = []
            for e in range(ne):
                key = pos[e:e + 1, :] - (starts[e] + w * MOE_SLOTS)
                pieces.append((jnp.broadcast_to(key, (MOE_SLOTS, tt)) == slot).astype(F32))
            onehot = jnp.concatenate(pieces, axis=0).astype(BF16)
            wins_ref[b] = jnp.dot(onehot, x, preferred_element_type=F32).astype(BF16)
            first = w == 0
            issued = 0
            for e in range(ne):
                head = pl.ds(e * MOE_SLOTS, gr)
                crows = pl.ds(e * gr, gr)
                carry = carry_ref[crows, :]
                merged = (wins_ref[b, head, :].astype(F32) + carry.astype(F32)).astype(BF16)
                wins_ref[b, head, :] = jnp.where(first, merged, wins_ref[b, head, :])

                ngc = used[e] // gr
                rem = used[e] % gr
                ng_w = jnp.clip(ngc - w * gpr, 0, gpr)

                def issue(j, c2, e=e):
                    src = wins_ref.at[b, pl.ds(e * MOE_SLOTS + j * gr, gr)]
                    dst = xe_ref.at[pl.ds(pl.multiple_of(e * cap + starts[e] + w * MOE_SLOTS + j * gr, gr), gr)]
                    pltpu.make_async_copy(src, dst, sem.at[b]).start()
                    return c2

                lax.fori_loop(0, ng_w, issue, 0)
                issued = issued + ng_w

                part = wins_ref[b, pl.ds(e * MOE_SLOTS + jnp.clip(ngc - w * gpr, 0, gpr - 1) * gr, gr), :]
                keep_part = (rem > 0) & (ngc // gpr == w)
                clear = (rem == 0) & first
                carry_ref[crows, :] = jnp.where(keep_part, part, jnp.where(clear, jnp.zeros_like(carry), carry))

            nout_ref[b] = issued
            return c

        lax.fori_loop(0, _num_rounds(used), round_body, 0)

    @pl.when(s == pl.num_programs(0) - 1)
    def _():
        drain(0)
        drain(1)


def _dispatch(offs, xn, pos3, *, cap):
    n, d = xn.shape
    ntile, ne, tt = pos3.shape
    nsub = min(DISPATCH_TILES_PER_STEP, ntile)
    assert ntile % nsub == 0
    kern = functools.partial(_dispatch_kernel, cap=cap)
    return pl.pallas_call(
        kern,
        grid_spec=pltpu.PrefetchScalarGridSpec(
            num_scalar_prefetch=1,
            grid=(ntile // nsub,),
            in_specs=[
                pl.BlockSpec((nsub * tt, d), lambda i, off: (i, 0)),
                pl.BlockSpec((nsub, ne, tt), lambda i, off: (i, 0, 0)),
            ],
            out_specs=pl.BlockSpec(memory_space=pl.ANY),
            scratch_shapes=[
                pltpu.VMEM((2, ne * MOE_SLOTS, d), BF16),
                pltpu.VMEM((ne * BF16_ROWS, d), BF16),
                pltpu.SemaphoreType.DMA((2,)),
                pltpu.SMEM((2,), I32),
            ],
        ),
        out_shape=jax.ShapeDtypeStruct((ne * cap, d), BF16),
        compiler_params=_cparams(("arbitrary",)),
        name="ec_dispatch",
    )(offs, xn, pos3)


def _ffn_kernel(x_ref, wg_ref, wu_ref, wd_ref, o_ref):
    x = x_ref[...]
    a = jnp.dot(x, wg_ref[0, 0], preferred_element_type=F32)
    u = jnp.dot(x, wu_ref[0, 0], preferred_element_type=F32)
    hid = (a * jax.nn.sigmoid(a) * u).astype(BF16)
    o_ref[...] = jnp.dot(hid, wd_ref[0, 0], preferred_element_type=F32).astype(o_ref.dtype)


def _expert_ffn(xe, wg, wu, wd, *, layer, cap, tr=1024):
    _, ne, d, f = wg.shape
    tr = min(tr, cap)
    nt = cap // tr
    return pl.pallas_call(
        _ffn_kernel,
        grid=(ne, nt),
        in_specs=[
            pl.BlockSpec((tr, d), lambda e, t: (e * nt + t, 0)),
            pl.BlockSpec((1, 1, d, f), lambda e, t: (layer, e, 0, 0)),
            pl.BlockSpec((1, 1, d, f), lambda e, t: (layer, e, 0, 0)),
            pl.BlockSpec((1, 1, f, d), lambda e, t: (layer, e, 0, 0)),
        ],
        out_specs=pl.BlockSpec((tr, d), lambda e, t: (e * nt + t, 0)),
        out_shape=jax.ShapeDtypeStruct((ne * cap, d), BF16),
        compiler_params=_cparams(("parallel", "parallel")),
        name="expert_ffn",
    )(xe, wg, wu, wd)


def _combine_kernel(off_ref, h_ref, aff_ref, pos_ref, p_ref, ye_ref, png_ref, wpg_ref, wpu_ref, fng_ref,
                    o_ref, wins_ref, acc_ref, sem, *, cap, final_norm):
    s = pl.program_id(0)
    ntile = pl.num_programs(0)
    nsub, ne, tt = pos_ref.shape
    spill_buf = 2 * nsub

    def window_copies(starts_t, w, buf):
        copies, wstarts = [], []
        for e in range(ne):
            ws = pl.multiple_of(jnp.minimum(starts_t[e] + w * MOE_SLOTS, cap - MOE_SLOTS), BF16_ROWS)
            copies.append(pltpu.make_async_copy(
                ye_ref.at[pl.ds(pl.multiple_of(e * cap + ws, BF16_ROWS), MOE_SLOTS)],
                wins_ref.at[buf, pl.ds(e * MOE_SLOTS, MOE_SLOTS)], sem.at[buf]))
            wstarts.append(ws)
        return copies, wstarts

    cur = s % 2
    tiles = [_tile_windows(off_ref, s * nsub + k, ne) for k in range(nsub)]

    @pl.when(s == 0)
    def _():
        for k in range(nsub):
            for cp in window_copies(tiles[k][0], 0, cur * nsub + k)[0]:
                cp.start()

    @pl.when(s + 1 < ntile)
    def _():
        for k in range(nsub):
            nxt_starts, _ = _tile_windows(off_ref, (s + 1) * nsub + k, ne)
            for cp in window_copies(nxt_starts, 0, (1 - cur) * nsub + k)[0]:
                cp.start()

    slot = lax.broadcasted_iota(I32, (MOE_SLOTS, tt), 0)
    tn = (((0,), (0,)), ((), ()))

    def gates(k, w, wstarts):
        starts = tiles[k][0]
        pos = pos_ref[k]
        aff = aff_ref[k]
        pieces = []
        for e in range(ne):
            base = starts[e] + w * MOE_SLOTS
            pe = pos[e:e + 1, :]
            in_round = (pe >= base) & (pe < base + MOE_SLOTS)
            hit = (jnp.broadcast_to(pe - wstarts[e], (MOE_SLOTS, tt)) == slot) & jnp.broadcast_to(in_round, (MOE_SLOTS, tt))
            pieces.append(jnp.where(hit, jnp.broadcast_to(aff[e:e + 1, :], (MOE_SLOTS, tt)), 0.0))
        return jnp.concatenate(pieces, axis=0).astype(BF16)

    for k in range(nsub):
        starts, used = tiles[k]
        rows = pl.ds(k * tt, tt)
        buf0 = cur * nsub + k
        copies0, wstarts0 = window_copies(starts, 0, buf0)
        a0 = gates(k, 0, wstarts0)
        for cp in copies0:
            cp.wait()
        acc_ref[rows, :] = lax.dot_general(a0, wins_ref[buf0], tn, preferred_element_type=F32)

        def round_body(w, c, k=k, starts=starts, rows=rows):
            copies, wstarts = window_copies(starts, w, spill_buf)
            for cp in copies:
                cp.start()
            a_w = gates(k, w, wstarts)
            for cp in copies:
                cp.wait()
            acc_ref[rows, :] += lax.dot_general(a_w, wins_ref[spill_buf], tn, preferred_element_type=F32)
            return c

        lax.fori_loop(1, _num_rounds(used), round_body, 0)
    h2 = h_ref[...] + acc_ref[...]
    gate = jax.nn.sigmoid(jnp.dot(_rms(h2, png_ref[...]).astype(BF16), wpg_ref[...], preferred_element_type=F32))
    up = jnp.dot(p_ref[0].astype(BF16), wpu_ref[...], preferred_element_type=F32)
    h3 = h2 + up * gate
    if final_norm:
        h3 = _rms(h3, fng_ref[...])
    o_ref[...] = h3


def _combine(offs, h1, aff3, pos3, p_all, ye, ple_norm, wpg, wpu, final_g, *, layer, cap, final_norm):
    n, d = h1.shape
    ntile, ne, tt = pos3.shape
    pd = p_all.shape[2]
    nsub = min(COMBINE_TILES_PER_STEP, ntile)
    assert ntile % nsub == 0
    rows = nsub * tt
    nbuf = 2 * nsub + 1
    kern = functools.partial(_combine_kernel, cap=cap, final_norm=final_norm)
    return pl.pallas_call(
        kern,
        grid_spec=pltpu.PrefetchScalarGridSpec(
            num_scalar_prefetch=1,
            grid=(ntile // nsub,),
            in_specs=[
                pl.BlockSpec((rows, d), lambda i, off: (i, 0)),
                pl.BlockSpec((nsub, ne, tt), lambda i, off: (i, 0, 0)),
                pl.BlockSpec((nsub, ne, tt), lambda i, off: (i, 0, 0)),
                pl.BlockSpec((1, rows, pd), lambda i, off: (layer, i, 0)),
                pl.BlockSpec(memory_space=pl.ANY),
                pl.BlockSpec((1, d), lambda i, off: (0, 0)),
                pl.BlockSpec((d, d), lambda i, off: (0, 0)),
                pl.BlockSpec((pd, d), lambda i, off: (0, 0)),
                pl.BlockSpec((1, d), lambda i, off: (0, 0)),
            ],
            out_specs=pl.BlockSpec((rows, d), lambda i, off: (i, 0)),
            scratch_shapes=[pltpu.VMEM((nbuf, ne * MOE_SLOTS, d), BF16), pltpu.VMEM((rows, d), F32),
                            pltpu.SemaphoreType.DMA((nbuf,))],
        ),
        out_shape=jax.ShapeDtypeStruct((n, d), F32),
        compiler_params=_cparams(("arbitrary",)),
        name="combine_ple",
    )(offs, h1, aff3, pos3, p_all, ye, ple_norm.reshape(1, d), wpg, wpu, final_g.reshape(1, d))


def _moe_block(h1, xn, aff3, p_all, w, i, *, final_norm):
    n, d = h1.shape
    cap = EC_CAPACITY * n // N_EXPERTS
    pos3, offs = _select(aff3, cap=cap)
    xe = _dispatch(offs, xn, pos3, cap=cap)
    ye = _expert_ffn(xe, w["exp_w_gate"], w["exp_w_up"], w["exp_w_down"], layer=i, cap=cap)
    return _combine(offs, h1, aff3, pos3, p_all, ye, w["ple_norm"][i], w["ple_w_gate"][i], w["ple_w_up"][i],
                    w["final_norm"], layer=i, cap=cap, final_norm=final_norm)


def _rope_tables(seq, dim):
    inv = ROPE_THETA ** (-jnp.arange(0, dim, 2, dtype=F32) / dim)
    ang = jnp.arange(seq, dtype=F32)[:, None] * inv[None, :]
    return jnp.cos(ang), jnp.sin(ang)


def _trunk(x, p, w):
    batch, seq, d = x.shape
    n = batch * seq
    h = x.reshape(n, d)
    dk = d // RET_HEADS
    cos, sin = _rope_tables(seq, dk)

    proj = _in_proj(h, w["mix_norm"][0], w["ret_w_in"][0], cos, sin, seq=seq, rope_units=2,
                    unit_scales=(1.0, float(dk) ** -0.5, 1.0, 1.0, 1.0, 1.0))
    a = _retention(proj, w["ret_decay_logit"][0], w["ret_gn_w"][0], batch=batch, seq=seq, d_model=d)
    h1, xn, aff3 = _out_router(a, w["ret_w_out"][0], h, w["ffn_norm"][0], w["router_w"][0])
    p_all = p.reshape(p.shape[0], n, p.shape[-1])
    h = _moe_block(h1, xn, aff3, p_all, w, 0, final_norm=False)

    qkv = _in_proj(h, w["mix_norm"][1], w["na_w_in"][0], cos, sin, seq=seq, rope_units=0,
                   unit_scales=(float(NA_HEAD_DIM) ** -0.5, 1.0, 1.0))
    a = _na_attention(qkv, _na_bias_table(w["na_rpb"][0], seq // GRID_W), batch=batch, seq=seq, d_model=d)
    h1, xn, aff3 = _out_router(a, w["na_w_out"][0], h, w["ffn_norm"][1], w["router_w"][1])
    y = _moe_block(h1, xn, aff3, p_all, w, 1, final_norm=True)
    return y.reshape(batch, seq, d)


def kernel(x_prompt, x_sample, p_prompt, p_sample, ret_w_in, ret_decay_logit, ret_gn_w, ret_w_out, na_w_in, na_rpb, na_w_out, mix_norm, ffn_norm, ple_norm, router_w, exp_w_gate, exp_w_up, exp_w_down, ple_w_up, ple_w_gate, final_norm):
    w = dict(
        ret_w_in=ret_w_in.astype(BF16), ret_decay_logit=ret_decay_logit, ret_gn_w=ret_gn_w,
        ret_w_out=ret_w_out.astype(BF16), na_w_in=na_w_in.astype(BF16), na_rpb=na_rpb,
        na_w_out=na_w_out.astype(BF16), mix_norm=mix_norm, ffn_norm=ffn_norm, ple_norm=ple_norm,
        router_w=router_w, exp_w_gate=exp_w_gate.astype(BF16), exp_w_up=exp_w_up.astype(BF16),
        exp_w_down=exp_w_down.astype(BF16), ple_w_up=ple_w_up.astype(BF16), ple_w_gate=ple_w_gate.astype(BF16),
        final_norm=final_norm,
    )
    return _trunk(x_prompt, p_prompt, w), _trunk(x_sample, p_sample, w)
```

```python
import functools

import numpy as np
import jax
import jax.numpy as jnp
from jax import lax
from jax.experimental import pallas as pl
from jax.experimental.pallas import tpu as pltpu

F32 = jnp.float32
BF16 = jnp.bfloat16
I32 = jnp.int32

NORM_EPS = 1e-6
ROPE_THETA = 10000.0
GRID_W = 64
RET_HEADS = 4
RET_CHUNK = 256
NA_HEADS = 16
NA_HEAD_DIM = 64
NA_MAX_ROWS = 8
NA_WIN_COLS = 16
NA_ROW_UNROLL = 16
NA_PAIRS_PER_STEP = 2
N_EXPERTS = 16
EC_CAPACITY = 2
LANES = 128
BF16_ROWS = 16
MOE_TILE = 256
MOE_SLOTS = 64
DISPATCH_TILES_PER_STEP = 4
COMBINE_TILES_PER_STEP = 2
NEG_BIG = -1e30

_VMEM_LIMIT = 56 * 1024 * 1024


def _cparams(sem):
    return pltpu.CompilerParams(dimension_semantics=sem, vmem_limit_bytes=_VMEM_LIMIT)


def _rms(x, g):
    return x * lax.rsqrt(jnp.mean(x * x, axis=-1, keepdims=True) + NORM_EPS) * g


def _in_proj_kernel(x_ref, g_ref, w_ref, cos_ref, sin_ref, o_ref, xn_ref, *, nj, unit_w, rope_units, unit_scales,
                    rope_dim):
    j = pl.program_id(1)

    @pl.when(j == 0)
    def _():
        xn_ref[...] = _rms(x_ref[...], g_ref[...]).astype(BF16)

    upb = w_ref.shape[1] // unit_w
    sw = o_ref.shape[2]
    spu = unit_w // sw
    half = rope_dim // 2

    def rotate(y, c, cos, sin):
        assert sw == rope_dim
        for hh in range(spu):
            lo = hh * rope_dim
            x1 = y[:, lo:lo + half]
            x2 = y[:, lo + half:lo + rope_dim]
            o_ref[c * spu + hh, :, 0:half] = (x1 * cos - x2 * sin).astype(o_ref.dtype)
            o_ref[c * spu + hh, :, half:rope_dim] = (x1 * sin + x2 * cos).astype(o_ref.dtype)

    for c in range(upb):
        base = c * unit_w
        y = jnp.dot(xn_ref[...], w_ref[:, base:base + unit_w], preferred_element_type=F32)
        kinds = [(jj * upb + c < rope_units, unit_scales[jj * upb + c]) for jj in range(nj)]
        if all(k == kinds[0] for k in kinds):
            is_rope, scale = kinds[0]
            if is_rope:
                rotate(y, c, cos_ref[...] * scale, sin_ref[...] * scale)
            else:
                ys = (y if scale == 1.0 else y * scale).astype(o_ref.dtype)
                for sl in range(spu):
                    o_ref[c * spu + sl] = ys[:, sl * sw:(sl + 1) * sw]
        else:
            rope_here = jnp.bool_(False)
            scale = jnp.float32(1.0)
            for jj, (is_rope, sc) in enumerate(kinds):
                rope_here = jnp.where(j == jj, is_rope, rope_here)
                scale = jnp.where(j == jj, jnp.float32(sc), scale)
            rotate(y, c, jnp.where(rope_here, cos_ref[...], 1.0) * scale, jnp.where(rope_here, sin_ref[...], 0.0) * scale)


def _in_proj(x, g, w_bf16, cos, sin, *, seq, rope_units, unit_scales, slab_w, tm=1024, units_per_step=3):
    n, d = x.shape
    tm = min(tm, seq)
    ncol = w_bf16.shape[1]
    tn = units_per_step * d
    nj = ncol // tn
    assert len(unit_scales) == ncol // d
    rope_dim = 2 * cos.shape[1]
    nseq = seq // tm
    kern = functools.partial(_in_proj_kernel, nj=nj, unit_w=d, rope_units=rope_units, unit_scales=unit_scales,
                             rope_dim=rope_dim)
    return pl.pallas_call(
        kern,
        grid=(n // tm, ncol // tn),
        in_specs=[
            pl.BlockSpec((tm, d), lambda i, j: (i, 0)),
            pl.BlockSpec((1, d), lambda i, j: (0, 0)),
            pl.BlockSpec((d, tn), lambda i, j: (0, j)),
            pl.BlockSpec((tm, cos.shape[1]), lambda i, j: (i % nseq, 0)),
            pl.BlockSpec((tm, cos.shape[1]), lambda i, j: (i % nseq, 0)),
        ],
        out_specs=pl.BlockSpec((tn // slab_w, tm, slab_w), lambda i, j: (j, i, 0)),
        out_shape=jax.ShapeDtypeStruct((ncol // slab_w, n, slab_w), BF16),
        scratch_shapes=[pltpu.VMEM((tm, d), BF16)],
        compiler_params=_cparams(("parallel", "arbitrary")),
        name="in_proj",
    )(x, g.reshape(1, d), w_bf16, cos, sin)


def _retention_kernel(dl_ref, q_ref, k_ref, v_ref, g_ref, gnw_ref, o_ref, acc_ref, stf_ref, stb_ref, *, seq, chunk):
    h = pl.program_id(1)
    nc = seq // chunk
    dl = dl_ref[...]
    lg = jnp.minimum(dl, 0.0) - jnp.log1p(jnp.exp(-jnp.abs(dl)))
    col = lax.broadcasted_iota(I32, dl.shape, 1)
    lgh = jnp.sum(jnp.where(col == h, lg, 0.0), axis=1, keepdims=True)
    lgf = lgh[0:1, :]
    lgb = lgh[1:2, :]

    ri = lax.broadcasted_iota(I32, (chunk, chunk), 0)
    ci = lax.broadcasted_iota(I32, (chunk, chunk), 1)
    diff = (ri - ci).astype(F32)
    dmat = jnp.exp(jnp.where(diff >= 0, diff * lgf, -diff * lgb))
    pos = lax.broadcasted_iota(I32, (chunk, 1), 0).astype(F32)
    qdec_f = jnp.exp((pos + 1.0) * lgf)
    kdec_f = jnp.exp((chunk - 1.0 - pos) * lgf)
    qdec_b = jnp.exp((chunk - pos) * lgb)
    kdec_b = jnp.exp(pos * lgb)
    cdec_f = jnp.exp(chunk * lgf)
    cdec_b = jnp.exp(chunk * lgb)
    nt = (((1,), (1,)), ((), ()))
    tn = (((0,), (0,)), ((), ()))

    stf_ref[...] = jnp.zeros_like(stf_ref)
    stb_ref[...] = jnp.zeros_like(stb_ref)

    def chunk_slice(c):
        return pl.ds(pl.multiple_of(c * chunk, chunk), chunk)

    def scaled(x, dec):
        return (x.astype(F32) * dec).astype(BF16)

    def wide(ref, sl):
        return jnp.concatenate([ref[i, sl, :] for i in range(ref.shape[0])], axis=1)

    def step(t):
        slf = chunk_slice(t)
        slb = chunk_slice(nc - 1 - t)
        qf, kf, vf = q_ref[0, slf, :], k_ref[0, slf, :], wide(v_ref, slf)
        qb, kb, vb = q_ref[0, slb, :], k_ref[0, slb, :], wide(v_ref, slb)
        s = lax.dot_general(qf, kf, nt, preferred_element_type=F32) * dmat
        cross_b = jnp.dot(scaled(qb, qdec_b), stb_ref[...].astype(BF16), preferred_element_type=F32)
        inner = jnp.dot(s.astype(BF16), vf, preferred_element_type=F32)
        cross_f = jnp.dot(scaled(qf, qdec_f), stf_ref[...].astype(BF16), preferred_element_type=F32)
        stb_ref[...] = stb_ref[...] * cdec_b + lax.dot_general(scaled(kb, kdec_b), vb, tn, preferred_element_type=F32)
        stf_ref[...] = stf_ref[...] * cdec_f + lax.dot_general(scaled(kf, kdec_f), vf, tn, preferred_element_type=F32)
        return slf, inner + cross_f, slb, cross_b

    def finish(sl, o):
        mu = jnp.mean(o, axis=-1, keepdims=True)
        oc = o - mu
        var = jnp.mean(oc * oc, axis=-1, keepdims=True)
        on = oc * lax.rsqrt(var + NORM_EPS) * gnw_ref[...]
        gate = wide(g_ref, sl).astype(F32)
        o_ref[0, sl, :] = (gate * jax.nn.sigmoid(gate) * on).astype(o_ref.dtype)

    def first_half(t, carry):
        slf, of, slb, ob = step(t)
        acc_ref[slf, :] = of
        acc_ref[slb, :] = ob
        return carry

    def second_half(t, carry):
        slf, of, slb, ob = step(t)
        finish(slf, acc_ref[slf, :] + of)
        finish(slb, acc_ref[slb, :] + ob)
        return carry

    lax.fori_loop(0, nc // 2, first_half, 0)
    lax.fori_loop(nc // 2, nc, second_half, 0)


def _retention(proj, decay_logit, gn_w, *, batch, seq, d_model):
    heads = RET_HEADS
    dk = d_model // heads
    dv = 2 * d_model // heads
    n = batch * seq
    assert seq % (2 * RET_CHUNK) == 0, "the two scans are paired chunk by chunk"
    kern = functools.partial(_retention_kernel, seq=seq, chunk=RET_CHUNK)
    return pl.pallas_call(
        kern,
        grid=(batch, heads),
        in_specs=[
            pl.BlockSpec((2, heads), lambda b, h: (0, 0)),
            pl.BlockSpec((1, seq, dk), lambda b, h: (h, b, 0)),
            pl.BlockSpec((1, seq, dk), lambda b, h: (heads + h, b, 0)),
            pl.BlockSpec((dv // dk, seq, dk), lambda b, h: (heads + h, b, 0)),
            pl.BlockSpec((dv // dk, seq, dk), lambda b, h: (2 * heads + h, b, 0)),
            pl.BlockSpec((1, dv), lambda b, h: (0, h)),
        ],
        out_specs=pl.BlockSpec((1, seq, dv), lambda b, h: (h, b, 0)),
        out_shape=jax.ShapeDtypeStruct((heads, n, dv), BF16),
        scratch_shapes=[pltpu.VMEM((seq, dv), F32), pltpu.VMEM((dk, dv), F32), pltpu.VMEM((dk, dv), F32)],
        compiler_params=_cparams(("parallel", "parallel")),
        name="retention",
    )(decay_logit, proj, proj, proj, proj, gn_w.reshape(1, -1))


def _na_kernel(bias_ref, q_ref, k_ref, v_ref, o_ref, *, rows):
    gw = GRID_W
    kr = min(NA_MAX_ROWS, rows)
    dh = NA_HEAD_DIM
    lane_q = lax.broadcasted_iota(I32, (gw, 2 * dh), 1)
    nt = (((1,), (1,)), ((), ()))

    npair = q_ref.shape[0]
    group = min(NA_ROW_UNROLL // npair, rows)

    def body(it, carry):
        chains = [(it * group + i, pp) for i in range(group) for pp in range(npair)]
        rs = [jnp.clip(r - kr // 2, 0, rows - kr) for r, _ in chains]
        s = []
        for (r, pp), r0 in zip(chains, rs):
            q2 = q_ref[pp, pl.ds(pl.multiple_of(r * gw, gw), gw), :]
            zero = jnp.zeros_like(q2)
            qq = jnp.concatenate([jnp.where(lane_q < dh, q2, zero), jnp.where(lane_q >= dh, q2, zero)], axis=0)
            k2 = k_ref[pp, pl.ds(pl.multiple_of(r0 * gw, gw), kr * gw), :]
            s.append(lax.dot_general(qq, k2, nt, preferred_element_type=F32) + bias_ref[pp, r - r0])
        m = [jnp.max(si, axis=-1, keepdims=True) for si in s]
        p = [jnp.exp(si - mi) for si, mi in zip(s, m)]
        inv = [1.0 / jnp.sum(pi, axis=-1, keepdims=True) for pi in p]
        o = [jnp.dot(pi.astype(BF16), v_ref[pp, pl.ds(pl.multiple_of(r0 * gw, gw), kr * gw), :],
                     preferred_element_type=F32) * ii for pi, ii, (_, pp), r0 in zip(p, inv, chains, rs)]
        for (r, pp), oi in zip(chains, o):
            out = jnp.where(lane_q < dh, oi[0:gw, :], oi[gw:2 * gw, :])
            o_ref[pp, pl.ds(pl.multiple_of(r * gw, gw), gw), :] = out.astype(o_ref.dtype)
        return carry

    lax.fori_loop(0, rows // group, body, 0)


def _na_bias_table(rpb, rows):
    kr = min(NA_MAX_ROWS, rows)
    c = np.arange(GRID_W)
    kc = np.arange(GRID_W)
    win_start = np.clip(c - NA_WIN_COLS // 2, 0, GRID_W - NA_WIN_COLS)
    valid = (kc[None, :] >= win_start[:, None]) & (kc[None, :] < win_start[:, None] + NA_WIN_COLS)
    dc_idx = np.clip(kc[None, :] - c[:, None] + NA_WIN_COLS - 1, 0, 2 * NA_WIN_COLS - 2)
    delta = np.arange(kr)
    a = np.arange(kr)
    dr_idx = a[None, :] - delta[:, None] + NA_MAX_ROWS - 1
    row_sel = (dr_idx[:, :, None] == np.arange(rpb.shape[1])).astype(np.float32)
    col_sel = (dc_idx[None, :, :] == np.arange(rpb.shape[2])[:, None, None]).astype(np.float32)
    t = jnp.einsum("dar,hrs,sck->hdcak", row_sel, rpb.astype(F32), col_sel, precision=lax.Precision.HIGHEST)
    t = jnp.where(jnp.asarray(valid)[None, None, :, None, :], t, NEG_BIG)
    h = rpb.shape[0]
    t = t.reshape(h // 2, 2, kr, GRID_W, kr * GRID_W).transpose(0, 2, 1, 3, 4)
    return t.reshape(h // 2, kr, 2 * GRID_W, kr * GRID_W)


def _na_attention(qkv, bias, *, batch, seq, d_model):
    n = batch * seq
    rows = seq // GRID_W
    kr = min(NA_MAX_ROWS, rows)
    pairs = NA_HEADS // 2
    pw = 2 * NA_HEAD_DIM
    pps = NA_PAIRS_PER_STEP
    steps = pairs // pps
    assert rows % min(NA_ROW_UNROLL // pps, rows) == 0 and pairs % pps == 0
    kern = functools.partial(_na_kernel, rows=rows)
    return pl.pallas_call(
        kern,
        grid=(batch, steps),
        in_specs=[
            pl.BlockSpec((pps, kr, 2 * GRID_W, kr * GRID_W), lambda b, hp: (hp, 0, 0, 0)),
            pl.BlockSpec((pps, seq, pw), lambda b, hp: (hp, b, 0)),
            pl.BlockSpec((pps, seq, pw), lambda b, hp: (steps + hp, b, 0)),
            pl.BlockSpec((pps, seq, pw), lambda b, hp: (2 * steps + hp, b, 0)),
        ],
        out_specs=pl.BlockSpec((pps, seq, pw), lambda b, hp: (hp, b, 0)),
        out_shape=jax.ShapeDtypeStruct((pairs, n, pw), BF16),
        compiler_params=_cparams(("parallel", "parallel")),
        name="na_attention",
    )(bias, qkv, qkv, qkv)


def _out_router_kernel(a_ref, w_ref, h_ref, g_ref, rw_ref, h1_ref, xn_ref, aff_ref):
    a = jnp.concatenate([a_ref[i] for i in range(a_ref.shape[0])], axis=1)
    y = jnp.dot(a, w_ref[...], preferred_element_type=F32) + h_ref[...]
    h1_ref[...] = y
    xn = _rms(y, g_ref[...])
    xn_ref[...] = xn.astype(xn_ref.dtype)
    nt = (((1,), (1,)), ((), ()))
    ne = rw_ref.shape[0]
    xh = xn.astype(BF16)
    xl = (xn - xh.astype(F32)).astype(BF16)
    rw = rw_ref[...]
    rh = rw.astype(BF16)
    rl = (rw - rh.astype(F32)).astype(BF16)
    t1 = lax.dot_general(jnp.concatenate([rh, rl], axis=0), xh, nt, preferred_element_type=F32)
    t2 = lax.dot_general(rh, xl, nt, preferred_element_type=F32)
    logits = t1[0:ne, :] + (t1[ne:2 * ne, :] + t2)
    m = jnp.max(logits, axis=0, keepdims=True)
    e = jnp.exp(logits - m)
    aff = e / jnp.sum(e, axis=0, keepdims=True)
    for t in range(aff_ref.shape[0]):
        aff_ref[t] = aff[:, t * MOE_TILE:(t + 1) * MOE_TILE]


def _out_router(a, w_bf16, h, g, router_w, *, tm=1024):
    nslab, n, sw = a.shape
    kdim = nslab * sw
    tm = min(tm, n)
    d = h.shape[1]
    ne = router_w.shape[1]
    tpb = tm // MOE_TILE
    return pl.pallas_call(
        _out_router_kernel,
        grid=(n // tm,),
        in_specs=[
            pl.BlockSpec((nslab, tm, sw), lambda i: (0, i, 0)),
            pl.BlockSpec((kdim, d), lambda i: (0, 0)),
            pl.BlockSpec((tm, d), lambda i: (i, 0)),
            pl.BlockSpec((1, d), lambda i: (0, 0)),
            pl.BlockSpec((ne, d), lambda i: (0, 0)),
        ],
        out_specs=[
            pl.BlockSpec((tm, d), lambda i: (i, 0)),
            pl.BlockSpec((tm, d), lambda i: (i, 0)),
            pl.BlockSpec((tpb, ne, MOE_TILE), lambda i: (i, 0, 0)),
        ],
        out_shape=[
            jax.ShapeDtypeStruct((n, d), F32),
            jax.ShapeDtypeStruct((n, d), BF16),
            jax.ShapeDtypeStruct((n // MOE_TILE, ne, MOE_TILE), F32),
        ],
        compiler_params=_cparams(("parallel",)),
        name="out_router",
    )(a, w_bf16, h, g.reshape(1, d), router_w.T)


def _select_kernel(aff_ref, pos_ref, off_ref, *, cap):
    ntile, ne, tt = aff_ref.shape

    def count(pred_fn):
        def body(c, acc):
            x = pltpu.bitcast(aff_ref[c], I32)
            return acc + pred_fn(x).astype(I32)
        acc = lax.fori_loop(0, ntile, body, jnp.zeros((ne, tt), I32), unroll=4)
        return jnp.sum(acc, axis=1, keepdims=True)

    def bit_step(i, t):
        cand = t | jnp.left_shift(jnp.int32(1), 30 - i)
        return jnp.where(count(lambda x: x >= cand) >= cap, cand, t)

    thr = lax.fori_loop(0, 31, bit_step, jnp.zeros((ne, 1), I32))
    need_eq = (cap - count(lambda x: x > thr)).astype(F32)

    li = lax.broadcasted_iota(I32, (tt, tt), 0)
    lj = lax.broadcasted_iota(I32, (tt, tt), 1)
    upper = (li < lj).astype(BF16)

    def tile_body(c, carry):
        eq_carry, pos_carry = carry
        off_ref[c] = jnp.broadcast_to(pos_carry, (ne, LANES)).astype(I32)
        x = pltpu.bitcast(aff_ref[c], I32)
        gt = x > thr
        eq = x == thr
        gtf = gt.astype(F32)
        eqf = eq.astype(F32)
        pre = jnp.dot(jnp.concatenate([gtf, eqf], axis=0).astype(BF16), upper, preferred_element_type=F32)
        eq_rank = eq_carry + pre[ne:2 * ne, :]
        taken = eq & (eq_rank < need_eq)
        taken_before = jnp.minimum(eq_rank, need_eq) - jnp.minimum(eq_carry, need_eq)
        pos = pos_carry + pre[0:ne, :] + taken_before
        pos_ref[c] = jnp.where(gt | taken, pos.astype(I32), -1)
        eq_next = eq_carry + jnp.sum(eqf, axis=1, keepdims=True)
        taken_total = jnp.minimum(eq_next, need_eq) - jnp.minimum(eq_carry, need_eq)
        return eq_next, pos_carry + jnp.sum(gtf, axis=1, keepdims=True) + taken_total

    lax.fori_loop(0, ntile, tile_body, (jnp.zeros((ne, 1), F32), jnp.zeros((ne, 1), F32)), unroll=2)


def _select(aff3, *, cap):
    ntile, ne, tt = aff3.shape
    kern = functools.partial(_select_kernel, cap=cap)
    pos3, off3 = pl.pallas_call(
        kern,
        out_shape=[
            jax.ShapeDtypeStruct((ntile, ne, tt), I32),
            jax.ShapeDtypeStruct((ntile, ne, LANES), I32),
        ],
        compiler_params=pltpu.CompilerParams(vmem_limit_bytes=_VMEM_LIMIT),
        name="ec_select",
    )(aff3)
    offs = jnp.concatenate([off3[:, :, 0], jnp.full((1, ne), cap, I32)], axis=0).reshape(-1)
    return pos3, offs


def _tile_windows(off_ref, s, ne):
    starts, used = [], []
    for e in range(ne):
        off = off_ref[s * ne + e]
        nxt = off_ref[(s + 1) * ne + e]
        st = (off // BF16_ROWS) * BF16_ROWS
        starts.append(st)
        used.append(nxt - st)
    return starts, used


def _num_rounds(used):
    m = used[0]
    for u in used[1:]:
        m = jnp.maximum(m, u)
    return (m + MOE_SLOTS - 1) // MOE_SLOTS


def _dispatch_kernel(off_ref, x_ref, pos_ref, xe_ref, wins_ref, carry_ref, sem, nout_ref, *, cap):
    s = pl.program_id(0)
    nsub, ne, tt = pos_ref.shape
    gr = BF16_ROWS
    gpr = MOE_SLOTS // gr

    @pl.when(s == 0)
    def _():
        carry_ref[...] = jnp.zeros_like(carry_ref)
        nout_ref[0] = 0
        nout_ref[1] = 0

    slot = lax.broadcasted_iota(I32, (MOE_SLOTS, tt), 0)

    def drain(b):
        def wait_one(i, c):
            pltpu.make_async_copy(wins_ref.at[b, pl.ds(0, gr)], xe_ref.at[pl.ds(0, gr)], sem.at[b]).wait()
            return c
        lax.fori_loop(0, nout_ref[b], wait_one, 0)
        nout_ref[b] = 0

    for k in range(nsub):
        tile = s * nsub + k
        starts, used = _tile_windows(off_ref, tile, ne)
        pos = pos_ref[k]
        x = x_ref[pl.ds(k * tt, tt), :]

        def round_body(w, c, tile=tile, starts=starts, used=used, pos=pos, x=x):
            b = (tile + w) % 2
            drain(b)
            pieces = []
            for e in range(ne):
                key = pos[e:e + 1, :] - (starts[e] + w * MOE_SLOTS)
                pieces.append((jnp.broadcast_to(key, (MOE_SLOTS, tt)) == slot).astype(F32))
            onehot = jnp.concatenate(pieces, axis=0).astype(BF16)
            wins_ref[b] = jnp.dot(onehot, x, preferred_element_type=F32).astype(BF16)
            first = w == 0
            issued = 0
            for e in range(ne):
                head = pl.ds(e * MOE_SLOTS, gr)
                crows = pl.ds(e * gr, gr)
                carry = carry_ref[crows, :]
                merged = (wins_ref[b, head, :].astype(F32) + carry.astype(F32)).astype(BF16)
                wins_ref[b, head, :] = jnp.where(first, merged, wins_ref[b, head, :])

                ngc = used[e] // gr
                rem = used[e] % gr
                ng_w = jnp.clip(ngc - w * gpr, 0, gpr)

                def issue(j, c2, e=e):
                    src = wins_ref.at[b, pl.ds(e * MOE_SLOTS + j * gr, gr)]
                    dst = xe_ref.at[pl.ds(pl.multiple_of(e * cap + starts[e] + w * MOE_SLOTS + j * gr, gr), gr)]
                    pltpu.make_async_copy(src, dst, sem.at[b]).start()
                    return c2

                lax.fori_loop(0, ng_w, issue, 0)
                issued = issued + ng_w

                part = wins_ref[b, pl.ds(e * MOE_SLOTS + jnp.clip(ngc - w * gpr, 0, gpr - 1) * gr, gr), :]
                keep_part = (rem > 0) & (ngc // gpr == w)
                clear = (rem == 0) & first
                carry_ref[crows, :] = jnp.where(keep_part, part, jnp.where(clear, jnp.zeros_like(carry), carry))

            nout_ref[b] = issued
            return c

        lax.fori_loop(0, _num_rounds(used), round_body, 0)

    @pl.when(s == pl.num_programs(0) - 1)
    def _():
        drain(0)
        drain(1)


def _dispatch(offs, xn, pos3, *, cap):
    n, d = xn.shape
    ntile, ne, tt = pos3.shape
    nsub = min(DISPATCH_TILES_PER_STEP, ntile)
    assert ntile % nsub == 0
    kern = functools.partial(_dispatch_kernel, cap=cap)
    return pl.pallas_call(
        kern,
        grid_spec=pltpu.PrefetchScalarGridSpec(
            num_scalar_prefetch=1,
            grid=(ntile // nsub,),
            in_specs=[
                pl.BlockSpec((nsub * tt, d), lambda i, off: (i, 0)),
                pl.BlockSpec((nsub, ne, tt), lambda i, off: (i, 0, 0)),
            ],
            out_specs=pl.BlockSpec(memory_space=pl.ANY),
            scratch_shapes=[
                pltpu.VMEM((2, ne * MOE_SLOTS, d), BF16),
                pltpu.VMEM((ne * BF16_ROWS, d), BF16),
                pltpu.SemaphoreType.DMA((2,)),
                pltpu.SMEM((2,), I32),
            ],
        ),
        out_shape=jax.ShapeDtypeStruct((ne * cap, d), BF16),
        compiler_params=_cparams(("arbitrary",)),
        name="ec_dispatch",
    )(offs, xn, pos3)


def _ffn_kernel(x_ref, wg_ref, wu_ref, wd_ref, o_ref):
    x = x_ref[...]
    a = jnp.dot(x, wg_ref[0, 0], preferred_element_type=F32)
    u = jnp.dot(x, wu_ref[0, 0], preferred_element_type=F32)
    hid = (a * jax.nn.sigmoid(a) * u).astype(BF16)
    o_ref[...] = jnp.dot(hid, wd_ref[0, 0], preferred_element_type=F32).astype(o_ref.dtype)


def _expert_ffn(xe, wg, wu, wd, *, layer, cap, tr=1024):
    _, ne, d, f = wg.shape
    tr = min(tr, cap)
    nt = cap // tr
    return pl.pallas_call(
        _ffn_kernel,
        grid=(ne, nt),
        in_specs=[
            pl.BlockSpec((tr, d), lambda e, t: (e * nt + t, 0)),
            pl.BlockSpec((1, 1, d, f), lambda e, t: (layer, e, 0, 0)),
            pl.BlockSpec((1, 1, d, f), lambda e, t: (layer, e, 0, 0)),
            pl.BlockSpec((1, 1, f, d), lambda e, t: (layer, e, 0, 0)),
        ],
        out_specs=pl.BlockSpec((tr, d), lambda e, t: (e * nt + t, 0)),
        out_shape=jax.ShapeDtypeStruct((ne * cap, d), BF16),
        compiler_params=_cparams(("parallel", "parallel")),
        name="expert_ffn",
    )(xe, wg, wu, wd)


def _combine_kernel(off_ref, h_ref, aff_ref, pos_ref, p_ref, ye_ref, png_ref, wpg_ref, wpu_ref, fng_ref,
                    o_ref, wins_ref, acc_ref, sem, *, cap, final_norm):
    s = pl.program_id(0)
    ntile = pl.num_programs(0)
    nsub, ne, tt = pos_ref.shape
    spill_buf = 2 * nsub

    def window_copies(starts_t, w, buf):
        copies, wstarts = [], []
        for e in range(ne):
            ws = pl.multiple_of(jnp.minimum(starts_t[e] + w * MOE_SLOTS, cap - MOE_SLOTS), BF16_ROWS)
            copies.append(pltpu.make_async_copy(
                ye_ref.at[pl.ds(pl.multiple_of(e * cap + ws, BF16_ROWS), MOE_SLOTS)],
                wins_ref.at[buf, pl.ds(e * MOE_SLOTS, MOE_SLOTS)], sem.at[buf]))
            wstarts.append(ws)
        return copies, wstarts

    cur = s % 2
    tiles = [_tile_windows(off_ref, s * nsub + k, ne) for k in range(nsub)]

    @pl.when(s == 0)
    def _():
        for k in range(nsub):
            for cp in window_copies(tiles[k][0], 0, cur * nsub + k)[0]:
                cp.start()

    @pl.when(s + 1 < ntile)
    def _():
        for k in range(nsub):
            nxt_starts, _ = _tile_windows(off_ref, (s + 1) * nsub + k, ne)
            for cp in window_copies(nxt_starts, 0, (1 - cur) * nsub + k)[0]:
                cp.start()

    slot = lax.broadcasted_iota(I32, (MOE_SLOTS, tt), 0)
    tn = (((0,), (0,)), ((), ()))

    def gates(k, w, wstarts):
        starts = tiles[k][0]
        pos = pos_ref[k]
        aff = aff_ref[k]
        pieces = []
        for e in range(ne):
            base = starts[e] + w * MOE_SLOTS
            pe = pos[e:e + 1, :]
            in_round = (pe >= base) & (pe < base + MOE_SLOTS)
            hit = (jnp.broadcast_to(pe - wstarts[e], (MOE_SLOTS, tt)) == slot) & jnp.broadcast_to(in_round, (MOE_SLOTS, tt))
            pieces.append(jnp.where(hit, jnp.broadcast_to(aff[e:e + 1, :], (MOE_SLOTS, tt)), 0.0))
        return jnp.concatenate(pieces, axis=0).astype(BF16)

    for k in range(nsub):
        starts, used = tiles[k]
        rows = pl.ds(k * tt, tt)
        buf0 = cur * nsub + k
        copies0, wstarts0 = window_copies(starts, 0, buf0)
        a0 = gates(k, 0, wstarts0)
        for cp in copies0:
            cp.wait()
        acc_ref[rows, :] = lax.dot_general(a0, wins_ref[buf0], tn, preferred_element_type=F32)

        def round_body(w, c, k=k, starts=starts, rows=rows):
            copies, wstarts = window_copies(starts, w, spill_buf)
            for cp in copies:
                cp.start()
            a_w = gates(k, w, wstarts)
            for cp in copies:
                cp.wait()
            acc_ref[rows, :] += lax.dot_general(a_w, wins_ref[spill_buf], tn, preferred_element_type=F32)
            return c

        lax.fori_loop(1, _num_rounds(used), round_body, 0)
    h2 = h_ref[...] + acc_ref[...]
    gate = jax.nn.sigmoid(jnp.dot(_rms(h2, png_ref[...]).astype(BF16), wpg_ref[...], preferred_element_type=F32))
    up = jnp.dot(p_ref[0].astype(BF16), wpu_ref[...], preferred_element_type=F32)
    h3 = h2 + up * gate
    if final_norm:
        h3 = _rms(h3, fng_ref[...])
    o_ref[...] = h3


def _combine(offs, h1, aff3, pos3, p_all, ye, ple_norm, wpg, wpu, final_g, *, layer, cap, final_norm):
    n, d = h1.shape
    ntile, ne, tt = pos3.shape
    pd = p_all.shape[2]
    nsub = min(COMBINE_TILES_PER_STEP, ntile)
    assert ntile % nsub == 0
    rows = nsub * tt
    nbuf = 2 * nsub + 1
    kern = functools.partial(_combine_kernel, cap=cap, final_norm=final_norm)
    return pl.pallas_call(
        kern,
        grid_spec=pltpu.PrefetchScalarGridSpec(
            num_scalar_prefetch=1,
            grid=(ntile // nsub,),
            in_specs=[
                pl.BlockSpec((rows, d), lambda i, off: (i, 0)),
                pl.BlockSpec((nsub, ne, tt), lambda i, off: (i, 0, 0)),
                pl.BlockSpec((nsub, ne, tt), lambda i, off: (i, 0, 0)),
                pl.BlockSpec((1, rows, pd), lambda i, off: (layer, i, 0)),
                pl.BlockSpec(memory_space=pl.ANY),
                pl.BlockSpec((1, d), lambda i, off: (0, 0)),
                pl.BlockSpec((d, d), lambda i, off: (0, 0)),
                pl.BlockSpec((pd, d), lambda i, off: (0, 0)),
                pl.BlockSpec((1, d), lambda i, off: (0, 0)),
            ],
            out_specs=pl.BlockSpec((rows, d), lambda i, off: (i, 0)),
            scratch_shapes=[pltpu.VMEM((nbuf, ne * MOE_SLOTS, d), BF16), pltpu.VMEM((rows, d), F32),
                            pltpu.SemaphoreType.DMA((nbuf,))],
        ),
        out_shape=jax.ShapeDtypeStruct((n, d), F32),
        compiler_params=_cparams(("arbitrary",)),
        name="combine_ple",
    )(offs, h1, aff3, pos3, p_all, ye, ple_norm.reshape(1, d), wpg, wpu, final_g.reshape(1, d))


def _moe_block(h1, xn, aff3, p_all, w, i, *, final_norm):
    n, d = h1.shape
    cap = EC_CAPACITY * n // N_EXPERTS
    pos3, offs = _select(aff3, cap=cap)
    xe = _dispatch(offs, xn, pos3, cap=cap)
    ye = _expert_ffn(xe, w["exp_w_gate"], w["exp_w_up"], w["exp_w_down"], layer=i, cap=cap)
    return _combine(offs, h1, aff3, pos3, p_all, ye, w["ple_norm"][i], w["ple_w_gate"][i], w["ple_w_up"][i],
                    w["final_norm"], layer=i, cap=cap, final_norm=final_norm)


def _rope_tables(seq, dim):
    inv = ROPE_THETA ** (-jnp.arange(0, dim, 2, dtype=F32) / dim)
    ang = jnp.arange(seq, dtype=F32)[:, None] * inv[None, :]
    return jnp.cos(ang), jnp.sin(ang)


def _trunk(x, p, w):
    batch, seq, d = x.shape
    n = batch * seq
    h = x.reshape(n, d)
    dk = d // RET_HEADS
    cos, sin = _rope_tables(seq, dk)

    proj = _in_proj(h, w["mix_norm"][0], w["ret_w_in"][0], cos, sin, seq=seq, rope_units=2,
                    unit_scales=(1.0, float(dk) ** -0.5, 1.0, 1.0, 1.0, 1.0), slab_w=dk)
    a = _retention(proj, w["ret_decay_logit"][0], w["ret_gn_w"][0], batch=batch, seq=seq, d_model=d)
    h1, xn, aff3 = _out_router(a, w["ret_w_out"][0], h, w["ffn_norm"][0], w["router_w"][0])
    p_all = p.reshape(p.shape[0], n, p.shape[-1])
    h = _moe_block(h1, xn, aff3, p_all, w, 0, final_norm=False)

    qkv = _in_proj(h, w["mix_norm"][1], w["na_w_in"][0], cos, sin, seq=seq, rope_units=0,
                   unit_scales=(float(NA_HEAD_DIM) ** -0.5, 1.0, 1.0), slab_w=2 * NA_HEAD_DIM)
    a = _na_attention(qkv, _na_bias_table(w["na_rpb"][0], seq // GRID_W), batch=batch, seq=seq, d_model=d)
    h1, xn, aff3 = _out_router(a, w["na_w_out"][0], h, w["ffn_norm"][1], w["router_w"][1])
    y = _moe_block(h1, xn, aff3, p_all, w, 1, final_norm=True)
    return y.reshape(batch, seq, d)


def kernel(x_prompt, x_sample, p_prompt, p_sample, ret_w_in, ret_decay_logit, ret_gn_w, ret_w_out, na_w_in, na_rpb, na_w_out, mix_norm, ffn_norm, ple_norm, router_w, exp_w_gate, exp_w_up, exp_w_down, ple_w_up, ple_w_gate, final_norm):
    w = dict(
        ret_w_in=ret_w_in.astype(BF16), ret_decay_logit=ret_decay_logit, ret_gn_w=ret_gn_w,
        ret_w_out=ret_w_out.astype(BF16), na_w_in=na_w_in.astype(BF16), na_rpb=na_rpb,
        na_w_out=na_w_out.astype(BF16), mix_norm=mix_norm, ffn_norm=ffn_norm, ple_norm=ple_norm,
        router_w=router_w, exp_w_gate=exp_w_gate.astype(BF16), exp_w_up=exp_w_up.astype(BF16),
        exp_w_down=exp_w_down.astype(BF16), ple_w_up=ple_w_up.astype(BF16), ple_w_gate=ple_w_gate.astype(BF16),
        final_norm=final_norm,
    )
    return _trunk(x_prompt, p_prompt, w), _trunk(x_sample, p_sample, w)
```

```python
import functools

import numpy as np
import jax
import jax.numpy as jnp
from jax import lax
from jax.experimental import pallas as pl
from jax.experimental.pallas import tpu as pltpu

F32 = jnp.float32
BF16 = jnp.bfloat16
I32 = jnp.int32

NORM_EPS = 1e-6
ROPE_THETA = 10000.0
GRID_W = 64
RET_HEADS = 4
RET_CHUNK = 256
NA_HEADS = 16
NA_HEAD_DIM = 64
NA_MAX_ROWS = 8
NA_WIN_COLS = 16
NA_ROW_UNROLL = 16
N_EXPERTS = 16
EC_CAPACITY = 2
LANES = 128
BF16_ROWS = 16
MOE_TILE = 256
MOE_SLOTS = 64
DISPATCH_TILES_PER_STEP = 4
COMBINE_TILES_PER_STEP = 2
NEG_BIG = -1e30

_VMEM_LIMIT = 56 * 1024 * 1024


def _cparams(sem):
    return pltpu.CompilerParams(dimension_semantics=sem, vmem_limit_bytes=_VMEM_LIMIT)


def _rms(x, g):
    return x * lax.rsqrt(jnp.mean(x * x, axis=-1, keepdims=True) + NORM_EPS) * g


def _in_proj_kernel(x_ref, g_ref, w_ref, cos_ref, sin_ref, o_ref, xn_ref, *, nj, unit_w, rope_units, unit_scales,
                    rope_dim):
    j = pl.program_id(1)

    @pl.when(j == 0)
    def _():
        xn_ref[...] = _rms(x_ref[...], g_ref[...]).astype(BF16)

    upb = w_ref.shape[1] // unit_w
    sw = o_ref.shape[2]
    spu = unit_w // sw
    half = rope_dim // 2

    def rotate(y, c, cos, sin):
        assert sw == rope_dim
        for hh in range(spu):
            lo = hh * rope_dim
            x1 = y[:, lo:lo + half]
            x2 = y[:, lo + half:lo + rope_dim]
            o_ref[c * spu + hh, :, 0:half] = (x1 * cos - x2 * sin).astype(o_ref.dtype)
            o_ref[c * spu + hh, :, half:rope_dim] = (x1 * sin + x2 * cos).astype(o_ref.dtype)

    for c in range(upb):
        base = c * unit_w
        y = jnp.dot(xn_ref[...], w_ref[:, base:base + unit_w], preferred_element_type=F32)
        kinds = [(jj * upb + c < rope_units, unit_scales[jj * upb + c]) for jj in range(nj)]
        if all(k == kinds[0] for k in kinds):
            is_rope, scale = kinds[0]
            if is_rope:
                rotate(y, c, cos_ref[...] * scale, sin_ref[...] * scale)
            else:
                ys = (y if scale == 1.0 else y * scale).astype(o_ref.dtype)
                for sl in range(spu):
                    o_ref[c * spu + sl] = ys[:, sl * sw:(sl + 1) * sw]
        else:
            rope_here = jnp.bool_(False)
            scale = jnp.float32(1.0)
            for jj, (is_rope, sc) in enumerate(kinds):
                rope_here = jnp.where(j == jj, is_rope, rope_here)
                scale = jnp.where(j == jj, jnp.float32(sc), scale)
            rotate(y, c, jnp.where(rope_here, cos_ref[...], 1.0) * scale, jnp.where(rope_here, sin_ref[...], 0.0) * scale)


def _in_proj(x, g, w_bf16, cos, sin, *, seq, rope_units, unit_scales, slab_w, tm=1024, units_per_step=3):
    n, d = x.shape
    tm = min(tm, seq)
    ncol = w_bf16.shape[1]
    tn = units_per_step * d
    nj = ncol // tn
    assert len(unit_scales) == ncol // d
    rope_dim = 2 * cos.shape[1]
    nseq = seq // tm
    kern = functools.partial(_in_proj_kernel, nj=nj, unit_w=d, rope_units=rope_units, unit_scales=unit_scales,
                             rope_dim=rope_dim)
    return pl.pallas_call(
        kern,
        grid=(n // tm, ncol // tn),
        in_specs=[
            pl.BlockSpec((tm, d), lambda i, j: (i, 0)),
            pl.BlockSpec((1, d), lambda i, j: (0, 0)),
            pl.BlockSpec((d, tn), lambda i, j: (0, j)),
            pl.BlockSpec((tm, cos.shape[1]), lambda i, j: (i % nseq, 0)),
            pl.BlockSpec((tm, cos.shape[1]), lambda i, j: (i % nseq, 0)),
        ],
        out_specs=pl.BlockSpec((tn // slab_w, tm, slab_w), lambda i, j: (j, i, 0)),
        out_shape=jax.ShapeDtypeStruct((ncol // slab_w, n, slab_w), BF16),
        scratch_shapes=[pltpu.VMEM((tm, d), BF16)],
        compiler_params=_cparams(("parallel", "arbitrary")),
        name="in_proj",
    )(x, g.reshape(1, d), w_bf16, cos, sin)


def _retention_kernel(dl_ref, q_ref, k_ref, v_ref, g_ref, gnw_ref, o_ref, acc_ref, stf_ref, stb_ref, *, seq, chunk):
    h = pl.program_id(1)
    nc = seq // chunk
    dl = dl_ref[...]
    lg = jnp.minimum(dl, 0.0) - jnp.log1p(jnp.exp(-jnp.abs(dl)))
    col = lax.broadcasted_iota(I32, dl.shape, 1)
    lgh = jnp.sum(jnp.where(col == h, lg, 0.0), axis=1, keepdims=True)
    lgf = lgh[0:1, :]
    lgb = lgh[1:2, :]

    ri = lax.broadcasted_iota(I32, (chunk, chunk), 0)
    ci = lax.broadcasted_iota(I32, (chunk, chunk), 1)
    diff = (ri - ci).astype(F32)
    dmat = jnp.exp(jnp.where(diff >= 0, diff * lgf, -diff * lgb))
    pos = lax.broadcasted_iota(I32, (chunk, 1), 0).astype(F32)
    qdec_f = jnp.exp((pos + 1.0) * lgf)
    kdec_f = jnp.exp((chunk - 1.0 - pos) * lgf)
    qdec_b = jnp.exp((chunk - pos) * lgb)
    kdec_b = jnp.exp(pos * lgb)
    cdec_f = jnp.exp(chunk * lgf)
    cdec_b = jnp.exp(chunk * lgb)
    nt = (((1,), (1,)), ((), ()))
    tn = (((0,), (0,)), ((), ()))

    stf_ref[...] = jnp.zeros_like(stf_ref)
    stb_ref[...] = jnp.zeros_like(stb_ref)

    def chunk_slice(c):
        return pl.ds(pl.multiple_of(c * chunk, chunk), chunk)

    def scaled(x, dec):
        return (x.astype(F32) * dec).astype(BF16)

    def wide(ref, sl):
        return jnp.concatenate([ref[i, sl, :] for i in range(ref.shape[0])], axis=1)

    def step(t):
        slf = chunk_slice(t)
        slb = chunk_slice(nc - 1 - t)
        qf, kf, vf = q_ref[0, slf, :], k_ref[0, slf, :], wide(v_ref, slf)
        qb, kb, vb = q_ref[0, slb, :], k_ref[0, slb, :], wide(v_ref, slb)
        s = lax.dot_general(qf, kf, nt, preferred_element_type=F32) * dmat
        cross_b = jnp.dot(scaled(qb, qdec_b), stb_ref[...].astype(BF16), preferred_element_type=F32)
        inner = jnp.dot(s.astype(BF16), vf, preferred_element_type=F32)
        cross_f = jnp.dot(scaled(qf, qdec_f), stf_ref[...].astype(BF16), preferred_element_type=F32)
        stb_ref[...] = stb_ref[...] * cdec_b + lax.dot_general(scaled(kb, kdec_b), vb, tn, preferred_element_type=F32)
        stf_ref[...] = stf_ref[...] * cdec_f + lax.dot_general(scaled(kf, kdec_f), vf, tn, preferred_element_type=F32)
        return slf, inner + cross_f, slb, cross_b

    def finish(sl, o):
        mu = jnp.mean(o, axis=-1, keepdims=True)
        oc = o - mu
        var = jnp.mean(oc * oc, axis=-1, keepdims=True)
        on = oc * lax.rsqrt(var + NORM_EPS) * gnw_ref[...]
        gate = wide(g_ref, sl).astype(F32)
        o_ref[0, sl, :] = (gate * jax.nn.sigmoid(gate) * on).astype(o_ref.dtype)

    def first_half(t, carry):
        slf, of, slb, ob = step(t)
        acc_ref[slf, :] = of
        acc_ref[slb, :] = ob
        return carry

    def second_half(t, carry):
        slf, of, slb, ob = step(t)
        finish(slf, acc_ref[slf, :] + of)
        finish(slb, acc_ref[slb, :] + ob)
        return carry

    lax.fori_loop(0, nc // 2, first_half, 0)
    lax.fori_loop(nc // 2, nc, second_half, 0)


def _retention(proj, decay_logit, gn_w, *, batch, seq, d_model):
    heads = RET_HEADS
    dk = d_model // heads
    dv = 2 * d_model // heads
    n = batch * seq
    assert seq % (2 * RET_CHUNK) == 0, "the two scans are paired chunk by chunk"
    kern = functools.partial(_retention_kernel, seq=seq, chunk=RET_CHUNK)
    return pl.pallas_call(
        kern,
        grid=(batch, heads),
        in_specs=[
            pl.BlockSpec((2, heads), lambda b, h: (0, 0)),
            pl.BlockSpec((1, seq, dk), lambda b, h: (h, b, 0)),
            pl.BlockSpec((1, seq, dk), lambda b, h: (heads + h, b, 0)),
            pl.BlockSpec((dv // dk, seq, dk), lambda b, h: (heads + h, b, 0)),
            pl.BlockSpec((dv // dk, seq, dk), lambda b, h: (2 * heads + h, b, 0)),
            pl.BlockSpec((1, dv), lambda b, h: (0, h)),
        ],
        out_specs=pl.BlockSpec((1, seq, dv), lambda b, h: (h, b, 0)),
        out_shape=jax.ShapeDtypeStruct((heads, n, dv), BF16),
        scratch_shapes=[pltpu.VMEM((seq, dv), F32), pltpu.VMEM((dk, dv), F32), pltpu.VMEM((dk, dv), F32)],
        compiler_params=_cparams(("parallel", "parallel")),
        name="retention",
    )(decay_logit, proj, proj, proj, proj, gn_w.reshape(1, -1))


def _na_kernel(bias_ref, q_ref, k_ref, v_ref, o_ref, *, rows):
    gw = GRID_W
    kr = min(NA_MAX_ROWS, rows)
    dh = NA_HEAD_DIM
    lane_q = lax.broadcasted_iota(I32, (gw, 2 * dh), 1)
    nt = (((1,), (1,)), ((), ()))

    group = min(NA_ROW_UNROLL, rows)

    def body(it, carry):
        rr = [it * group + i for i in range(group)]
        rs = [jnp.clip(r - kr // 2, 0, rows - kr) for r in rr]
        s = []
        for r, r0 in zip(rr, rs):
            q2 = q_ref[0, pl.ds(pl.multiple_of(r * gw, gw), gw), :]
            zero = jnp.zeros_like(q2)
            qq = jnp.concatenate([jnp.where(lane_q < dh, q2, zero), jnp.where(lane_q >= dh, q2, zero)], axis=0)
            k2 = k_ref[0, pl.ds(pl.multiple_of(r0 * gw, gw), kr * gw), :]
            s.append(lax.dot_general(qq, k2, nt, preferred_element_type=F32) + bias_ref[0, r - r0])
        m = [jnp.max(si, axis=-1, keepdims=True) for si in s]
        p = [jnp.exp(si - mi) for si, mi in zip(s, m)]
        inv = [1.0 / jnp.sum(pi, axis=-1, keepdims=True) for pi in p]
        o = [jnp.dot(pi.astype(BF16), v_ref[0, pl.ds(pl.multiple_of(r0 * gw, gw), kr * gw), :],
                     preferred_element_type=F32) * ii for pi, ii, r0 in zip(p, inv, rs)]
        for r, oi in zip(rr, o):
            out = jnp.where(lane_q < dh, oi[0:gw, :], oi[gw:2 * gw, :])
            o_ref[0, pl.ds(pl.multiple_of(r * gw, gw), gw), :] = out.astype(o_ref.dtype)
        return carry

    lax.fori_loop(0, rows // group, body, 0)


def _na_bias_table(rpb, rows):
    kr = min(NA_MAX_ROWS, rows)
    c = np.arange(GRID_W)
    kc = np.arange(GRID_W)
    win_start = np.clip(c - NA_WIN_COLS // 2, 0, GRID_W - NA_WIN_COLS)
    valid = (kc[None, :] >= win_start[:, None]) & (kc[None, :] < win_start[:, None] + NA_WIN_COLS)
    dc_idx = np.clip(kc[None, :] - c[:, None] + NA_WIN_COLS - 1, 0, 2 * NA_WIN_COLS - 2)
    delta = np.arange(kr)
    a = np.arange(kr)
    dr_idx = a[None, :] - delta[:, None] + NA_MAX_ROWS - 1
    row_sel = (dr_idx[:, :, None] == np.arange(rpb.shape[1])).astype(np.float32)
    col_sel = (dc_idx[None, :, :] == np.arange(rpb.shape[2])[:, None, None]).astype(np.float32)
    t = jnp.einsum("dar,hrs,sck->hdcak", row_sel, rpb.astype(F32), col_sel, precision=lax.Precision.HIGHEST)
    t = jnp.where(jnp.asarray(valid)[None, None, :, None, :], t, NEG_BIG)
    h = rpb.shape[0]
    t = t.reshape(h // 2, 2, kr, GRID_W, kr * GRID_W).transpose(0, 2, 1, 3, 4)
    return t.reshape(h // 2, kr, 2 * GRID_W, kr * GRID_W)


def _na_attention(qkv, bias, *, batch, seq, d_model):
    n = batch * seq
    rows = seq // GRID_W
    kr = min(NA_MAX_ROWS, rows)
    pairs = NA_HEADS // 2
    pw = 2 * NA_HEAD_DIM
    assert rows % min(NA_ROW_UNROLL, rows) == 0
    kern = functools.partial(_na_kernel, rows=rows)
    return pl.pallas_call(
        kern,
        grid=(batch, pairs),
        in_specs=[
            pl.BlockSpec((1, kr, 2 * GRID_W, kr * GRID_W), lambda b, hp: (hp, 0, 0, 0)),
            pl.BlockSpec((1, seq, pw), lambda b, hp: (hp, b, 0)),
            pl.BlockSpec((1, seq, pw), lambda b, hp: (pairs + hp, b, 0)),
            pl.BlockSpec((1, seq, pw), lambda b, hp: (2 * pairs + hp, b, 0)),
        ],
        out_specs=pl.BlockSpec((1, seq, pw), lambda b, hp: (hp, b, 0)),
        out_shape=jax.ShapeDtypeStruct((pairs, n, pw), BF16),
        compiler_params=_cparams(("parallel", "parallel")),
        name="na_attention",
    )(bias, qkv, qkv, qkv)


def _out_router_kernel(a_ref, w_ref, h_ref, g_ref, rw_ref, h1_ref, xn_ref, aff_ref):
    a = jnp.concatenate([a_ref[i] for i in range(a_ref.shape[0])], axis=1)
    y = jnp.dot(a, w_ref[...], preferred_element_type=F32) + h_ref[...]
    h1_ref[...] = y
    xn = _rms(y, g_ref[...])
    xn_ref[...] = xn.astype(xn_ref.dtype)
    nt = (((1,), (1,)), ((), ()))
    ne = rw_ref.shape[0]
    xh = xn.astype(BF16)
    xl = (xn - xh.astype(F32)).astype(BF16)
    rw = rw_ref[...]
    rh = rw.astype(BF16)
    rl = (rw - rh.astype(F32)).astype(BF16)
    t1 = lax.dot_general(jnp.concatenate([rh, rl], axis=0), xh, nt, preferred_element_type=F32)
    t2 = lax.dot_general(rh, xl, nt, preferred_element_type=F32)
    logits = t1[0:ne, :] + (t1[ne:2 * ne, :] + t2)
    m = jnp.max(logits, axis=0, keepdims=True)
    e = jnp.exp(logits - m)
    aff = e / jnp.sum(e, axis=0, keepdims=True)
    for t in range(aff_ref.shape[0]):
        aff_ref[t] = aff[:, t * MOE_TILE:(t + 1) * MOE_TILE]


def _out_router(a, w_bf16, h, g, router_w, *, tm=1024):
    nslab, n, sw = a.shape
    kdim = nslab * sw
    tm = min(tm, n)
    d = h.shape[1]
    ne = router_w.shape[1]
    tpb = tm // MOE_TILE
    return pl.pallas_call(
        _out_router_kernel,
        grid=(n // tm,),
        in_specs=[
            pl.BlockSpec((nslab, tm, sw), lambda i: (0, i, 0)),
            pl.BlockSpec((kdim, d), lambda i: (0, 0)),
            pl.BlockSpec((tm, d), lambda i: (i, 0)),
            pl.BlockSpec((1, d), lambda i: (0, 0)),
            pl.BlockSpec((ne, d), lambda i: (0, 0)),
        ],
        out_specs=[
            pl.BlockSpec((tm, d), lambda i: (i, 0)),
            pl.BlockSpec((tm, d), lambda i: (i, 0)),
            pl.BlockSpec((tpb, ne, MOE_TILE), lambda i: (i, 0, 0)),
        ],
        out_shape=[
            jax.ShapeDtypeStruct((n, d), F32),
            jax.ShapeDtypeStruct((n, d), BF16),
            jax.ShapeDtypeStruct((n // MOE_TILE, ne, MOE_TILE), F32),
        ],
        compiler_params=_cparams(("parallel",)),
        name="out_router",
    )(a, w_bf16, h, g.reshape(1, d), router_w.T)


def _select_kernel(aff_ref, pos_ref, off_ref, *, cap):
    ntile, ne, tt = aff_ref.shape

    def count_ge(bits):
        level = pltpu.bitcast(bits, F32)

        def body(c, acc):
            return acc + (aff_ref[c] >= level).astype(I32)
        acc = lax.fori_loop(0, ntile, body, jnp.zeros((ne, tt), I32), unroll=4)
        return jnp.sum(acc, axis=1, keepdims=True)

    def bit_step(i, t):
        cand = t | jnp.left_shift(jnp.int32(1), 30 - i)
        return jnp.where(count_ge(cand) >= cap, cand, t)

    thr_bits = lax.fori_loop(0, 31, bit_step, jnp.zeros((ne, 1), I32))
    thr = pltpu.bitcast(thr_bits, F32)
    above = pltpu.bitcast(thr_bits + 1, F32)
    need_eq = (cap - count_ge(thr_bits + 1)).astype(F32)

    li = lax.broadcasted_iota(I32, (tt, tt), 0)
    lj = lax.broadcasted_iota(I32, (tt, tt), 1)
    upper = (li < lj).astype(BF16)

    def tile_body(c, carry):
        eq_carry, pos_carry = carry
        off_ref[c] = jnp.broadcast_to(pos_carry, (ne, LANES)).astype(I32)
        x = aff_ref[c]
        gt = x >= above
        eq = (x >= thr) & jnp.logical_not(gt)
        gtf = gt.astype(F32)
        eqf = eq.astype(F32)
        pre = jnp.dot(jnp.concatenate([gtf, eqf], axis=0).astype(BF16), upper, preferred_element_type=F32)
        eq_rank = eq_carry + pre[ne:2 * ne, :]
        taken = eq & (eq_rank < need_eq)
        taken_before = jnp.minimum(eq_rank, need_eq) - jnp.minimum(eq_carry, need_eq)
        pos = pos_carry + pre[0:ne, :] + taken_before
        pos_ref[c] = jnp.where(gt | taken, pos.astype(I32), -1)
        eq_next = eq_carry + jnp.sum(eqf, axis=1, keepdims=True)
        taken_total = jnp.minimum(eq_next, need_eq) - jnp.minimum(eq_carry, need_eq)
        return eq_next, pos_carry + jnp.sum(gtf, axis=1, keepdims=True) + taken_total

    lax.fori_loop(0, ntile, tile_body, (jnp.zeros((ne, 1), F32), jnp.zeros((ne, 1), F32)), unroll=2)


def _select(aff3, *, cap):
    ntile, ne, tt = aff3.shape
    kern = functools.partial(_select_kernel, cap=cap)
    pos3, off3 = pl.pallas_call(
        kern,
        out_shape=[
            jax.ShapeDtypeStruct((ntile, ne, tt), I32),
            jax.ShapeDtypeStruct((ntile, ne, LANES), I32),
        ],
        compiler_params=pltpu.CompilerParams(vmem_limit_bytes=_VMEM_LIMIT),
        name="ec_select",
    )(aff3)
    offs = jnp.concatenate([off3[:, :, 0], jnp.full((1, ne), cap, I32)], axis=0).reshape(-1)
    return pos3, offs


def _tile_windows(off_ref, s, ne):
    starts, used = [], []
    for e in range(ne):
        off = off_ref[s * ne + e]
        nxt = off_ref[(s + 1) * ne + e]
        st = (off // BF16_ROWS) * BF16_ROWS
        starts.append(st)
        used.append(nxt - st)
    return starts, used


def _num_rounds(used):
    m = used[0]
    for u in used[1:]:
        m = jnp.maximum(m, u)
    return (m + MOE_SLOTS - 1) // MOE_SLOTS


def _dispatch_kernel(off_ref, x_ref, pos_ref, xe_ref, wins_ref, carry_ref, sem, nout_ref, *, cap):
    s = pl.program_id(0)
    nsub, ne, tt = pos_ref.shape
    gr = BF16_ROWS
    gpr = MOE_SLOTS // gr

    @pl.when(s == 0)
    def _():
        carry_ref[...] = jnp.zeros_like(carry_ref)
        nout_ref[0] = 0
        nout_ref[1] = 0

    slot = lax.broadcasted_iota(I32, (MOE_SLOTS, tt), 0)

    def drain(b):
        def wait_one(i, c):
            pltpu.make_async_copy(wins_ref.at[b, pl.ds(0, gr)], xe_ref.at[pl.ds(0, gr)], sem.at[b]).wait()
            return c
        lax.fori_loop(0, nout_ref[b], wait_one, 0)
        nout_ref[b] = 0

    for k in range(nsub):
        tile = s * nsub + k
        starts, used = _tile_windows(off_ref, tile, ne)
        pos = pos_ref[k]
        x = x_ref[pl.ds(k * tt, tt), :]

        def round_body(w, c, tile=tile, starts=starts, used=used, pos=pos, x=x):
            b = (tile + w) % 2
            drain(b)
            pieces = []
            for e in range(ne):
                key = pos[e:e + 1, :] - (starts[e] + w * MOE_SLOTS)
                pieces.append((jnp.broadcast_to(key, (MOE_SLOTS, tt)) == slot).astype(F32))
            onehot = jnp.concatenate(pieces, axis=0).astype(BF16)
            wins_ref[b] = jnp.dot(onehot, x, preferred_element_type=F32).astype(BF16)
            first = w == 0
            issued = 0
            for e in range(ne):
                head = pl.ds(e * MOE_SLOTS, gr)
                crows = pl.ds(e * gr, gr)
                carry = carry_ref[crows, :]
                merged = (wins_ref[b, head, :].astype(F32) + carry.astype(F32)).astype(BF16)
                wins_ref[b, head, :] = jnp.where(first, merged, wins_ref[b, head, :])

                ngc = used[e] // gr
                rem = used[e] % gr
                ng_w = jnp.clip(ngc - w * gpr, 0, gpr)

                def issue(j, c2, e=e):
                    src = wins_ref.at[b, pl.ds(e * MOE_SLOTS + j * gr, gr)]
                    dst = xe_ref.at[pl.ds(pl.multiple_of(e * cap + starts[e] + w * MOE_SLOTS + j * gr, gr), gr)]
                    pltpu.make_async_copy(src, dst, sem.at[b]).start()
                    return c2

                lax.fori_loop(0, ng_w, issue, 0)
                issued = issued + ng_w

                part = wins_ref[b, pl.ds(e * MOE_SLOTS + jnp.clip(ngc - w * gpr, 0, gpr - 1) * gr, gr), :]
                keep_part = (rem > 0) & (ngc // gpr == w)
                clear = (rem == 0) & first
                carry_ref[crows, :] = jnp.where(keep_part, part, jnp.where(clear, jnp.zeros_like(carry), carry))

            nout_ref[b] = issued
            return c

        lax.fori_loop(0, _num_rounds(used), round_body, 0)

    @pl.when(s == pl.num_programs(0) - 1)
    def _():
        drain(0)
        drain(1)


def _dispatch(offs, xn, pos3, *, cap):
    n, d = xn.shape
    ntile, ne, tt = pos3.shape
    nsub = min(DISPATCH_TILES_PER_STEP, ntile)
    assert ntile % nsub == 0
    kern = functools.partial(_dispatch_kernel, cap=cap)
    return pl.pallas_call(
        kern,
        grid_spec=pltpu.PrefetchScalarGridSpec(
            num_scalar_prefetch=1,
            grid=(ntile // nsub,),
            in_specs=[
                pl.BlockSpec((nsub * tt, d), lambda i, off: (i, 0)),
                pl.BlockSpec((nsub, ne, tt), lambda i, off: (i, 0, 0)),
            ],
            out_specs=pl.BlockSpec(memory_space=pl.ANY),
            scratch_shapes=[
                pltpu.VMEM((2, ne * MOE_SLOTS, d), BF16),
                pltpu.VMEM((ne * BF16_ROWS, d), BF16),
                pltpu.SemaphoreType.DMA((2,)),
                pltpu.SMEM((2,), I32),
            ],
        ),
        out_shape=jax.ShapeDtypeStruct((ne * cap, d), BF16),
        compiler_params=_cparams(("arbitrary",)),
        name="ec_dispatch",
    )(offs, xn, pos3)


def _ffn_kernel(x_ref, wg_ref, wu_ref, wd_ref, o_ref):
    x = x_ref[...]
    a = jnp.dot(x, wg_ref[0, 0], preferred_element_type=F32)
    u = jnp.dot(x, wu_ref[0, 0], preferred_element_type=F32)
    hid = (a * jax.nn.sigmoid(a) * u).astype(BF16)
    o_ref[...] = jnp.dot(hid, wd_ref[0, 0], preferred_element_type=F32).astype(o_ref.dtype)


def _expert_ffn(xe, wg, wu, wd, *, layer, cap, tr=1024):
    _, ne, d, f = wg.shape
    tr = min(tr, cap)
    nt = cap // tr
    return pl.pallas_call(
        _ffn_kernel,
        grid=(ne, nt),
        in_specs=[
            pl.BlockSpec((tr, d), lambda e, t: (e * nt + t, 0)),
            pl.BlockSpec((1, 1, d, f), lambda e, t: (layer, e, 0, 0)),
            pl.BlockSpec((1, 1, d, f), lambda e, t: (layer, e, 0, 0)),
            pl.BlockSpec((1, 1, f, d), lambda e, t: (layer, e, 0, 0)),
        ],
        out_specs=pl.BlockSpec((tr, d), lambda e, t: (e * nt + t, 0)),
        out_shape=jax.ShapeDtypeStruct((ne * cap, d), BF16),
        compiler_params=_cparams(("parallel", "parallel")),
        name="expert_ffn",
    )(xe, wg, wu, wd)


def _combine_kernel(off_ref, h_ref, aff_ref, pos_ref, p_ref, ye_ref, png_ref, wpg_ref, wpu_ref, fng_ref,
                    o_ref, wins_ref, acc_ref, sem, *, cap, final_norm):
    s = pl.program_id(0)
    ntile = pl.num_programs(0)
    nsub, ne, tt = pos_ref.shape
    spill_buf = 2 * nsub

    def window_copies(starts_t, w, buf):
        copies, wstarts = [], []
        for e in range(ne):
            ws = pl.multiple_of(jnp.minimum(starts_t[e] + w * MOE_SLOTS, cap - MOE_SLOTS), BF16_ROWS)
            copies.append(pltpu.make_async_copy(
                ye_ref.at[pl.ds(pl.multiple_of(e * cap + ws, BF16_ROWS), MOE_SLOTS)],
                wins_ref.at[buf, pl.ds(e * MOE_SLOTS, MOE_SLOTS)], sem.at[buf]))
            wstarts.append(ws)
        return copies, wstarts

    cur = s % 2
    tiles = [_tile_windows(off_ref, s * nsub + k, ne) for k in range(nsub)]

    @pl.when(s == 0)
    def _():
        for k in range(nsub):
            for cp in window_copies(tiles[k][0], 0, cur * nsub + k)[0]:
                cp.start()

    @pl.when(s + 1 < ntile)
    def _():
        for k in range(nsub):
            nxt_starts, _ = _tile_windows(off_ref, (s + 1) * nsub + k, ne)
            for cp in window_copies(nxt_starts, 0, (1 - cur) * nsub + k)[0]:
                cp.start()

    slot = lax.broadcasted_iota(I32, (MOE_SLOTS, tt), 0)
    tn = (((0,), (0,)), ((), ()))

    def gates(k, w, wstarts):
        starts = tiles[k][0]
        pos = pos_ref[k]
        aff = aff_ref[k]
        pieces = []
        for e in range(ne):
            base = starts[e] + w * MOE_SLOTS
            pe = pos[e:e + 1, :]
            in_round = (pe >= base) & (pe < base + MOE_SLOTS)
            hit = (jnp.broadcast_to(pe - wstarts[e], (MOE_SLOTS, tt)) == slot) & jnp.broadcast_to(in_round, (MOE_SLOTS, tt))
            pieces.append(jnp.where(hit, jnp.broadcast_to(aff[e:e + 1, :], (MOE_SLOTS, tt)), 0.0))
        return jnp.concatenate(pieces, axis=0).astype(BF16)

    for k in range(nsub):
        starts, used = tiles[k]
        rows = pl.ds(k * tt, tt)
        buf0 = cur * nsub + k
        copies0, wstarts0 = window_copies(starts, 0, buf0)
        a0 = gates(k, 0, wstarts0)
        for cp in copies0:
            cp.wait()
        acc_ref[rows, :] = lax.dot_general(a0, wins_ref[buf0], tn, preferred_element_type=F32)

        def round_body(w, c, k=k, starts=starts, rows=rows):
            copies, wstarts = window_copies(starts, w, spill_buf)
            for cp in copies:
                cp.start()
            a_w = gates(k, w, wstarts)
            for cp in copies:
                cp.wait()
            acc_ref[rows, :] += lax.dot_general(a_w, wins_ref[spill_buf], tn, preferred_element_type=F32)
            return c

        lax.fori_loop(1, _num_rounds(used), round_body, 0)
    h2 = h_ref[...] + acc_ref[...]
    gate = jax.nn.sigmoid(jnp.dot(_rms(h2, png_ref[...]).astype(BF16), wpg_ref[...], preferred_element_type=F32))
    up = jnp.dot(p_ref[0].astype(BF16), wpu_ref[...], preferred_element_type=F32)
    h3 = h2 + up * gate
    if final_norm:
        h3 = _rms(h3, fng_ref[...])
    o_ref[...] = h3


def _combine(offs, h1, aff3, pos3, p_all, ye, ple_norm, wpg, wpu, final_g, *, layer, cap, final_norm):
    n, d = h1.shape
    ntile, ne, tt = pos3.shape
    pd = p_all.shape[2]
    nsub = min(COMBINE_TILES_PER_STEP, ntile)
    assert ntile % nsub == 0
    rows = nsub * tt
    nbuf = 2 * nsub + 1
    kern = functools.partial(_combine_kernel, cap=cap, final_norm=final_norm)
    return pl.pallas_call(
        kern,
        grid_spec=pltpu.PrefetchScalarGridSpec(
            num_scalar_prefetch=1,
            grid=(ntile // nsub,),
            in_specs=[
                pl.BlockSpec((rows, d), lambda i, off: (i, 0)),
                pl.BlockSpec((nsub, ne, tt), lambda i, off: (i, 0, 0)),
                pl.BlockSpec((nsub, ne, tt), lambda i, off: (i, 0, 0)),
                pl.BlockSpec((1, rows, pd), lambda i, off: (layer, i, 0)),
                pl.BlockSpec(memory_space=pl.ANY),
                pl.BlockSpec((1, d), lambda i, off: (0, 0)),
                pl.BlockSpec((d, d), lambda i, off: (0, 0)),
                pl.BlockSpec((pd, d), lambda i, off: (0, 0)),
                pl.BlockSpec((1, d), lambda i, off: (0, 0)),
            ],
            out_specs=pl.BlockSpec((rows, d), lambda i, off: (i, 0)),
            scratch_shapes=[pltpu.VMEM((nbuf, ne * MOE_SLOTS, d), BF16), pltpu.VMEM((rows, d), F32),
                            pltpu.SemaphoreType.DMA((nbuf,))],
        ),
        out_shape=jax.ShapeDtypeStruct((n, d), F32),
        compiler_params=_cparams(("arbitrary",)),
        name="combine_ple",
    )(offs, h1, aff3, pos3, p_all, ye, ple_norm.reshape(1, d), wpg, wpu, final_g.reshape(1, d))


def _moe_block(h1, xn, aff3, p_all, w, i, *, final_norm):
    n, d = h1.shape
    cap = EC_CAPACITY * n // N_EXPERTS
    pos3, offs = _select(aff3, cap=cap)
    xe = _dispatch(offs, xn, pos3, cap=cap)
    ye = _expert_ffn(xe, w["exp_w_gate"], w["exp_w_up"], w["exp_w_down"], layer=i, cap=cap)
    return _combine(offs, h1, aff3, pos3, p_all, ye, w["ple_norm"][i], w["ple_w_gate"][i], w["ple_w_up"][i],
                    w["final_norm"], layer=i, cap=cap, final_norm=final_norm)


def _rope_tables(seq, dim):
    inv = ROPE_THETA ** (-jnp.arange(0, dim, 2, dtype=F32) / dim)
    ang = jnp.arange(seq, dtype=F32)[:, None] * inv[None, :]
    return jnp.cos(ang), jnp.sin(ang)


def _trunk(x, p, w):
    batch, seq, d = x.shape
    n = batch * seq
    h = x.reshape(n, d)
    dk = d // RET_HEADS
    cos, sin = _rope_tables(seq, dk)

    proj = _in_proj(h, w["mix_norm"][0], w["ret_w_in"][0], cos, sin, seq=seq, rope_units=2,
                    unit_scales=(1.0, float(dk) ** -0.5, 1.0, 1.0, 1.0, 1.0), slab_w=dk)
    a = _retention(proj, w["ret_decay_logit"][0], w["ret_gn_w"][0], batch=batch, seq=seq, d_model=d)
    h1, xn, aff3 = _out_router(a, w["ret_w_out"][0], h, w["ffn_norm"][0], w["router_w"][0])
    p_all = p.reshape(p.shape[0], n, p.shape[-1])
    h = _moe_block(h1, xn, aff3, p_all, w, 0, final_norm=False)

    qkv = _in_proj(h, w["mix_norm"][1], w["na_w_in"][0], cos, sin, seq=seq, rope_units=0,
                   unit_scales=(float(NA_HEAD_DIM) ** -0.5, 1.0, 1.0), slab_w=2 * NA_HEAD_DIM)
    a = _na_attention(qkv, _na_bias_table(w["na_rpb"][0], seq // GRID_W), batch=batch, seq=seq, d_model=d)
    h1, xn, aff3 = _out_router(a, w["na_w_out"][0], h, w["ffn_norm"][1], w["router_w"][1])
    y = _moe_block(h1, xn, aff3, p_all, w, 1, final_norm=True)
    return y.reshape(batch, seq, d)


def kernel(x_prompt, x_sample, p_prompt, p_sample, ret_w_in, ret_decay_logit, ret_gn_w, ret_w_out, na_w_in, na_rpb, na_w_out, mix_norm, ffn_norm, ple_norm, router_w, exp_w_gate, exp_w_up, exp_w_down, ple_w_up, ple_w_gate, final_norm):
    w = dict(
        ret_w_in=ret_w_in.astype(BF16), ret_decay_logit=ret_decay_logit, ret_gn_w=ret_gn_w,
        ret_w_out=ret_w_out.astype(BF16), na_w_in=na_w_in.astype(BF16), na_rpb=na_rpb,
        na_w_out=na_w_out.astype(BF16), mix_norm=mix_norm, ffn_norm=ffn_norm, ple_norm=ple_norm,
        router_w=router_w, exp_w_gate=exp_w_gate.astype(BF16), exp_w_up=exp_w_up.astype(BF16),
        exp_w_down=exp_w_down.astype(BF16), ple_w_up=ple_w_up.astype(BF16), ple_w_gate=ple_w_gate.astype(BF16),
        final_norm=final_norm,
    )
    return _trunk(x_prompt, p_prompt, w), _trunk(x_sample, p_sample, w)
```

```python
import functools

import numpy as np
import jax
import jax.numpy as jnp
from jax import lax
from jax.experimental import pallas as pl
from jax.experimental.pallas import tpu as pltpu

F32 = jnp.float32
BF16 = jnp.bfloat16
I32 = jnp.int32

NORM_EPS = 1e-6
ROPE_THETA = 10000.0
GRID_W = 64
RET_HEADS = 4
RET_CHUNK = 256
NA_HEADS = 16
NA_HEAD_DIM = 64
NA_MAX_ROWS = 8
NA_WIN_COLS = 16
NA_ROW_UNROLL = 32
N_EXPERTS = 16
EC_CAPACITY = 2
LANES = 128
BF16_ROWS = 16
MOE_TILE = 256
MOE_SLOTS = 64
DISPATCH_TILES_PER_STEP = 4
COMBINE_TILES_PER_STEP = 2
NEG_BIG = -1e30

V7X_VMEM_BYTES = 64 * 1024 * 1024
_VMEM_LIMIT = V7X_VMEM_BYTES * 7 // 8


def _cparams(sem):
    return pltpu.CompilerParams(dimension_semantics=sem, vmem_limit_bytes=_VMEM_LIMIT)


def _rms(x, g):
    return x * lax.rsqrt(jnp.mean(x * x, axis=-1, keepdims=True) + NORM_EPS) * g


def _in_proj_kernel(x_ref, g_ref, w_ref, cos_ref, sin_ref, o_ref, xn_ref, *, nj, unit_w, rope_units, unit_scales,
                    rope_dim):
    j = pl.program_id(1)

    @pl.when(j == 0)
    def _():
        xn_ref[...] = _rms(x_ref[...], g_ref[...]).astype(BF16)

    upb = w_ref.shape[1] // unit_w
    sw = o_ref.shape[2]
    spu = unit_w // sw
    half = rope_dim // 2

    def rotate(y, c, cos, sin):
        assert sw == rope_dim
        for hh in range(spu):
            lo = hh * rope_dim
            x1 = y[:, lo:lo + half]
            x2 = y[:, lo + half:lo + rope_dim]
            o_ref[c * spu + hh, :, 0:half] = (x1 * cos - x2 * sin).astype(o_ref.dtype)
            o_ref[c * spu + hh, :, half:rope_dim] = (x1 * sin + x2 * cos).astype(o_ref.dtype)

    for c in range(upb):
        base = c * unit_w
        y = jnp.dot(xn_ref[...], w_ref[:, base:base + unit_w], preferred_element_type=F32)
        kinds = [(jj * upb + c < rope_units, unit_scales[jj * upb + c]) for jj in range(nj)]
        if all(k == kinds[0] for k in kinds):
            is_rope, scale = kinds[0]
            if is_rope:
                rotate(y, c, cos_ref[...] * scale, sin_ref[...] * scale)
            else:
                ys = (y if scale == 1.0 else y * scale).astype(o_ref.dtype)
                for sl in range(spu):
                    o_ref[c * spu + sl] = ys[:, sl * sw:(sl + 1) * sw]
        else:
            rope_here = jnp.bool_(False)
            scale = jnp.float32(1.0)
            for jj, (is_rope, sc) in enumerate(kinds):
                rope_here = jnp.where(j == jj, is_rope, rope_here)
                scale = jnp.where(j == jj, jnp.float32(sc), scale)
            rotate(y, c, jnp.where(rope_here, cos_ref[...], 1.0) * scale, jnp.where(rope_here, sin_ref[...], 0.0) * scale)


def _in_proj(x, g, w_bf16, cos, sin, *, seq, rope_units, unit_scales, slab_w, tm=1024, units_per_step=3):
    n, d = x.shape
    tm = min(tm, seq)
    ncol = w_bf16.shape[1]
    tn = units_per_step * d
    nj = ncol // tn
    assert len(unit_scales) == ncol // d
    rope_dim = 2 * cos.shape[1]
    nseq = seq // tm
    kern = functools.partial(_in_proj_kernel, nj=nj, unit_w=d, rope_units=rope_units, unit_scales=unit_scales,
                             rope_dim=rope_dim)
    return pl.pallas_call(
        kern,
        grid=(n // tm, ncol // tn),
        in_specs=[
            pl.BlockSpec((tm, d), lambda i, j: (i, 0)),
            pl.BlockSpec((1, d), lambda i, j: (0, 0)),
            pl.BlockSpec((d, tn), lambda i, j: (0, j)),
            pl.BlockSpec((tm, cos.shape[1]), lambda i, j: (i % nseq, 0)),
            pl.BlockSpec((tm, cos.shape[1]), lambda i, j: (i % nseq, 0)),
        ],
        out_specs=pl.BlockSpec((tn // slab_w, tm, slab_w), lambda i, j: (j, i, 0)),
        out_shape=jax.ShapeDtypeStruct((ncol // slab_w, n, slab_w), BF16),
        scratch_shapes=[pltpu.VMEM((tm, d), BF16)],
        compiler_params=_cparams(("parallel", "arbitrary")),
        name="in_proj",
    )(x, g.reshape(1, d), w_bf16, cos, sin)


def _retention_kernel(dl_ref, q_ref, k_ref, v_ref, g_ref, gnw_ref, o_ref, acc_ref, stf_ref, stb_ref, *, seq, chunk):
    h = pl.program_id(1)
    nc = seq // chunk
    dl = dl_ref[...]
    lg = jnp.minimum(dl, 0.0) - jnp.log1p(jnp.exp(-jnp.abs(dl)))
    col = lax.broadcasted_iota(I32, dl.shape, 1)
    lgh = jnp.sum(jnp.where(col == h, lg, 0.0), axis=1, keepdims=True)
    lgf = lgh[0:1, :]
    lgb = lgh[1:2, :]

    ri = lax.broadcasted_iota(I32, (chunk, chunk), 0)
    ci = lax.broadcasted_iota(I32, (chunk, chunk), 1)
    diff = (ri - ci).astype(F32)
    dmat = jnp.exp(jnp.where(diff >= 0, diff * lgf, -diff * lgb))
    pos = lax.broadcasted_iota(I32, (chunk, 1), 0).astype(F32)
    qdec_f = jnp.exp((pos + 1.0) * lgf)
    kdec_f = jnp.exp((chunk - 1.0 - pos) * lgf)
    qdec_b = jnp.exp((chunk - pos) * lgb)
    kdec_b = jnp.exp(pos * lgb)
    cdec_f = jnp.exp(chunk * lgf)
    cdec_b = jnp.exp(chunk * lgb)
    nt = (((1,), (1,)), ((), ()))
    tn = (((0,), (0,)), ((), ()))

    stf_ref[...] = jnp.zeros_like(stf_ref)
    stb_ref[...] = jnp.zeros_like(stb_ref)

    def chunk_slice(c):
        return pl.ds(pl.multiple_of(c * chunk, chunk), chunk)

    def scaled(x, dec):
        return (x.astype(F32) * dec).astype(BF16)

    def wide(ref, sl):
        return jnp.concatenate([ref[i, sl, :] for i in range(ref.shape[0])], axis=1)

    def step(t):
        slf = chunk_slice(t)
        slb = chunk_slice(nc - 1 - t)
        qf, kf, vf = q_ref[0, slf, :], k_ref[0, slf, :], wide(v_ref, slf)
        qb, kb, vb = q_ref[0, slb, :], k_ref[0, slb, :], wide(v_ref, slb)
        s = lax.dot_general(qf, kf, nt, preferred_element_type=F32) * dmat
        cross_b = jnp.dot(scaled(qb, qdec_b), stb_ref[...].astype(BF16), preferred_element_type=F32)
        inner = jnp.dot(s.astype(BF16), vf, preferred_element_type=F32)
        cross_f = jnp.dot(scaled(qf, qdec_f), stf_ref[...].astype(BF16), preferred_element_type=F32)
        stb_ref[...] = stb_ref[...] * cdec_b + lax.dot_general(scaled(kb, kdec_b), vb, tn, preferred_element_type=F32)
        stf_ref[...] = stf_ref[...] * cdec_f + lax.dot_general(scaled(kf, kdec_f), vf, tn, preferred_element_type=F32)
        return slf, inner + cross_f, slb, cross_b

    def finish(sl, o):
        mu = jnp.mean(o, axis=-1, keepdims=True)
        oc = o - mu
        var = jnp.mean(oc * oc, axis=-1, keepdims=True)
        on = oc * lax.rsqrt(var + NORM_EPS) * gnw_ref[...]
        gate = wide(g_ref, sl).astype(F32)
        o_ref[0, sl, :] = (gate * jax.nn.sigmoid(gate) * on).astype(o_ref.dtype)

    def first_half(t, carry):
        slf, of, slb, ob = step(t)
        acc_ref[slf, :] = of
        acc_ref[slb, :] = ob
        return carry

    def second_half(t, carry):
        slf, of, slb, ob = step(t)
        finish(slf, acc_ref[slf, :] + of)
        finish(slb, acc_ref[slb, :] + ob)
        return carry

    lax.fori_loop(0, nc // 2, first_half, 0)
    lax.fori_loop(nc // 2, nc, second_half, 0)


def _retention(proj, decay_logit, gn_w, *, batch, seq, d_model):
    heads = RET_HEADS
    dk = d_model // heads
    dv = 2 * d_model // heads
    n = batch * seq
    assert seq % (2 * RET_CHUNK) == 0, "the two scans are paired chunk by chunk"
    kern = functools.partial(_retention_kernel, seq=seq, chunk=RET_CHUNK)
    return pl.pallas_call(
        kern,
        grid=(batch, heads),
        in_specs=[
            pl.BlockSpec((2, heads), lambda b, h: (0, 0)),
            pl.BlockSpec((1, seq, dk), lambda b, h: (h, b, 0)),
            pl.BlockSpec((1, seq, dk), lambda b, h: (heads + h, b, 0)),
            pl.BlockSpec((dv // dk, seq, dk), lambda b, h: (heads + h, b, 0)),
            pl.BlockSpec((dv // dk, seq, dk), lambda b, h: (2 * heads + h, b, 0)),
            pl.BlockSpec((1, dv), lambda b, h: (0, h)),
        ],
        out_specs=pl.BlockSpec((1, seq, dv), lambda b, h: (h, b, 0)),
        out_shape=jax.ShapeDtypeStruct((heads, n, dv), BF16),
        scratch_shapes=[pltpu.VMEM((seq, dv), F32), pltpu.VMEM((dk, dv), F32), pltpu.VMEM((dk, dv), F32)],
        compiler_params=_cparams(("parallel", "parallel")),
        name="retention",
    )(decay_logit, proj, proj, proj, proj, gn_w.reshape(1, -1))


def _na_kernel(bias_ref, q_ref, k_ref, v_ref, o_ref, *, rows):
    gw = GRID_W
    kr = min(NA_MAX_ROWS, rows)
    dh = NA_HEAD_DIM
    lane_q = lax.broadcasted_iota(I32, (gw, 2 * dh), 1)
    nt = (((1,), (1,)), ((), ()))

    group = min(NA_ROW_UNROLL, rows)

    def body(it, carry):
        rr = [it * group + i for i in range(group)]
        rs = [jnp.clip(r - kr // 2, 0, rows - kr) for r in rr]
        s = []
        for r, r0 in zip(rr, rs):
            q2 = q_ref[0, pl.ds(pl.multiple_of(r * gw, gw), gw), :]
            zero = jnp.zeros_like(q2)
            qq = jnp.concatenate([jnp.where(lane_q < dh, q2, zero), jnp.where(lane_q >= dh, q2, zero)], axis=0)
            k2 = k_ref[0, pl.ds(pl.multiple_of(r0 * gw, gw), kr * gw), :]
            s.append(lax.dot_general(qq, k2, nt, preferred_element_type=F32) + bias_ref[0, r - r0])
        m = [jnp.max(si, axis=-1, keepdims=True) for si in s]
        p = [jnp.exp(si - mi) for si, mi in zip(s, m)]
        inv = [1.0 / jnp.sum(pi, axis=-1, keepdims=True) for pi in p]
        o = [jnp.dot(pi.astype(BF16), v_ref[0, pl.ds(pl.multiple_of(r0 * gw, gw), kr * gw), :],
                     preferred_element_type=F32) * ii for pi, ii, r0 in zip(p, inv, rs)]
        for r, oi in zip(rr, o):
            out = jnp.where(lane_q < dh, oi[0:gw, :], oi[gw:2 * gw, :])
            o_ref[0, pl.ds(pl.multiple_of(r * gw, gw), gw), :] = out.astype(o_ref.dtype)
        return carry

    lax.fori_loop(0, rows // group, body, 0)


def _na_bias_table(rpb, rows):
    kr = min(NA_MAX_ROWS, rows)
    c = np.arange(GRID_W)
    kc = np.arange(GRID_W)
    win_start = np.clip(c - NA_WIN_COLS // 2, 0, GRID_W - NA_WIN_COLS)
    valid = (kc[None, :] >= win_start[:, None]) & (kc[None, :] < win_start[:, None] + NA_WIN_COLS)
    dc_idx = np.clip(kc[None, :] - c[:, None] + NA_WIN_COLS - 1, 0, 2 * NA_WIN_COLS - 2)
    delta = np.arange(kr)
    a = np.arange(kr)
    dr_idx = a[None, :] - delta[:, None] + NA_MAX_ROWS - 1
    row_sel = (dr_idx[:, :, None] == np.arange(rpb.shape[1])).astype(np.float32)
    col_sel = (dc_idx[None, :, :] == np.arange(rpb.shape[2])[:, None, None]).astype(np.float32)
    t = jnp.einsum("dar,hrs,sck->hdcak", row_sel, rpb.astype(F32), col_sel, precision=lax.Precision.HIGHEST)
    t = jnp.where(jnp.asarray(valid)[None, None, :, None, :], t, NEG_BIG)
    h = rpb.shape[0]
    t = t.reshape(h // 2, 2, kr, GRID_W, kr * GRID_W).transpose(0, 2, 1, 3, 4)
    return t.reshape(h // 2, kr, 2 * GRID_W, kr * GRID_W)


def _na_attention(qkv, bias, *, batch, seq, d_model):
    n = batch * seq
    rows = seq // GRID_W
    kr = min(NA_MAX_ROWS, rows)
    pairs = NA_HEADS // 2
    pw = 2 * NA_HEAD_DIM
    assert rows % min(NA_ROW_UNROLL, rows) == 0
    kern = functools.partial(_na_kernel, rows=rows)
    return pl.pallas_call(
        kern,
        grid=(batch, pairs),
        in_specs=[
            pl.BlockSpec((1, kr, 2 * GRID_W, kr * GRID_W), lambda b, hp: (hp, 0, 0, 0)),
            pl.BlockSpec((1, seq, pw), lambda b, hp: (hp, b, 0)),
            pl.BlockSpec((1, seq, pw), lambda b, hp: (pairs + hp, b, 0)),
            pl.BlockSpec((1, seq, pw), lambda b, hp: (2 * pairs + hp, b, 0)),
        ],
        out_specs=pl.BlockSpec((1, seq, pw), lambda b, hp: (hp, b, 0)),
        out_shape=jax.ShapeDtypeStruct((pairs, n, pw), BF16),
        compiler_params=_cparams(("parallel", "parallel")),
        name="na_attention",
    )(bias, qkv, qkv, qkv)


def _out_router_kernel(a_ref, w_ref, h_ref, g_ref, rw_ref, h1_ref, xn_ref, aff_ref):
    a = jnp.concatenate([a_ref[i] for i in range(a_ref.shape[0])], axis=1)
    y = jnp.dot(a, w_ref[...], preferred_element_type=F32) + h_ref[...]
    h1_ref[...] = y
    xn = _rms(y, g_ref[...])
    xn_ref[...] = xn.astype(xn_ref.dtype)
    nt = (((1,), (1,)), ((), ()))
    ne = rw_ref.shape[0]
    xh = xn.astype(BF16)
    xl = (xn - xh.astype(F32)).astype(BF16)
    rw = rw_ref[...]
    rh = rw.astype(BF16)
    rl = (rw - rh.astype(F32)).astype(BF16)
    t1 = lax.dot_general(jnp.concatenate([rh, rl], axis=0), xh, nt, preferred_element_type=F32)
    t2 = lax.dot_general(rh, xl, nt, preferred_element_type=F32)
    logits = t1[0:ne, :] + (t1[ne:2 * ne, :] + t2)
    m = jnp.max(logits, axis=0, keepdims=True)
    e = jnp.exp(logits - m)
    aff = e / jnp.sum(e, axis=0, keepdims=True)
    for t in range(aff_ref.shape[0]):
        aff_ref[t] = aff[:, t * MOE_TILE:(t + 1) * MOE_TILE]


def _out_router(a, w_bf16, h, g, router_w, *, tm=1024):
    nslab, n, sw = a.shape
    kdim = nslab * sw
    tm = min(tm, n)
    d = h.shape[1]
    ne = router_w.shape[1]
    tpb = tm // MOE_TILE
    return pl.pallas_call(
        _out_router_kernel,
        grid=(n // tm,),
        in_specs=[
            pl.BlockSpec((nslab, tm, sw), lambda i: (0, i, 0)),
            pl.BlockSpec((kdim, d), lambda i: (0, 0)),
            pl.BlockSpec((tm, d), lambda i: (i, 0)),
            pl.BlockSpec((1, d), lambda i: (0, 0)),
            pl.BlockSpec((ne, d), lambda i: (0, 0)),
        ],
        out_specs=[
            pl.BlockSpec((tm, d), lambda i: (i, 0)),
            pl.BlockSpec((tm, d), lambda i: (i, 0)),
            pl.BlockSpec((tpb, ne, MOE_TILE), lambda i: (i, 0, 0)),
        ],
        out_shape=[
            jax.ShapeDtypeStruct((n, d), F32),
            jax.ShapeDtypeStruct((n, d), BF16),
            jax.ShapeDtypeStruct((n // MOE_TILE, ne, MOE_TILE), F32),
        ],
        compiler_params=_cparams(("parallel",)),
        name="out_router",
    )(a, w_bf16, h, g.reshape(1, d), router_w.T)


def _select_kernel(aff_ref, pos_ref, off_ref, *, cap):
    ntile, ne, tt = aff_ref.shape

    def count_ge(bits):
        level = pltpu.bitcast(bits, F32)

        def body(c, acc):
            return acc + (aff_ref[c] >= level).astype(I32)
        acc = lax.fori_loop(0, ntile, body, jnp.zeros((ne, tt), I32), unroll=4)
        return jnp.sum(acc, axis=1, keepdims=True)

    def bit_step(i, t):
        cand = t | jnp.left_shift(jnp.int32(1), 30 - i)
        return jnp.where(count_ge(cand) >= cap, cand, t)

    thr_bits = lax.fori_loop(0, 31, bit_step, jnp.zeros((ne, 1), I32))
    thr = pltpu.bitcast(thr_bits, F32)
    above = pltpu.bitcast(thr_bits + 1, F32)
    need_eq = (cap - count_ge(thr_bits + 1)).astype(F32)

    li = lax.broadcasted_iota(I32, (tt, tt), 0)
    lj = lax.broadcasted_iota(I32, (tt, tt), 1)
    upper = (li < lj).astype(BF16)

    def tile_body(c, carry):
        eq_carry, pos_carry = carry
        off_ref[c] = jnp.broadcast_to(pos_carry, (ne, LANES)).astype(I32)
        x = aff_ref[c]
        gt = x >= above
        eq = (x >= thr) & jnp.logical_not(gt)
        gtf = gt.astype(F32)
        eqf = eq.astype(F32)
        pre = jnp.dot(jnp.concatenate([gtf, eqf], axis=0).astype(BF16), upper, preferred_element_type=F32)
        eq_rank = eq_carry + pre[ne:2 * ne, :]
        taken = eq & (eq_rank < need_eq)
        taken_before = jnp.minimum(eq_rank, need_eq) - jnp.minimum(eq_carry, need_eq)
        pos = pos_carry + pre[0:ne, :] + taken_before
        pos_ref[c] = jnp.where(gt | taken, pos.astype(I32), -1)
        eq_next = eq_carry + jnp.sum(eqf, axis=1, keepdims=True)
        taken_total = jnp.minimum(eq_next, need_eq) - jnp.minimum(eq_carry, need_eq)
        return eq_next, pos_carry + jnp.sum(gtf, axis=1, keepdims=True) + taken_total

    lax.fori_loop(0, ntile, tile_body, (jnp.zeros((ne, 1), F32), jnp.zeros((ne, 1), F32)), unroll=2)


def _select(aff3, *, cap):
    ntile, ne, tt = aff3.shape
    kern = functools.partial(_select_kernel, cap=cap)
    pos3, off3 = pl.pallas_call(
        kern,
        out_shape=[
            jax.ShapeDtypeStruct((ntile, ne, tt), I32),
            jax.ShapeDtypeStruct((ntile, ne, LANES), I32),
        ],
        compiler_params=pltpu.CompilerParams(vmem_limit_bytes=_VMEM_LIMIT),
        name="ec_select",
    )(aff3)
    offs = jnp.concatenate([off3[:, :, 0], jnp.full((1, ne), cap, I32)], axis=0).reshape(-1)
    return pos3, offs


def _tile_windows(off_ref, s, ne):
    starts, used = [], []
    for e in range(ne):
        off = off_ref[s * ne + e]
        nxt = off_ref[(s + 1) * ne + e]
        st = (off // BF16_ROWS) * BF16_ROWS
        starts.append(st)
        used.append(nxt - st)
    return starts, used


def _num_rounds(used):
    m = used[0]
    for u in used[1:]:
        m = jnp.maximum(m, u)
    return (m + MOE_SLOTS - 1) // MOE_SLOTS


def _dispatch_kernel(off_ref, x_ref, pos_ref, xe_ref, wins_ref, carry_ref, sem, nout_ref, *, cap):
    s = pl.program_id(0)
    nsub, ne, tt = pos_ref.shape
    gr = BF16_ROWS
    gpr = MOE_SLOTS // gr

    @pl.when(s == 0)
    def _():
        carry_ref[...] = jnp.zeros_like(carry_ref)
        nout_ref[0] = 0
        nout_ref[1] = 0

    slot = lax.broadcasted_iota(I32, (MOE_SLOTS, tt), 0)

    def drain(b):
        def wait_one(i, c):
            pltpu.make_async_copy(wins_ref.at[b, pl.ds(0, gr)], xe_ref.at[pl.ds(0, gr)], sem.at[b]).wait()
            return c
        lax.fori_loop(0, nout_ref[b], wait_one, 0)
        nout_ref[b] = 0

    for k in range(nsub):
        tile = s * nsub + k
        starts, used = _tile_windows(off_ref, tile, ne)
        pos = pos_ref[k]
        x = x_ref[pl.ds(k * tt, tt), :]

        def round_body(w, c, tile=tile, starts=starts, used=used, pos=pos, x=x):
            b = (tile + w) % 2
            drain(b)
            pieces = []
            for e in range(ne):
                key = pos[e:e + 1, :] - (starts[e] + w * MOE_SLOTS)
                pieces.append((jnp.broadcast_to(key, (MOE_SLOTS, tt)) == slot).astype(F32))
            onehot = jnp.concatenate(pieces, axis=0).astype(BF16)
            wins_ref[b] = jnp.dot(onehot, x, preferred_element_type=F32).astype(BF16)
            first = w == 0
            issued = 0
            for e in range(ne):
                head = pl.ds(e * MOE_SLOTS, gr)
                crows = pl.ds(e * gr, gr)
                carry = carry_ref[crows, :]
                merged = (wins_ref[b, head, :].astype(F32) + carry.astype(F32)).astype(BF16)
                wins_ref[b, head, :] = jnp.where(first, merged, wins_ref[b, head, :])

                ngc = used[e] // gr
                rem = used[e] % gr
                ng_w = jnp.clip(ngc - w * gpr, 0, gpr)

                def issue(j, c2, e=e):
                    src = wins_ref.at[b, pl.ds(e * MOE_SLOTS + j * gr, gr)]
                    dst = xe_ref.at[pl.ds(pl.multiple_of(e * cap + starts[e] + w * MOE_SLOTS + j * gr, gr), gr)]
                    pltpu.make_async_copy(src, dst, sem.at[b]).start()
                    return c2

                lax.fori_loop(0, ng_w, issue, 0)
                issued = issued + ng_w

                part = wins_ref[b, pl.ds(e * MOE_SLOTS + jnp.clip(ngc - w * gpr, 0, gpr - 1) * gr, gr), :]
                keep_part = (rem > 0) & (ngc // gpr == w)
                clear = (rem == 0) & first
                carry_ref[crows, :] = jnp.where(keep_part, part, jnp.where(clear, jnp.zeros_like(carry), carry))

            nout_ref[b] = issued
            return c

        lax.fori_loop(0, _num_rounds(used), round_body, 0)

    @pl.when(s == pl.num_programs(0) - 1)
    def _():
        drain(0)
        drain(1)


def _dispatch(offs, xn, pos3, *, cap):
    n, d = xn.shape
    ntile, ne, tt = pos3.shape
    nsub = min(DISPATCH_TILES_PER_STEP, ntile)
    assert ntile % nsub == 0
    kern = functools.partial(_dispatch_kernel, cap=cap)
    return pl.pallas_call(
        kern,
        grid_spec=pltpu.PrefetchScalarGridSpec(
            num_scalar_prefetch=1,
            grid=(ntile // nsub,),
            in_specs=[
                pl.BlockSpec((nsub * tt, d), lambda i, off: (i, 0)),
                pl.BlockSpec((nsub, ne, tt), lambda i, off: (i, 0, 0)),
            ],
            out_specs=pl.BlockSpec(memory_space=pl.ANY),
            scratch_shapes=[
                pltpu.VMEM((2, ne * MOE_SLOTS, d), BF16),
                pltpu.VMEM((ne * BF16_ROWS, d), BF16),
                pltpu.SemaphoreType.DMA((2,)),
                pltpu.SMEM((2,), I32),
            ],
        ),
        out_shape=jax.ShapeDtypeStruct((ne * cap, d), BF16),
        compiler_params=_cparams(("arbitrary",)),
        name="ec_dispatch",
    )(offs, xn, pos3)


def _ffn_kernel(x_ref, wg_ref, wu_ref, wd_ref, o_ref):
    x = x_ref[...]
    a = jnp.dot(x, wg_ref[0, 0], preferred_element_type=F32)
    u = jnp.dot(x, wu_ref[0, 0], preferred_element_type=F32)
    hid = (a * jax.nn.sigmoid(a) * u).astype(BF16)
    o_ref[...] = jnp.dot(hid, wd_ref[0, 0], preferred_element_type=F32).astype(o_ref.dtype)


def _expert_ffn(xe, wg, wu, wd, *, layer, cap, tr=1024):
    _, ne, d, f = wg.shape
    tr = min(tr, cap)
    nt = cap // tr
    return pl.pallas_call(
        _ffn_kernel,
        grid=(ne, nt),
        in_specs=[
            pl.BlockSpec((tr, d), lambda e, t: (e * nt + t, 0)),
            pl.BlockSpec((1, 1, d, f), lambda e, t: (layer, e, 0, 0)),
            pl.BlockSpec((1, 1, d, f), lambda e, t: (layer, e, 0, 0)),
            pl.BlockSpec((1, 1, f, d), lambda e, t: (layer, e, 0, 0)),
        ],
        out_specs=pl.BlockSpec((tr, d), lambda e, t: (e * nt + t, 0)),
        out_shape=jax.ShapeDtypeStruct((ne * cap, d), BF16),
        compiler_params=_cparams(("parallel", "parallel")),
        name="expert_ffn",
    )(xe, wg, wu, wd)


def _combine_kernel(off_ref, h_ref, aff_ref, pos_ref, p_ref, ye_ref, png_ref, wpg_ref, wpu_ref, fng_ref,
                    o_ref, wins_ref, acc_ref, sem, *, cap, final_norm):
    s = pl.program_id(0)
    ntile = pl.num_programs(0)
    nsub, ne, tt = pos_ref.shape
    spill_buf = 2 * nsub

    def window_copies(starts_t, w, buf):
        copies, wstarts = [], []
        for e in range(ne):
            ws = pl.multiple_of(jnp.minimum(starts_t[e] + w * MOE_SLOTS, cap - MOE_SLOTS), BF16_ROWS)
            copies.append(pltpu.make_async_copy(
                ye_ref.at[pl.ds(pl.multiple_of(e * cap + ws, BF16_ROWS), MOE_SLOTS)],
                wins_ref.at[buf, pl.ds(e * MOE_SLOTS, MOE_SLOTS)], sem.at[buf]))
            wstarts.append(ws)
        return copies, wstarts

    cur = s % 2
    tiles = [_tile_windows(off_ref, s * nsub + k, ne) for k in range(nsub)]

    @pl.when(s == 0)
    def _():
        for k in range(nsub):
            for cp in window_copies(tiles[k][0], 0, cur * nsub + k)[0]:
                cp.start()

    @pl.when(s + 1 < ntile)
    def _():
        for k in range(nsub):
            nxt_starts, _ = _tile_windows(off_ref, (s + 1) * nsub + k, ne)
            for cp in window_copies(nxt_starts, 0, (1 - cur) * nsub + k)[0]:
                cp.start()

    slot = lax.broadcasted_iota(I32, (MOE_SLOTS, tt), 0)
    tn = (((0,), (0,)), ((), ()))

    def gates(k, w, wstarts):
        starts = tiles[k][0]
        pos = pos_ref[k]
        aff = aff_ref[k]
        pieces = []
        for e in range(ne):
            base = starts[e] + w * MOE_SLOTS
            pe = pos[e:e + 1, :]
            in_round = (pe >= base) & (pe < base + MOE_SLOTS)
            hit = (jnp.broadcast_to(pe - wstarts[e], (MOE_SLOTS, tt)) == slot) & jnp.broadcast_to(in_round, (MOE_SLOTS, tt))
            pieces.append(jnp.where(hit, jnp.broadcast_to(aff[e:e + 1, :], (MOE_SLOTS, tt)), 0.0))
        return jnp.concatenate(pieces, axis=0).astype(BF16)

    for k in range(nsub):
        starts, used = tiles[k]
        rows = pl.ds(k * tt, tt)
        buf0 = cur * nsub + k
        copies0, wstarts0 = window_copies(starts, 0, buf0)
        a0 = gates(k, 0, wstarts0)
        for cp in copies0:
            cp.wait()
        acc_ref[rows, :] = lax.dot_general(a0, wins_ref[buf0], tn, preferred_element_type=F32)

        def round_body(w, c, k=k, starts=starts, rows=rows):
            copies, wstarts = window_copies(starts, w, spill_buf)
            for cp in copies:
                cp.start()
            a_w = gates(k, w, wstarts)
            for cp in copies:
                cp.wait()
            acc_ref[rows, :] += lax.dot_general(a_w, wins_ref[spill_buf], tn, preferred_element_type=F32)
            return c

        lax.fori_loop(1, _num_rounds(used), round_body, 0)
    h2 = h_ref[...] + acc_ref[...]
    gate = jax.nn.sigmoid(jnp.dot(_rms(h2, png_ref[...]).astype(BF16), wpg_ref[...], preferred_element_type=F32))
    up = jnp.dot(p_ref[0].astype(BF16), wpu_ref[...], preferred_element_type=F32)
    h3 = h2 + up * gate
    if final_norm:
        h3 = _rms(h3, fng_ref[...])
    o_ref[...] = h3


def _combine(offs, h1, aff3, pos3, p_all, ye, ple_norm, wpg, wpu, final_g, *, layer, cap, final_norm):
    n, d = h1.shape
    ntile, ne, tt = pos3.shape
    pd = p_all.shape[2]
    nsub = min(COMBINE_TILES_PER_STEP, ntile)
    assert ntile % nsub == 0
    rows = nsub * tt
    nbuf = 2 * nsub + 1
    kern = functools.partial(_combine_kernel, cap=cap, final_norm=final_norm)
    return pl.pallas_call(
        kern,
        grid_spec=pltpu.PrefetchScalarGridSpec(
            num_scalar_prefetch=1,
            grid=(ntile // nsub,),
            in_specs=[
                pl.BlockSpec((rows, d), lambda i, off: (i, 0)),
                pl.BlockSpec((nsub, ne, tt), lambda i, off: (i, 0, 0)),
                pl.BlockSpec((nsub, ne, tt), lambda i, off: (i, 0, 0)),
                pl.BlockSpec((1, rows, pd), lambda i, off: (layer, i, 0)),
                pl.BlockSpec(memory_space=pl.ANY),
                pl.BlockSpec((1, d), lambda i, off: (0, 0)),
                pl.BlockSpec((d, d), lambda i, off: (0, 0)),
                pl.BlockSpec((pd, d), lambda i, off: (0, 0)),
                pl.BlockSpec((1, d), lambda i, off: (0, 0)),
            ],
            out_specs=pl.BlockSpec((rows, d), lambda i, off: (i, 0)),
            scratch_shapes=[pltpu.VMEM((nbuf, ne * MOE_SLOTS, d), BF16), pltpu.VMEM((rows, d), F32),
                            pltpu.SemaphoreType.DMA((nbuf,))],
        ),
        out_shape=jax.ShapeDtypeStruct((n, d), F32),
        compiler_params=_cparams(("arbitrary",)),
        name="combine_ple",
    )(offs, h1, aff3, pos3, p_all, ye, ple_norm.reshape(1, d), wpg, wpu, final_g.reshape(1, d))


def _moe_block(h1, xn, aff3, p_all, w, i, *, final_norm):
    n, d = h1.shape
    cap = EC_CAPACITY * n // N_EXPERTS
    pos3, offs = _select(aff3, cap=cap)
    xe = _dispatch(offs, xn, pos3, cap=cap)
    ye = _expert_ffn(xe, w["exp_w_gate"], w["exp_w_up"], w["exp_w_down"], layer=i, cap=cap)
    return _combine(offs, h1, aff3, pos3, p_all, ye, w["ple_norm"][i], w["ple_w_gate"][i], w["ple_w_up"][i],
                    w["final_norm"], layer=i, cap=cap, final_norm=final_norm)


def _rope_tables(seq, dim):
    inv = ROPE_THETA ** (-jnp.arange(0, dim, 2, dtype=F32) / dim)
    ang = jnp.arange(seq, dtype=F32)[:, None] * inv[None, :]
    return jnp.cos(ang), jnp.sin(ang)


def _trunk(x, p, w):
    batch, seq, d = x.shape
    n = batch * seq
    h = x.reshape(n, d)
    dk = d // RET_HEADS
    cos, sin = _rope_tables(seq, dk)

    proj = _in_proj(h, w["mix_norm"][0], w["ret_w_in"][0], cos, sin, seq=seq, rope_units=2,
                    unit_scales=(1.0, float(dk) ** -0.5, 1.0, 1.0, 1.0, 1.0), slab_w=dk)
    a = _retention(proj, w["ret_decay_logit"][0], w["ret_gn_w"][0], batch=batch, seq=seq, d_model=d)
    h1, xn, aff3 = _out_router(a, w["ret_w_out"][0], h, w["ffn_norm"][0], w["router_w"][0])
    p_all = p.reshape(p.shape[0], n, p.shape[-1])
    h = _moe_block(h1, xn, aff3, p_all, w, 0, final_norm=False)

    qkv = _in_proj(h, w["mix_norm"][1], w["na_w_in"][0], cos, sin, seq=seq, rope_units=0,
                   unit_scales=(float(NA_HEAD_DIM) ** -0.5, 1.0, 1.0), slab_w=2 * NA_HEAD_DIM)
    a = _na_attention(qkv, _na_bias_table(w["na_rpb"][0], seq // GRID_W), batch=batch, seq=seq, d_model=d)
    h1, xn, aff3 = _out_router(a, w["na_w_out"][0], h, w["ffn_norm"][1], w["router_w"][1])
    y = _moe_block(h1, xn, aff3, p_all, w, 1, final_norm=True)
    return y.reshape(batch, seq, d)


def kernel(x_prompt, x_sample, p_prompt, p_sample, ret_w_in, ret_decay_logit, ret_gn_w, ret_w_out, na_w_in, na_rpb, na_w_out, mix_norm, ffn_norm, ple_norm, router_w, exp_w_gate, exp_w_up, exp_w_down, ple_w_up, ple_w_gate, final_norm):
    w = dict(
        ret_w_in=ret_w_in.astype(BF16), ret_decay_logit=ret_decay_logit, ret_gn_w=ret_gn_w,
        ret_w_out=ret_w_out.astype(BF16), na_w_in=na_w_in.astype(BF16), na_rpb=na_rpb,
        na_w_out=na_w_out.astype(BF16), mix_norm=mix_norm, ffn_norm=ffn_norm, ple_norm=ple_norm,
        router_w=router_w, exp_w_gate=exp_w_gate.astype(BF16), exp_w_up=exp_w_up.astype(BF16),
        exp_w_down=exp_w_down.astype(BF16), ple_w_up=ple_w_up.astype(BF16), ple_w_gate=ple_w_gate.astype(BF16),
        final_norm=final_norm,
    )
    return _trunk(x_prompt, p_prompt, w), _trunk(x_sample, p_sample, w)
```

```python
import functools

import numpy as np
import jax
import jax.numpy as jnp
from jax import lax
from jax.experimental import pallas as pl
from jax.experimental.pallas import tpu as pltpu

F32 = jnp.float32
BF16 = jnp.bfloat16
I32 = jnp.int32

NORM_EPS = 1e-6
ROPE_THETA = 10000.0
GRID_W = 64
RET_HEADS = 4
RET_CHUNK = 256
NA_HEADS = 16
NA_HEAD_DIM = 64
NA_MAX_ROWS = 8
NA_WIN_COLS = 16
NA_ROW_UNROLL = 32
N_EXPERTS = 16
EC_CAPACITY = 2
LANES = 128
BF16_ROWS = 16
MOE_TILE = 256
MOE_SLOTS = 64
DISPATCH_TILES_PER_STEP = 4
COMBINE_TILES_PER_STEP = 2
NEG_BIG = -1e30

V7X_VMEM_BYTES = 64 * 1024 * 1024
_VMEM_LIMIT = V7X_VMEM_BYTES * 7 // 8


def _cparams(sem):
    return pltpu.CompilerParams(dimension_semantics=sem, vmem_limit_bytes=_VMEM_LIMIT)


def _rms(x, g):
    return x * lax.rsqrt(jnp.mean(x * x, axis=-1, keepdims=True) + NORM_EPS) * g


def _in_proj_kernel(x_ref, g_ref, w_ref, cos_ref, sin_ref, o_ref, xn_ref, *, nj, unit_w, rope_units, unit_scales,
                    rope_dim):
    j = pl.program_id(1)

    @pl.when(j == 0)
    def _():
        xn_ref[...] = _rms(x_ref[...], g_ref[...]).astype(BF16)

    upb = w_ref.shape[1] // unit_w
    sw = o_ref.shape[2]
    spu = unit_w // sw
    half = rope_dim // 2

    def rotate(y, c, cos, sin):
        assert sw == rope_dim
        for hh in range(spu):
            lo = hh * rope_dim
            x1 = y[:, lo:lo + half]
            x2 = y[:, lo + half:lo + rope_dim]
            o_ref[c * spu + hh, :, 0:half] = (x1 * cos - x2 * sin).astype(o_ref.dtype)
            o_ref[c * spu + hh, :, half:rope_dim] = (x1 * sin + x2 * cos).astype(o_ref.dtype)

    for c in range(upb):
        base = c * unit_w
        y = jnp.dot(xn_ref[...], w_ref[:, base:base + unit_w], preferred_element_type=F32)
        kinds = [(jj * upb + c < rope_units, unit_scales[jj * upb + c]) for jj in range(nj)]
        if all(k == kinds[0] for k in kinds):
            is_rope, scale = kinds[0]
            if is_rope:
                rotate(y, c, cos_ref[...] * scale, sin_ref[...] * scale)
            else:
                ys = (y if scale == 1.0 else y * scale).astype(o_ref.dtype)
                for sl in range(spu):
                    o_ref[c * spu + sl] = ys[:, sl * sw:(sl + 1) * sw]
        else:
            rope_here = jnp.bool_(False)
            scale = jnp.float32(1.0)
            for jj, (is_rope, sc) in enumerate(kinds):
                rope_here = jnp.where(j == jj, is_rope, rope_here)
                scale = jnp.where(j == jj, jnp.float32(sc), scale)
            rotate(y, c, jnp.where(rope_here, cos_ref[...], 1.0) * scale, jnp.where(rope_here, sin_ref[...], 0.0) * scale)


def _in_proj(x, g, w_bf16, cos, sin, *, seq, rope_units, unit_scales, slab_w, tm=1024, units_per_step=3):
    n, d = x.shape
    tm = min(tm, seq)
    ncol = w_bf16.shape[1]
    tn = units_per_step * d
    nj = ncol // tn
    assert len(unit_scales) == ncol // d
    rope_dim = 2 * cos.shape[1]
    nseq = seq // tm
    kern = functools.partial(_in_proj_kernel, nj=nj, unit_w=d, rope_units=rope_units, unit_scales=unit_scales,
                             rope_dim=rope_dim)
    return pl.pallas_call(
        kern,
        grid=(n // tm, ncol // tn),
        in_specs=[
            pl.BlockSpec((tm, d), lambda i, j: (i, 0)),
            pl.BlockSpec((1, d), lambda i, j: (0, 0)),
            pl.BlockSpec((d, tn), lambda i, j: (0, j)),
            pl.BlockSpec((tm, cos.shape[1]), lambda i, j: (i % nseq, 0)),
            pl.BlockSpec((tm, cos.shape[1]), lambda i, j: (i % nseq, 0)),
        ],
        out_specs=pl.BlockSpec((tn // slab_w, tm, slab_w), lambda i, j: (j, i, 0)),
        out_shape=jax.ShapeDtypeStruct((ncol // slab_w, n, slab_w), BF16),
        scratch_shapes=[pltpu.VMEM((tm, d), BF16)],
        compiler_params=_cparams(("parallel", "arbitrary")),
        name="in_proj",
    )(x, g.reshape(1, d), w_bf16, cos, sin)


def _retention_kernel(dl_ref, q_ref, k_ref, v_ref, g_ref, gnw_ref, o_ref, acc_ref, stf_ref, stb_ref, *, seq, chunk):
    h = pl.program_id(1)
    nc = seq // chunk
    dl = dl_ref[...]
    lg = jnp.minimum(dl, 0.0) - jnp.log1p(jnp.exp(-jnp.abs(dl)))
    col = lax.broadcasted_iota(I32, dl.shape, 1)
    lgh = jnp.sum(jnp.where(col == h, lg, 0.0), axis=1, keepdims=True)
    lgf = lgh[0:1, :]
    lgb = lgh[1:2, :]

    ri = lax.broadcasted_iota(I32, (chunk, chunk), 0)
    ci = lax.broadcasted_iota(I32, (chunk, chunk), 1)
    diff = (ri - ci).astype(F32)
    dmat = jnp.exp(jnp.where(diff >= 0, diff * lgf, -diff * lgb))
    pos = lax.broadcasted_iota(I32, (chunk, 1), 0).astype(F32)
    qdec_f = jnp.exp((pos + 1.0) * lgf)
    kdec_f = jnp.exp((chunk - 1.0 - pos) * lgf)
    qdec_b = jnp.exp((chunk - pos) * lgb)
    kdec_b = jnp.exp(pos * lgb)
    cdec_f = jnp.exp(chunk * lgf)
    cdec_b = jnp.exp(chunk * lgb)
    nt = (((1,), (1,)), ((), ()))
    tn = (((0,), (0,)), ((), ()))

    stf_ref[...] = jnp.zeros_like(stf_ref)
    stb_ref[...] = jnp.zeros_like(stb_ref)

    def chunk_slice(c):
        return pl.ds(pl.multiple_of(c * chunk, chunk), chunk)

    def scaled(x, dec):
        return (x.astype(F32) * dec).astype(BF16)

    def wide(ref, sl):
        return jnp.concatenate([ref[i, sl, :] for i in range(ref.shape[0])], axis=1)

    def step(t):
        slf = chunk_slice(t)
        slb = chunk_slice(nc - 1 - t)
        qf, kf, vf = q_ref[0, slf, :], k_ref[0, slf, :], wide(v_ref, slf)
        qb, kb, vb = q_ref[0, slb, :], k_ref[0, slb, :], wide(v_ref, slb)
        s = lax.dot_general(qf, kf, nt, preferred_element_type=F32) * dmat
        cross_b = jnp.dot(scaled(qb, qdec_b), stb_ref[...].astype(BF16), preferred_element_type=F32)
        inner = jnp.dot(s.astype(BF16), vf, preferred_element_type=F32)
        cross_f = jnp.dot(scaled(qf, qdec_f), stf_ref[...].astype(BF16), preferred_element_type=F32)
        stb_ref[...] = stb_ref[...] * cdec_b + lax.dot_general(scaled(kb, kdec_b), vb, tn, preferred_element_type=F32)
        stf_ref[...] = stf_ref[...] * cdec_f + lax.dot_general(scaled(kf, kdec_f), vf, tn, preferred_element_type=F32)
        return slf, inner + cross_f, slb, cross_b

    def finish(sl, o):
        mu = jnp.mean(o, axis=-1, keepdims=True)
        oc = o - mu
        var = jnp.mean(oc * oc, axis=-1, keepdims=True)
        on = oc * lax.rsqrt(var + NORM_EPS) * gnw_ref[...]
        gate = wide(g_ref, sl).astype(F32)
        o_ref[0, sl, :] = (gate * jax.nn.sigmoid(gate) * on).astype(o_ref.dtype)

    def first_half(t, carry):
        slf, of, slb, ob = step(t)
        acc_ref[slf, :] = of
        acc_ref[slb, :] = ob
        return carry

    def second_half(t, carry):
        slf, of, slb, ob = step(t)
        finish(slf, acc_ref[slf, :] + of)
        finish(slb, acc_ref[slb, :] + ob)
        return carry

    lax.fori_loop(0, nc // 2, first_half, 0, unroll=4)
    lax.fori_loop(nc // 2, nc, second_half, 0, unroll=4)


def _retention(proj, decay_logit, gn_w, *, batch, seq, d_model):
    heads = RET_HEADS
    dk = d_model // heads
    dv = 2 * d_model // heads
    n = batch * seq
    assert seq % (2 * RET_CHUNK) == 0, "the two scans are paired chunk by chunk"
    kern = functools.partial(_retention_kernel, seq=seq, chunk=RET_CHUNK)
    return pl.pallas_call(
        kern,
        grid=(batch, heads),
        in_specs=[
            pl.BlockSpec((2, heads), lambda b, h: (0, 0)),
            pl.BlockSpec((1, seq, dk), lambda b, h: (h, b, 0)),
            pl.BlockSpec((1, seq, dk), lambda b, h: (heads + h, b, 0)),
            pl.BlockSpec((dv // dk, seq, dk), lambda b, h: (heads + h, b, 0)),
            pl.BlockSpec((dv // dk, seq, dk), lambda b, h: (2 * heads + h, b, 0)),
            pl.BlockSpec((1, dv), lambda b, h: (0, h)),
        ],
        out_specs=pl.BlockSpec((1, seq, dv), lambda b, h: (h, b, 0)),
        out_shape=jax.ShapeDtypeStruct((heads, n, dv), BF16),
        scratch_shapes=[pltpu.VMEM((seq, dv), F32), pltpu.VMEM((dk, dv), F32), pltpu.VMEM((dk, dv), F32)],
        compiler_params=_cparams(("parallel", "parallel")),
        name="retention",
    )(decay_logit, proj, proj, proj, proj, gn_w.reshape(1, -1))


def _na_kernel(bias_ref, q_ref, k_ref, v_ref, o_ref, *, rows):
    gw = GRID_W
    kr = min(NA_MAX_ROWS, rows)
    dh = NA_HEAD_DIM
    lane_q = lax.broadcasted_iota(I32, (gw, 2 * dh), 1)
    nt = (((1,), (1,)), ((), ()))

    group = min(NA_ROW_UNROLL, rows)

    def body(it, carry):
        rr = [it * group + i for i in range(group)]
        rs = [jnp.clip(r - kr // 2, 0, rows - kr) for r in rr]
        s = []
        for r, r0 in zip(rr, rs):
            q2 = q_ref[0, pl.ds(pl.multiple_of(r * gw, gw), gw), :]
            zero = jnp.zeros_like(q2)
            qq = jnp.concatenate([jnp.where(lane_q < dh, q2, zero), jnp.where(lane_q >= dh, q2, zero)], axis=0)
            k2 = k_ref[0, pl.ds(pl.multiple_of(r0 * gw, gw), kr * gw), :]
            s.append(lax.dot_general(qq, k2, nt, preferred_element_type=F32) + bias_ref[0, r - r0])
        m = [jnp.max(si, axis=-1, keepdims=True) for si in s]
        p = [jnp.exp(si - mi) for si, mi in zip(s, m)]
        inv = [1.0 / jnp.sum(pi, axis=-1, keepdims=True) for pi in p]
        o = [jnp.dot(pi.astype(BF16), v_ref[0, pl.ds(pl.multiple_of(r0 * gw, gw), kr * gw), :],
                     preferred_element_type=F32) * ii for pi, ii, r0 in zip(p, inv, rs)]
        for r, oi in zip(rr, o):
            out = jnp.where(lane_q < dh, oi[0:gw, :], oi[gw:2 * gw, :])
            o_ref[0, pl.ds(pl.multiple_of(r * gw, gw), gw), :] = out.astype(o_ref.dtype)
        return carry

    lax.fori_loop(0, rows // group, body, 0)


def _na_bias_table(rpb, rows):
    kr = min(NA_MAX_ROWS, rows)
    c = np.arange(GRID_W)
    kc = np.arange(GRID_W)
    win_start = np.clip(c - NA_WIN_COLS // 2, 0, GRID_W - NA_WIN_COLS)
    valid = (kc[None, :] >= win_start[:, None]) & (kc[None, :] < win_start[:, None] + NA_WIN_COLS)
    dc_idx = np.clip(kc[None, :] - c[:, None] + NA_WIN_COLS - 1, 0, 2 * NA_WIN_COLS - 2)
    delta = np.arange(kr)
    a = np.arange(kr)
    dr_idx = a[None, :] - delta[:, None] + NA_MAX_ROWS - 1
    row_sel = (dr_idx[:, :, None] == np.arange(rpb.shape[1])).astype(np.float32)
    col_sel = (dc_idx[None, :, :] == np.arange(rpb.shape[2])[:, None, None]).astype(np.float32)
    t = jnp.einsum("dar,hrs,sck->hdcak", row_sel, rpb.astype(F32), col_sel, precision=lax.Precision.HIGHEST)
    t = jnp.where(jnp.asarray(valid)[None, None, :, None, :], t, NEG_BIG)
    h = rpb.shape[0]
    t = t.reshape(h // 2, 2, kr, GRID_W, kr * GRID_W).transpose(0, 2, 1, 3, 4)
    return t.reshape(h // 2, kr, 2 * GRID_W, kr * GRID_W)


def _na_attention(qkv, bias, *, batch, seq, d_model):
    n = batch * seq
    rows = seq // GRID_W
    kr = min(NA_MAX_ROWS, rows)
    pairs = NA_HEADS // 2
    pw = 2 * NA_HEAD_DIM
    assert rows % min(NA_ROW_UNROLL, rows) == 0
    kern = functools.partial(_na_kernel, rows=rows)
    return pl.pallas_call(
        kern,
        grid=(batch, pairs),
        in_specs=[
            pl.BlockSpec((1, kr, 2 * GRID_W, kr * GRID_W), lambda b, hp: (hp, 0, 0, 0)),
            pl.BlockSpec((1, seq, pw), lambda b, hp: (hp, b, 0)),
            pl.BlockSpec((1, seq, pw), lambda b, hp: (pairs + hp, b, 0)),
            pl.BlockSpec((1, seq, pw), lambda b, hp: (2 * pairs + hp, b, 0)),
        ],
        out_specs=pl.BlockSpec((1, seq, pw), lambda b, hp: (hp, b, 0)),
        out_shape=jax.ShapeDtypeStruct((pairs, n, pw), BF16),
        compiler_params=_cparams(("parallel", "parallel")),
        name="na_attention",
    )(bias, qkv, qkv, qkv)


def _out_router_kernel(a_ref, w_ref, h_ref, g_ref, rw_ref, h1_ref, xn_ref, aff_ref):
    a = jnp.concatenate([a_ref[i] for i in range(a_ref.shape[0])], axis=1)
    y = jnp.dot(a, w_ref[...], preferred_element_type=F32) + h_ref[...]
    h1_ref[...] = y
    xn = _rms(y, g_ref[...])
    xn_ref[...] = xn.astype(xn_ref.dtype)
    nt = (((1,), (1,)), ((), ()))
    ne = rw_ref.shape[0]
    xh = xn.astype(BF16)
    xl = (xn - xh.astype(F32)).astype(BF16)
    rw = rw_ref[...]
    rh = rw.astype(BF16)
    rl = (rw - rh.astype(F32)).astype(BF16)
    t1 = lax.dot_general(jnp.concatenate([rh, rl], axis=0), xh, nt, preferred_element_type=F32)
    t2 = lax.dot_general(rh, xl, nt, preferred_element_type=F32)
    logits = t1[0:ne, :] + (t1[ne:2 * ne, :] + t2)
    m = jnp.max(logits, axis=0, keepdims=True)
    e = jnp.exp(logits - m)
    aff = e / jnp.sum(e, axis=0, keepdims=True)
    for t in range(aff_ref.shape[0]):
        aff_ref[t] = aff[:, t * MOE_TILE:(t + 1) * MOE_TILE]


def _out_router(a, w_bf16, h, g, router_w, *, tm=1024):
    nslab, n, sw = a.shape
    kdim = nslab * sw
    tm = min(tm, n)
    d = h.shape[1]
    ne = router_w.shape[1]
    tpb = tm // MOE_TILE
    return pl.pallas_call(
        _out_router_kernel,
        grid=(n // tm,),
        in_specs=[
            pl.BlockSpec((nslab, tm, sw), lambda i: (0, i, 0)),
            pl.BlockSpec((kdim, d), lambda i: (0, 0)),
            pl.BlockSpec((tm, d), lambda i: (i, 0)),
            pl.BlockSpec((1, d), lambda i: (0, 0)),
            pl.BlockSpec((ne, d), lambda i: (0, 0)),
        ],
        out_specs=[
            pl.BlockSpec((tm, d), lambda i: (i, 0)),
            pl.BlockSpec((tm, d), lambda i: (i, 0)),
            pl.BlockSpec((tpb, ne, MOE_TILE), lambda i: (i, 0, 0)),
        ],
        out_shape=[
            jax.ShapeDtypeStruct((n, d), F32),
            jax.ShapeDtypeStruct((n, d), BF16),
            jax.ShapeDtypeStruct((n // MOE_TILE, ne, MOE_TILE), F32),
        ],
        compiler_params=_cparams(("parallel",)),
        name="out_router",
    )(a, w_bf16, h, g.reshape(1, d), router_w.T)


def _select_kernel(aff_ref, pos_ref, off_ref, *, cap):
    ntile, ne, tt = aff_ref.shape

    def count_ge(bits):
        level = pltpu.bitcast(bits, F32)

        def body(c, acc):
            return acc + (aff_ref[c] >= level).astype(I32)
        acc = lax.fori_loop(0, ntile, body, jnp.zeros((ne, tt), I32), unroll=4)
        return jnp.sum(acc, axis=1, keepdims=True)

    def bit_step(i, t):
        cand = t | jnp.left_shift(jnp.int32(1), 30 - i)
        return jnp.where(count_ge(cand) >= cap, cand, t)

    thr_bits = lax.fori_loop(0, 31, bit_step, jnp.zeros((ne, 1), I32))
    thr = pltpu.bitcast(thr_bits, F32)
    above = pltpu.bitcast(thr_bits + 1, F32)
    need_eq = (cap - count_ge(thr_bits + 1)).astype(F32)

    li = lax.broadcasted_iota(I32, (tt, tt), 0)
    lj = lax.broadcasted_iota(I32, (tt, tt), 1)
    upper = (li < lj).astype(BF16)

    def tile_body(c, carry):
        eq_carry, pos_carry = carry
        off_ref[c] = jnp.broadcast_to(pos_carry, (ne, LANES)).astype(I32)
        x = aff_ref[c]
        gt = x >= above
        eq = (x >= thr) & jnp.logical_not(gt)
        gtf = gt.astype(F32)
        eqf = eq.astype(F32)
        pre = jnp.dot(jnp.concatenate([gtf, eqf], axis=0).astype(BF16), upper, preferred_element_type=F32)
        eq_rank = eq_carry + pre[ne:2 * ne, :]
        taken = eq & (eq_rank < need_eq)
        taken_before = jnp.minimum(eq_rank, need_eq) - jnp.minimum(eq_carry, need_eq)
        pos = pos_carry + pre[0:ne, :] + taken_before
        pos_ref[c] = jnp.where(gt | taken, pos.astype(I32), -1)
        eq_next = eq_carry + jnp.sum(eqf, axis=1, keepdims=True)
        taken_total = jnp.minimum(eq_next, need_eq) - jnp.minimum(eq_carry, need_eq)
        return eq_next, pos_carry + jnp.sum(gtf, axis=1, keepdims=True) + taken_total

    lax.fori_loop(0, ntile, tile_body, (jnp.zeros((ne, 1), F32), jnp.zeros((ne, 1), F32)), unroll=2)


def _select(aff3, *, cap):
    ntile, ne, tt = aff3.shape
    kern = functools.partial(_select_kernel, cap=cap)
    pos3, off3 = pl.pallas_call(
        kern,
        out_shape=[
            jax.ShapeDtypeStruct((ntile, ne, tt), I32),
            jax.ShapeDtypeStruct((ntile, ne, LANES), I32),
        ],
        compiler_params=pltpu.CompilerParams(vmem_limit_bytes=_VMEM_LIMIT),
        name="ec_select",
    )(aff3)
    offs = jnp.concatenate([off3[:, :, 0], jnp.full((1, ne), cap, I32)], axis=0).reshape(-1)
    return pos3, offs


def _tile_windows(off_ref, s, ne):
    starts, used = [], []
    for e in range(ne):
        off = off_ref[s * ne + e]
        nxt = off_ref[(s + 1) * ne + e]
        st = (off // BF16_ROWS) * BF16_ROWS
        starts.append(st)
        used.append(nxt - st)
    return starts, used


def _num_rounds(used):
    m = used[0]
    for u in used[1:]:
        m = jnp.maximum(m, u)
    return (m + MOE_SLOTS - 1) // MOE_SLOTS


def _dispatch_kernel(off_ref, x_ref, pos_ref, xe_ref, wins_ref, carry_ref, sem, nout_ref, *, cap):
    s = pl.program_id(0)
    nsub, ne, tt = pos_ref.shape
    gr = BF16_ROWS
    gpr = MOE_SLOTS // gr

    @pl.when(s == 0)
    def _():
        carry_ref[...] = jnp.zeros_like(carry_ref)
        nout_ref[0] = 0
        nout_ref[1] = 0

    slot = lax.broadcasted_iota(I32, (MOE_SLOTS, tt), 0)

    def drain(b):
        def wait_one(i, c):
            pltpu.make_async_copy(wins_ref.at[b, pl.ds(0, gr)], xe_ref.at[pl.ds(0, gr)], sem.at[b]).wait()
            return c
        lax.fori_loop(0, nout_ref[b], wait_one, 0)
        nout_ref[b] = 0

    for k in range(nsub):
        tile = s * nsub + k
        starts, used = _tile_windows(off_ref, tile, ne)
        pos = pos_ref[k]
        x = x_ref[pl.ds(k * tt, tt), :]

        def round_body(w, c, tile=tile, starts=starts, used=used, pos=pos, x=x):
            b = (tile + w) % 2
            drain(b)
            pieces = []
            for e in range(ne):
                key = pos[e:e + 1, :] - (starts[e] + w * MOE_SLOTS)
                pieces.append((jnp.broadcast_to(key, (MOE_SLOTS, tt)) == slot).astype(F32))
            onehot = jnp.concatenate(pieces, axis=0).astype(BF16)
            wins_ref[b] = jnp.dot(onehot, x, preferred_element_type=F32).astype(BF16)
            first = w == 0
            issued = 0
            for e in range(ne):
                head = pl.ds(e * MOE_SLOTS, gr)
                crows = pl.ds(e * gr, gr)
                carry = carry_ref[crows, :]
                merged = (wins_ref[b, head, :].astype(F32) + carry.astype(F32)).astype(BF16)
                wins_ref[b, head, :] = jnp.where(first, merged, wins_ref[b, head, :])

                ngc = used[e] // gr
                rem = used[e] % gr
                ng_w = jnp.clip(ngc - w * gpr, 0, gpr)

                def issue(j, c2, e=e):
                    src = wins_ref.at[b, pl.ds(e * MOE_SLOTS + j * gr, gr)]
                    dst = xe_ref.at[pl.ds(pl.multiple_of(e * cap + starts[e] + w * MOE_SLOTS + j * gr, gr), gr)]
                    pltpu.make_async_copy(src, dst, sem.at[b]).start()
                    return c2

                lax.fori_loop(0, ng_w, issue, 0)
                issued = issued + ng_w

                part = wins_ref[b, pl.ds(e * MOE_SLOTS + jnp.clip(ngc - w * gpr, 0, gpr - 1) * gr, gr), :]
                keep_part = (rem > 0) & (ngc // gpr == w)
                clear = (rem == 0) & first
                carry_ref[crows, :] = jnp.where(keep_part, part, jnp.where(clear, jnp.zeros_like(carry), carry))

            nout_ref[b] = issued
            return c

        lax.fori_loop(0, _num_rounds(used), round_body, 0)

    @pl.when(s == pl.num_programs(0) - 1)
    def _():
        drain(0)
        drain(1)


def _dispatch(offs, xn, pos3, *, cap):
    n, d = xn.shape
    ntile, ne, tt = pos3.shape
    nsub = min(DISPATCH_TILES_PER_STEP, ntile)
    assert ntile % nsub == 0
    kern = functools.partial(_dispatch_kernel, cap=cap)
    return pl.pallas_call(
        kern,
        grid_spec=pltpu.PrefetchScalarGridSpec(
            num_scalar_prefetch=1,
            grid=(ntile // nsub,),
            in_specs=[
                pl.BlockSpec((nsub * tt, d), lambda i, off: (i, 0)),
                pl.BlockSpec((nsub, ne, tt), lambda i, off: (i, 0, 0)),
            ],
            out_specs=pl.BlockSpec(memory_space=pl.ANY),
            scratch_shapes=[
                pltpu.VMEM((2, ne * MOE_SLOTS, d), BF16),
                pltpu.VMEM((ne * BF16_ROWS, d), BF16),
                pltpu.SemaphoreType.DMA((2,)),
                pltpu.SMEM((2,), I32),
            ],
        ),
        out_shape=jax.ShapeDtypeStruct((ne * cap, d), BF16),
        compiler_params=_cparams(("arbitrary",)),
        name="ec_dispatch",
    )(offs, xn, pos3)


def _ffn_kernel(x_ref, wg_ref, wu_ref, wd_ref, o_ref):
    x = x_ref[...]
    a = jnp.dot(x, wg_ref[0, 0], preferred_element_type=F32)
    u = jnp.dot(x, wu_ref[0, 0], preferred_element_type=F32)
    hid = (a * jax.nn.sigmoid(a) * u).astype(BF16)
    o_ref[...] = jnp.dot(hid, wd_ref[0, 0], preferred_element_type=F32).astype(o_ref.dtype)


def _expert_ffn(xe, wg, wu, wd, *, layer, cap, tr=1024):
    _, ne, d, f = wg.shape
    tr = min(tr, cap)
    nt = cap // tr
    return pl.pallas_call(
        _ffn_kernel,
        grid=(ne, nt),
        in_specs=[
            pl.BlockSpec((tr, d), lambda e, t: (e * nt + t, 0)),
            pl.BlockSpec((1, 1, d, f), lambda e, t: (layer, e, 0, 0)),
            pl.BlockSpec((1, 1, d, f), lambda e, t: (layer, e, 0, 0)),
            pl.BlockSpec((1, 1, f, d), lambda e, t: (layer, e, 0, 0)),
        ],
        out_specs=pl.BlockSpec((tr, d), lambda e, t: (e * nt + t, 0)),
        out_shape=jax.ShapeDtypeStruct((ne * cap, d), BF16),
        compiler_params=_cparams(("parallel", "parallel")),
        name="expert_ffn",
    )(xe, wg, wu, wd)


def _combine_kernel(off_ref, h_ref, aff_ref, pos_ref, p_ref, ye_ref, png_ref, wpg_ref, wpu_ref, fng_ref,
                    o_ref, wins_ref, acc_ref, sem, *, cap, final_norm):
    s = pl.program_id(0)
    ntile = pl.num_programs(0)
    nsub, ne, tt = pos_ref.shape
    spill_buf = 2 * nsub

    def window_copies(starts_t, w, buf):
        copies, wstarts = [], []
        for e in range(ne):
            ws = pl.multiple_of(jnp.minimum(starts_t[e] + w * MOE_SLOTS, cap - MOE_SLOTS), BF16_ROWS)
            copies.append(pltpu.make_async_copy(
                ye_ref.at[pl.ds(pl.multiple_of(e * cap + ws, BF16_ROWS), MOE_SLOTS)],
                wins_ref.at[buf, pl.ds(e * MOE_SLOTS, MOE_SLOTS)], sem.at[buf]))
            wstarts.append(ws)
        return copies, wstarts

    cur = s % 2
    tiles = [_tile_windows(off_ref, s * nsub + k, ne) for k in range(nsub)]

    @pl.when(s == 0)
    def _():
        for k in range(nsub):
            for cp in window_copies(tiles[k][0], 0, cur * nsub + k)[0]:
                cp.start()

    @pl.when(s + 1 < ntile)
    def _():
        for k in range(nsub):
            nxt_starts, _ = _tile_windows(off_ref, (s + 1) * nsub + k, ne)
            for cp in window_copies(nxt_starts, 0, (1 - cur) * nsub + k)[0]:
                cp.start()

    slot = lax.broadcasted_iota(I32, (MOE_SLOTS, tt), 0)
    tn = (((0,), (0,)), ((), ()))

    def gates(k, w, wstarts):
        starts = tiles[k][0]
        pos = pos_ref[k]
        aff = aff_ref[k]
        pieces = []
        for e in range(ne):
            base = starts[e] + w * MOE_SLOTS
            pe = pos[e:e + 1, :]
            in_round = (pe >= base) & (pe < base + MOE_SLOTS)
            hit = (jnp.broadcast_to(pe - wstarts[e], (MOE_SLOTS, tt)) == slot) & jnp.broadcast_to(in_round, (MOE_SLOTS, tt))
            pieces.append(jnp.where(hit, jnp.broadcast_to(aff[e:e + 1, :], (MOE_SLOTS, tt)), 0.0))
        return jnp.concatenate(pieces, axis=0).astype(BF16)

    for k in range(nsub):
        starts, used = tiles[k]
        rows = pl.ds(k * tt, tt)
        buf0 = cur * nsub + k
        copies0, wstarts0 = window_copies(starts, 0, buf0)
        a0 = gates(k, 0, wstarts0)
        for cp in copies0:
            cp.wait()
        acc_ref[rows, :] = lax.dot_general(a0, wins_ref[buf0], tn, preferred_element_type=F32)

        def round_body(w, c, k=k, starts=starts, rows=rows):
            copies, wstarts = window_copies(starts, w, spill_buf)
            for cp in copies:
                cp.start()
            a_w = gates(k, w, wstarts)
            for cp in copies:
                cp.wait()
            acc_ref[rows, :] += lax.dot_general(a_w, wins_ref[spill_buf], tn, preferred_element_type=F32)
            return c

        lax.fori_loop(1, _num_rounds(used), round_body, 0)
    h2 = h_ref[...] + acc_ref[...]
    gate = jax.nn.sigmoid(jnp.dot(_rms(h2, png_ref[...]).astype(BF16), wpg_ref[...], preferred_element_type=F32))
    up = jnp.dot(p_ref[0].astype(BF16), wpu_ref[...], preferred_element_type=F32)
    h3 = h2 + up * gate
    if final_norm:
        h3 = _rms(h3, fng_ref[...])
    o_ref[...] = h3


def _combine(offs, h1, aff3, pos3, p_all, ye, ple_norm, wpg, wpu, final_g, *, layer, cap, final_norm):
    n, d = h1.shape
    ntile, ne, tt = pos3.shape
    pd = p_all.shape[2]
    nsub = min(COMBINE_TILES_PER_STEP, ntile)
    assert ntile % nsub == 0
    rows = nsub * tt
    nbuf = 2 * nsub + 1
    kern = functools.partial(_combine_kernel, cap=cap, final_norm=final_norm)
    return pl.pallas_call(
        kern,
        grid_spec=pltpu.PrefetchScalarGridSpec(
            num_scalar_prefetch=1,
            grid=(ntile // nsub,),
            in_specs=[
                pl.BlockSpec((rows, d), lambda i, off: (i, 0)),
                pl.BlockSpec((nsub, ne, tt), lambda i, off: (i, 0, 0)),
                pl.BlockSpec((nsub, ne, tt), lambda i, off: (i, 0, 0)),
                pl.BlockSpec((1, rows, pd), lambda i, off: (layer, i, 0)),
                pl.BlockSpec(memory_space=pl.ANY),
                pl.BlockSpec((1, d), lambda i, off: (0, 0)),
                pl.BlockSpec((d, d), lambda i, off: (0, 0)),
                pl.BlockSpec((pd, d), lambda i, off: (0, 0)),
                pl.BlockSpec((1, d), lambda i, off: (0, 0)),
            ],
            out_specs=pl.BlockSpec((rows, d), lambda i, off: (i, 0)),
            scratch_shapes=[pltpu.VMEM((nbuf, ne * MOE_SLOTS, d), BF16), pltpu.VMEM((rows, d), F32),
                            pltpu.SemaphoreType.DMA((nbuf,))],
        ),
        out_shape=jax.ShapeDtypeStruct((n, d), F32),
        compiler_params=_cparams(("arbitrary",)),
        name="combine_ple",
    )(offs, h1, aff3, pos3, p_all, ye, ple_norm.reshape(1, d), wpg, wpu, final_g.reshape(1, d))


def _moe_block(h1, xn, aff3, p_all, w, i, *, final_norm):
    n, d = h1.shape
    cap = EC_CAPACITY * n // N_EXPERTS
    pos3, offs = _select(aff3, cap=cap)
    xe = _dispatch(offs, xn, pos3, cap=cap)
    ye = _expert_ffn(xe, w["exp_w_gate"], w["exp_w_up"], w["exp_w_down"], layer=i, cap=cap)
    return _combine(offs, h1, aff3, pos3, p_all, ye, w["ple_norm"][i], w["ple_w_gate"][i], w["ple_w_up"][i],
                    w["final_norm"], layer=i, cap=cap, final_norm=final_norm)


def _rope_tables(seq, dim):
    inv = ROPE_THETA ** (-jnp.arange(0, dim, 2, dtype=F32) / dim)
    ang = jnp.arange(seq, dtype=F32)[:, None] * inv[None, :]
    return jnp.cos(ang), jnp.sin(ang)


def _trunk(x, p, w):
    batch, seq, d = x.shape
    n = batch * seq
    h = x.reshape(n, d)
    dk = d // RET_HEADS
    cos, sin = _rope_tables(seq, dk)

    proj = _in_proj(h, w["mix_norm"][0], w["ret_w_in"][0], cos, sin, seq=seq, rope_units=2,
                    unit_scales=(1.0, float(dk) ** -0.5, 1.0, 1.0, 1.0, 1.0), slab_w=dk)
    a = _retention(proj, w["ret_decay_logit"][0], w["ret_gn_w"][0], batch=batch, seq=seq, d_model=d)
    h1, xn, aff3 = _out_router(a, w["ret_w_out"][0], h, w["ffn_norm"][0], w["router_w"][0])
    p_all = p.reshape(p.shape[0], n, p.shape[-1])
    h = _moe_block(h1, xn, aff3, p_all, w, 0, final_norm=False)

    qkv = _in_proj(h, w["mix_norm"][1], w["na_w_in"][0], cos, sin, seq=seq, rope_units=0,
                   unit_scales=(float(NA_HEAD_DIM) ** -0.5, 1.0, 1.0), slab_w=2 * NA_HEAD_DIM)
    a = _na_attention(qkv, _na_bias_table(w["na_rpb"][0], seq // GRID_W), batch=batch, seq=seq, d_model=d)
    h1, xn, aff3 = _out_router(a, w["na_w_out"][0], h, w["ffn_norm"][1], w["router_w"][1])
    y = _moe_block(h1, xn, aff3, p_all, w, 1, final_norm=True)
    return y.reshape(batch, seq, d)


def kernel(x_prompt, x_sample, p_prompt, p_sample, ret_w_in, ret_decay_logit, ret_gn_w, ret_w_out, na_w_in, na_rpb, na_w_out, mix_norm, ffn_norm, ple_norm, router_w, exp_w_gate, exp_w_up, exp_w_down, ple_w_up, ple_w_gate, final_norm):
    w = dict(
        ret_w_in=ret_w_in.astype(BF16), ret_decay_logit=ret_decay_logit, ret_gn_w=ret_gn_w,
        ret_w_out=ret_w_out.astype(BF16), na_w_in=na_w_in.astype(BF16), na_rpb=na_rpb,
        na_w_out=na_w_out.astype(BF16), mix_norm=mix_norm, ffn_norm=ffn_norm, ple_norm=ple_norm,
        router_w=router_w, exp_w_gate=exp_w_gate.astype(BF16), exp_w_up=exp_w_up.astype(BF16),
        exp_w_down=exp_w_down.astype(BF16), ple_w_up=ple_w_up.astype(BF16), ple_w_gate=ple_w_gate.astype(BF16),
        final_norm=final_norm,
    )
    return _trunk(x_prompt, p_prompt, w), _trunk(x_sample, p_sample, w)
```

```python
import functools

import numpy as np
import jax
import jax.numpy as jnp
from jax import lax
from jax.experimental import pallas as pl
from jax.experimental.pallas import tpu as pltpu

F32 = jnp.float32
BF16 = jnp.bfloat16
I32 = jnp.int32

NORM_EPS = 1e-6
ROPE_THETA = 10000.0
GRID_W = 64
RET_HEADS = 4
RET_CHUNK = 256
NA_HEADS = 16
NA_HEAD_DIM = 64
NA_MAX_ROWS = 8
NA_WIN_COLS = 16
NA_ROW_UNROLL = 32
N_EXPERTS = 16
EC_CAPACITY = 2
LANES = 128
BF16_ROWS = 16
MOE_TILE = 256
MOE_SLOTS = 64
DISPATCH_TILES_PER_STEP = 4
COMBINE_TILES_PER_STEP = 2
NEG_BIG = -1e30

V7X_VMEM_BYTES = 64 * 1024 * 1024
_VMEM_LIMIT = V7X_VMEM_BYTES * 7 // 8


def _cparams(sem):
    return pltpu.CompilerParams(dimension_semantics=sem, vmem_limit_bytes=_VMEM_LIMIT)


def _rms(x, g):
    return x * lax.rsqrt(jnp.mean(x * x, axis=-1, keepdims=True) + NORM_EPS) * g


def _in_proj_kernel(x_ref, g_ref, w_ref, cos_ref, sin_ref, o_ref, xn_ref, *, nj, unit_w, rope_units, unit_scales,
                    rope_dim):
    j = pl.program_id(1)

    @pl.when(j == 0)
    def _():
        xn_ref[...] = _rms(x_ref[...], g_ref[...]).astype(BF16)

    upb = w_ref.shape[1] // unit_w
    sw = o_ref.shape[2]
    spu = unit_w // sw
    half = rope_dim // 2

    def rotate(y, c, cos, sin):
        assert sw == rope_dim
        for hh in range(spu):
            lo = hh * rope_dim
            x1 = y[:, lo:lo + half]
            x2 = y[:, lo + half:lo + rope_dim]
            o_ref[c * spu + hh, :, 0:half] = (x1 * cos - x2 * sin).astype(o_ref.dtype)
            o_ref[c * spu + hh, :, half:rope_dim] = (x1 * sin + x2 * cos).astype(o_ref.dtype)

    for c in range(upb):
        base = c * unit_w
        y = jnp.dot(xn_ref[...], w_ref[:, base:base + unit_w], preferred_element_type=F32)
        kinds = [(jj * upb + c < rope_units, unit_scales[jj * upb + c]) for jj in range(nj)]
        if all(k == kinds[0] for k in kinds):
            is_rope, scale = kinds[0]
            if is_rope:
                rotate(y, c, cos_ref[...] * scale, sin_ref[...] * scale)
            else:
                ys = (y if scale == 1.0 else y * scale).astype(o_ref.dtype)
                for sl in range(spu):
                    o_ref[c * spu + sl] = ys[:, sl * sw:(sl + 1) * sw]
        else:
            rope_here = jnp.bool_(False)
            scale = jnp.float32(1.0)
            for jj, (is_rope, sc) in enumerate(kinds):
                rope_here = jnp.where(j == jj, is_rope, rope_here)
                scale = jnp.where(j == jj, jnp.float32(sc), scale)
            rotate(y, c, jnp.where(rope_here, cos_ref[...], 1.0) * scale, jnp.where(rope_here, sin_ref[...], 0.0) * scale)


def _in_proj(x, g, w_bf16, cos, sin, *, seq, rope_units, unit_scales, slab_w, tm=1024, units_per_step=3):
    n, d = x.shape
    tm = min(tm, seq)
    ncol = w_bf16.shape[1]
    tn = units_per_step * d
    nj = ncol // tn
    assert len(unit_scales) == ncol // d
    rope_dim = 2 * cos.shape[1]
    nseq = seq // tm
    kern = functools.partial(_in_proj_kernel, nj=nj, unit_w=d, rope_units=rope_units, unit_scales=unit_scales,
                             rope_dim=rope_dim)
    return pl.pallas_call(
        kern,
        grid=(n // tm, ncol // tn),
        in_specs=[
            pl.BlockSpec((tm, d), lambda i, j: (i, 0)),
            pl.BlockSpec((1, d), lambda i, j: (0, 0)),
            pl.BlockSpec((d, tn), lambda i, j: (0, j)),
            pl.BlockSpec((tm, cos.shape[1]), lambda i, j: (i % nseq, 0)),
            pl.BlockSpec((tm, cos.shape[1]), lambda i, j: (i % nseq, 0)),
        ],
        out_specs=pl.BlockSpec((tn // slab_w, tm, slab_w), lambda i, j: (j, i, 0)),
        out_shape=jax.ShapeDtypeStruct((ncol // slab_w, n, slab_w), BF16),
        scratch_shapes=[pltpu.VMEM((tm, d), BF16)],
        compiler_params=_cparams(("parallel", "arbitrary")),
        name="in_proj",
    )(x, g.reshape(1, d), w_bf16, cos, sin)


def _retention_kernel(dl_ref, q_ref, k_ref, v_ref, g_ref, gnw_ref, o_ref, acc_ref, stf_ref, stb_ref, *, seq, chunk):
    h = pl.program_id(1)
    nc = seq // chunk
    dl = dl_ref[...]
    lg = jnp.minimum(dl, 0.0) - jnp.log1p(jnp.exp(-jnp.abs(dl)))
    col = lax.broadcasted_iota(I32, dl.shape, 1)
    lgh = jnp.sum(jnp.where(col == h, lg, 0.0), axis=1, keepdims=True)
    lgf = lgh[0:1, :]
    lgb = lgh[1:2, :]

    ri = lax.broadcasted_iota(I32, (chunk, chunk), 0)
    ci = lax.broadcasted_iota(I32, (chunk, chunk), 1)
    diff = (ri - ci).astype(F32)
    dmat = jnp.exp(jnp.where(diff >= 0, diff * lgf, -diff * lgb))
    pos = lax.broadcasted_iota(I32, (chunk, 1), 0).astype(F32)
    qdec_f = jnp.exp((pos + 1.0) * lgf)
    kdec_f = jnp.exp((chunk - 1.0 - pos) * lgf)
    qdec_b = jnp.exp((chunk - pos) * lgb)
    kdec_b = jnp.exp(pos * lgb)
    cdec_f = jnp.exp(chunk * lgf)
    cdec_b = jnp.exp(chunk * lgb)
    nt = (((1,), (1,)), ((), ()))
    tn = (((0,), (0,)), ((), ()))

    stf_ref[...] = jnp.zeros_like(stf_ref)
    stb_ref[...] = jnp.zeros_like(stb_ref)

    def chunk_slice(c):
        return pl.ds(pl.multiple_of(c * chunk, chunk), chunk)

    def scaled(x, dec):
        return (x.astype(F32) * dec).astype(BF16)

    def wide(ref, sl):
        return jnp.concatenate([ref[i, sl, :] for i in range(ref.shape[0])], axis=1)

    def step(t):
        slf = chunk_slice(t)
        slb = chunk_slice(nc - 1 - t)
        qf, kf, vf = q_ref[0, slf, :], k_ref[0, slf, :], wide(v_ref, slf)
        qb, kb, vb = q_ref[0, slb, :], k_ref[0, slb, :], wide(v_ref, slb)
        s = lax.dot_general(qf, kf, nt, preferred_element_type=F32) * dmat
        cross_b = jnp.dot(scaled(qb, qdec_b), stb_ref[...].astype(BF16), preferred_element_type=F32)
        inner = jnp.dot(s.astype(BF16), vf, preferred_element_type=F32)
        cross_f = jnp.dot(scaled(qf, qdec_f), stf_ref[...].astype(BF16), preferred_element_type=F32)
        stb_ref[...] = stb_ref[...] * cdec_b + lax.dot_general(scaled(kb, kdec_b), vb, tn, preferred_element_type=F32)
        stf_ref[...] = stf_ref[...] * cdec_f + lax.dot_general(scaled(kf, kdec_f), vf, tn, preferred_element_type=F32)
        return slf, inner + cross_f, slb, cross_b

    def finish(sl, o):
        mu = jnp.mean(o, axis=-1, keepdims=True)
        oc = o - mu
        var = jnp.mean(oc * oc, axis=-1, keepdims=True)
        on = oc * lax.rsqrt(var + NORM_EPS) * gnw_ref[...]
        gate = wide(g_ref, sl).astype(F32)
        o_ref[0, sl, :] = (gate * jax.nn.sigmoid(gate) * on).astype(o_ref.dtype)

    def first_half(t, carry):
        slf, of, slb, ob = step(t)
        acc_ref[slf, :] = of
        acc_ref[slb, :] = ob
        return carry

    def second_half(t, carry):
        slf, of, slb, ob = step(t)
        finish(slf, acc_ref[slf, :] + of)
        finish(slb, acc_ref[slb, :] + ob)
        return carry

    lax.fori_loop(0, nc // 2, first_half, 0, unroll=min(8, nc // 2))
    lax.fori_loop(nc // 2, nc, second_half, 0, unroll=min(8, nc // 2))


def _retention(proj, decay_logit, gn_w, *, batch, seq, d_model):
    heads = RET_HEADS
    dk = d_model // heads
    dv = 2 * d_model // heads
    n = batch * seq
    assert seq % (2 * RET_CHUNK) == 0, "the two scans are paired chunk by chunk"
    kern = functools.partial(_retention_kernel, seq=seq, chunk=RET_CHUNK)
    return pl.pallas_call(
        kern,
        grid=(batch, heads),
        in_specs=[
            pl.BlockSpec((2, heads), lambda b, h: (0, 0)),
            pl.BlockSpec((1, seq, dk), lambda b, h: (h, b, 0)),
            pl.BlockSpec((1, seq, dk), lambda b, h: (heads + h, b, 0)),
            pl.BlockSpec((dv // dk, seq, dk), lambda b, h: (heads + h, b, 0)),
            pl.BlockSpec((dv // dk, seq, dk), lambda b, h: (2 * heads + h, b, 0)),
            pl.BlockSpec((1, dv), lambda b, h: (0, h)),
        ],
        out_specs=pl.BlockSpec((1, seq, dv), lambda b, h: (h, b, 0)),
        out_shape=jax.ShapeDtypeStruct((heads, n, dv), BF16),
        scratch_shapes=[pltpu.VMEM((seq, dv), F32), pltpu.VMEM((dk, dv), F32), pltpu.VMEM((dk, dv), F32)],
        compiler_params=_cparams(("parallel", "parallel")),
        name="retention",
    )(decay_logit, proj, proj, proj, proj, gn_w.reshape(1, -1))


def _na_kernel(bias_ref, q_ref, k_ref, v_ref, o_ref, *, rows):
    gw = GRID_W
    kr = min(NA_MAX_ROWS, rows)
    dh = NA_HEAD_DIM
    lane_q = lax.broadcasted_iota(I32, (gw, 2 * dh), 1)
    nt = (((1,), (1,)), ((), ()))

    group = min(NA_ROW_UNROLL, rows)

    def body(it, carry):
        rr = [it * group + i for i in range(group)]
        rs = [jnp.clip(r - kr // 2, 0, rows - kr) for r in rr]
        s = []
        for r, r0 in zip(rr, rs):
            q2 = q_ref[0, pl.ds(pl.multiple_of(r * gw, gw), gw), :]
            zero = jnp.zeros_like(q2)
            qq = jnp.concatenate([jnp.where(lane_q < dh, q2, zero), jnp.where(lane_q >= dh, q2, zero)], axis=0)
            k2 = k_ref[0, pl.ds(pl.multiple_of(r0 * gw, gw), kr * gw), :]
            s.append(lax.dot_general(qq, k2, nt, preferred_element_type=F32) + bias_ref[0, r - r0])
        m = [jnp.max(si, axis=-1, keepdims=True) for si in s]
        p = [jnp.exp(si - mi) for si, mi in zip(s, m)]
        inv = [1.0 / jnp.sum(pi, axis=-1, keepdims=True) for pi in p]
        o = [jnp.dot(pi.astype(BF16), v_ref[0, pl.ds(pl.multiple_of(r0 * gw, gw), kr * gw), :],
                     preferred_element_type=F32) * ii for pi, ii, r0 in zip(p, inv, rs)]
        for r, oi in zip(rr, o):
            out = jnp.where(lane_q < dh, oi[0:gw, :], oi[gw:2 * gw, :])
            o_ref[0, pl.ds(pl.multiple_of(r * gw, gw), gw), :] = out.astype(o_ref.dtype)
        return carry

    lax.fori_loop(0, rows // group, body, 0)


def _na_bias_table(rpb, rows):
    kr = min(NA_MAX_ROWS, rows)
    c = np.arange(GRID_W)
    kc = np.arange(GRID_W)
    win_start = np.clip(c - NA_WIN_COLS // 2, 0, GRID_W - NA_WIN_COLS)
    valid = (kc[None, :] >= win_start[:, None]) & (kc[None, :] < win_start[:, None] + NA_WIN_COLS)
    dc_idx = np.clip(kc[None, :] - c[:, None] + NA_WIN_COLS - 1, 0, 2 * NA_WIN_COLS - 2)
    delta = np.arange(kr)
    a = np.arange(kr)
    dr_idx = a[None, :] - delta[:, None] + NA_MAX_ROWS - 1
    row_sel = (dr_idx[:, :, None] == np.arange(rpb.shape[1])).astype(np.float32)
    col_sel = (dc_idx[None, :, :] == np.arange(rpb.shape[2])[:, None, None]).astype(np.float32)
    t = jnp.einsum("dar,hrs,sck->hdcak", row_sel, rpb.astype(F32), col_sel, precision=lax.Precision.HIGHEST)
    t = jnp.where(jnp.asarray(valid)[None, None, :, None, :], t, NEG_BIG)
    h = rpb.shape[0]
    t = t.reshape(h // 2, 2, kr, GRID_W, kr * GRID_W).transpose(0, 2, 1, 3, 4)
    return t.reshape(h // 2, kr, 2 * GRID_W, kr * GRID_W)


def _na_attention(qkv, bias, *, batch, seq, d_model):
    n = batch * seq
    rows = seq // GRID_W
    kr = min(NA_MAX_ROWS, rows)
    pairs = NA_HEADS // 2
    pw = 2 * NA_HEAD_DIM
    assert rows % min(NA_ROW_UNROLL, rows) == 0
    kern = functools.partial(_na_kernel, rows=rows)
    return pl.pallas_call(
        kern,
        grid=(batch, pairs),
        in_specs=[
            pl.BlockSpec((1, kr, 2 * GRID_W, kr * GRID_W), lambda b, hp: (hp, 0, 0, 0)),
            pl.BlockSpec((1, seq, pw), lambda b, hp: (hp, b, 0)),
            pl.BlockSpec((1, seq, pw), lambda b, hp: (pairs + hp, b, 0)),
            pl.BlockSpec((1, seq, pw), lambda b, hp: (2 * pairs + hp, b, 0)),
        ],
        out_specs=pl.BlockSpec((1, seq, pw), lambda b, hp: (hp, b, 0)),
        out_shape=jax.ShapeDtypeStruct((pairs, n, pw), BF16),
        compiler_params=_cparams(("parallel", "parallel")),
        name="na_attention",
    )(bias, qkv, qkv, qkv)


def _out_router_kernel(a_ref, w_ref, h_ref, g_ref, rw_ref, h1_ref, xn_ref, aff_ref):
    a = jnp.concatenate([a_ref[i] for i in range(a_ref.shape[0])], axis=1)
    y = jnp.dot(a, w_ref[...], preferred_element_type=F32) + h_ref[...]
    h1_ref[...] = y
    xn = _rms(y, g_ref[...])
    xn_ref[...] = xn.astype(xn_ref.dtype)
    nt = (((1,), (1,)), ((), ()))
    ne = rw_ref.shape[0]
    xh = xn.astype(BF16)
    xl = (xn - xh.astype(F32)).astype(BF16)
    rw = rw_ref[...]
    rh = rw.astype(BF16)
    rl = (rw - rh.astype(F32)).astype(BF16)
    t1 = lax.dot_general(jnp.concatenate([rh, rl], axis=0), xh, nt, preferred_element_type=F32)
    t2 = lax.dot_general(rh, xl, nt, preferred_element_type=F32)
    logits = t1[0:ne, :] + (t1[ne:2 * ne, :] + t2)
    m = jnp.max(logits, axis=0, keepdims=True)
    e = jnp.exp(logits - m)
    aff = e / jnp.sum(e, axis=0, keepdims=True)
    for t in range(aff_ref.shape[0]):
        aff_ref[t] = aff[:, t * MOE_TILE:(t + 1) * MOE_TILE]


def _out_router(a, w_bf16, h, g, router_w, *, tm=1024):
    nslab, n, sw = a.shape
    kdim = nslab * sw
    tm = min(tm, n)
    d = h.shape[1]
    ne = router_w.shape[1]
    tpb = tm // MOE_TILE
    return pl.pallas_call(
        _out_router_kernel,
        grid=(n // tm,),
        in_specs=[
            pl.BlockSpec((nslab, tm, sw), lambda i: (0, i, 0)),
            pl.BlockSpec((kdim, d), lambda i: (0, 0)),
            pl.BlockSpec((tm, d), lambda i: (i, 0)),
            pl.BlockSpec((1, d), lambda i: (0, 0)),
            pl.BlockSpec((ne, d), lambda i: (0, 0)),
        ],
        out_specs=[
            pl.BlockSpec((tm, d), lambda i: (i, 0)),
            pl.BlockSpec((tm, d), lambda i: (i, 0)),
            pl.BlockSpec((tpb, ne, MOE_TILE), lambda i: (i, 0, 0)),
        ],
        out_shape=[
            jax.ShapeDtypeStruct((n, d), F32),
            jax.ShapeDtypeStruct((n, d), BF16),
            jax.ShapeDtypeStruct((n // MOE_TILE, ne, MOE_TILE), F32),
        ],
        compiler_params=_cparams(("parallel",)),
        name="out_router",
    )(a, w_bf16, h, g.reshape(1, d), router_w.T)


def _select_kernel(aff_ref, pos_ref, off_ref, *, cap):
    ntile, ne, tt = aff_ref.shape

    def count_ge(bits):
        level = pltpu.bitcast(bits, F32)

        def body(c, acc):
            return acc + (aff_ref[c] >= level).astype(I32)
        acc = lax.fori_loop(0, ntile, body, jnp.zeros((ne, tt), I32), unroll=4)
        return jnp.sum(acc, axis=1, keepdims=True)

    def bit_step(i, t):
        cand = t | jnp.left_shift(jnp.int32(1), 30 - i)
        return jnp.where(count_ge(cand) >= cap, cand, t)

    thr_bits = lax.fori_loop(0, 31, bit_step, jnp.zeros((ne, 1), I32))
    thr = pltpu.bitcast(thr_bits, F32)
    above = pltpu.bitcast(thr_bits + 1, F32)
    need_eq = (cap - count_ge(thr_bits + 1)).astype(F32)

    li = lax.broadcasted_iota(I32, (tt, tt), 0)
    lj = lax.broadcasted_iota(I32, (tt, tt), 1)
    upper = (li < lj).astype(BF16)

    def tile_body(c, carry):
        eq_carry, pos_carry = carry
        off_ref[c] = jnp.broadcast_to(pos_carry, (ne, LANES)).astype(I32)
        x = aff_ref[c]
        gt = x >= above
        eq = (x >= thr) & jnp.logical_not(gt)
        gtf = gt.astype(F32)
        eqf = eq.astype(F32)
        pre = jnp.dot(jnp.concatenate([gtf, eqf], axis=0).astype(BF16), upper, preferred_element_type=F32)
        eq_rank = eq_carry + pre[ne:2 * ne, :]
        taken = eq & (eq_rank < need_eq)
        taken_before = jnp.minimum(eq_rank, need_eq) - jnp.minimum(eq_carry, need_eq)
        pos = pos_carry + pre[0:ne, :] + taken_before
        pos_ref[c] = jnp.where(gt | taken, pos.astype(I32), -1)
        eq_next = eq_carry + jnp.sum(eqf, axis=1, keepdims=True)
        taken_total = jnp.minimum(eq_next, need_eq) - jnp.minimum(eq_carry, need_eq)
        return eq_next, pos_carry + jnp.sum(gtf, axis=1, keepdims=True) + taken_total

    lax.fori_loop(0, ntile, tile_body, (jnp.zeros((ne, 1), F32), jnp.zeros((ne, 1), F32)), unroll=2)


def _select(aff3, *, cap):
    ntile, ne, tt = aff3.shape
    kern = functools.partial(_select_kernel, cap=cap)
    pos3, off3 = pl.pallas_call(
        kern,
        out_shape=[
            jax.ShapeDtypeStruct((ntile, ne, tt), I32),
            jax.ShapeDtypeStruct((ntile, ne, LANES), I32),
        ],
        compiler_params=pltpu.CompilerParams(vmem_limit_bytes=_VMEM_LIMIT),
        name="ec_select",
    )(aff3)
    offs = jnp.concatenate([off3[:, :, 0], jnp.full((1, ne), cap, I32)], axis=0).reshape(-1)
    return pos3, offs


def _tile_windows(off_ref, s, ne):
    starts, used = [], []
    for e in range(ne):
        off = off_ref[s * ne + e]
        nxt = off_ref[(s + 1) * ne + e]
        st = (off // BF16_ROWS) * BF16_ROWS
        starts.append(st)
        used.append(nxt - st)
    return starts, used


def _num_rounds(used):
    m = used[0]
    for u in used[1:]:
        m = jnp.maximum(m, u)
    return (m + MOE_SLOTS - 1) // MOE_SLOTS


def _dispatch_kernel(off_ref, x_ref, pos_ref, xe_ref, wins_ref, carry_ref, sem, nout_ref, *, cap):
    s = pl.program_id(0)
    nsub, ne, tt = pos_ref.shape
    gr = BF16_ROWS
    gpr = MOE_SLOTS // gr

    @pl.when(s == 0)
    def _():
        carry_ref[...] = jnp.zeros_like(carry_ref)
        nout_ref[0] = 0
        nout_ref[1] = 0

    slot = lax.broadcasted_iota(I32, (MOE_SLOTS, tt), 0)

    def drain(b):
        def wait_one(i, c):
            pltpu.make_async_copy(wins_ref.at[b, pl.ds(0, gr)], xe_ref.at[pl.ds(0, gr)], sem.at[b]).wait()
            return c
        lax.fori_loop(0, nout_ref[b], wait_one, 0)
        nout_ref[b] = 0

    for k in range(nsub):
        tile = s * nsub + k
        starts, used = _tile_windows(off_ref, tile, ne)
        pos = pos_ref[k]
        x = x_ref[pl.ds(k * tt, tt), :]

        def round_body(w, c, tile=tile, starts=starts, used=used, pos=pos, x=x):
            b = (tile + w) % 2
            drain(b)
            pieces = []
            for e in range(ne):
                key = pos[e:e + 1, :] - (starts[e] + w * MOE_SLOTS)
                pieces.append((jnp.broadcast_to(key, (MOE_SLOTS, tt)) == slot).astype(F32))
            onehot = jnp.concatenate(pieces, axis=0).astype(BF16)
            wins_ref[b] = jnp.dot(onehot, x, preferred_element_type=F32).astype(BF16)
            first = w == 0
            issued = 0
            for e in range(ne):
                head = pl.ds(e * MOE_SLOTS, gr)
                crows = pl.ds(e * gr, gr)
                carry = carry_ref[crows, :]
                merged = (wins_ref[b, head, :].astype(F32) + carry.astype(F32)).astype(BF16)
                wins_ref[b, head, :] = jnp.where(first, merged, wins_ref[b, head, :])

                ngc = used[e] // gr
                rem = used[e] % gr
                ng_w = jnp.clip(ngc - w * gpr, 0, gpr)

                def issue(j, c2, e=e):
                    src = wins_ref.at[b, pl.ds(e * MOE_SLOTS + j * gr, gr)]
                    dst = xe_ref.at[pl.ds(pl.multiple_of(e * cap + starts[e] + w * MOE_SLOTS + j * gr, gr), gr)]
                    pltpu.make_async_copy(src, dst, sem.at[b]).start()
                    return c2

                lax.fori_loop(0, ng_w, issue, 0)
                issued = issued + ng_w

                part = wins_ref[b, pl.ds(e * MOE_SLOTS + jnp.clip(ngc - w * gpr, 0, gpr - 1) * gr, gr), :]
                keep_part = (rem > 0) & (ngc // gpr == w)
                clear = (rem == 0) & first
                carry_ref[crows, :] = jnp.where(keep_part, part, jnp.where(clear, jnp.zeros_like(carry), carry))

            nout_ref[b] = issued
            return c

        lax.fori_loop(0, _num_rounds(used), round_body, 0)

    @pl.when(s == pl.num_programs(0) - 1)
    def _():
        drain(0)
        drain(1)


def _dispatch(offs, xn, pos3, *, cap):
    n, d = xn.shape
    ntile, ne, tt = pos3.shape
    nsub = min(DISPATCH_TILES_PER_STEP, ntile)
    assert ntile % nsub == 0
    kern = functools.partial(_dispatch_kernel, cap=cap)
    return pl.pallas_call(
        kern,
        grid_spec=pltpu.PrefetchScalarGridSpec(
            num_scalar_prefetch=1,
            grid=(ntile // nsub,),
            in_specs=[
                pl.BlockSpec((nsub * tt, d), lambda i, off: (i, 0)),
                pl.BlockSpec((nsub, ne, tt), lambda i, off: (i, 0, 0)),
            ],
            out_specs=pl.BlockSpec(memory_space=pl.ANY),
            scratch_shapes=[
                pltpu.VMEM((2, ne * MOE_SLOTS, d), BF16),
                pltpu.VMEM((ne * BF16_ROWS, d), BF16),
                pltpu.SemaphoreType.DMA((2,)),
                pltpu.SMEM((2,), I32),
            ],
        ),
        out_shape=jax.ShapeDtypeStruct((ne * cap, d), BF16),
        compiler_params=_cparams(("arbitrary",)),
        name="ec_dispatch",
    )(offs, xn, pos3)


def _ffn_kernel(x_ref, wg_ref, wu_ref, wd_ref, o_ref):
    x = x_ref[...]
    a = jnp.dot(x, wg_ref[0, 0], preferred_element_type=F32)
    u = jnp.dot(x, wu_ref[0, 0], preferred_element_type=F32)
    hid = (a * jax.nn.sigmoid(a) * u).astype(BF16)
    o_ref[...] = jnp.dot(hid, wd_ref[0, 0], preferred_element_type=F32).astype(o_ref.dtype)


def _expert_ffn(xe, wg, wu, wd, *, layer, cap, tr=1024):
    _, ne, d, f = wg.shape
    tr = min(tr, cap)
    nt = cap // tr
    return pl.pallas_call(
        _ffn_kernel,
        grid=(ne, nt),
        in_specs=[
            pl.BlockSpec((tr, d), lambda e, t: (e * nt + t, 0)),
            pl.BlockSpec((1, 1, d, f), lambda e, t: (layer, e, 0, 0)),
            pl.BlockSpec((1, 1, d, f), lambda e, t: (layer, e, 0, 0)),
            pl.BlockSpec((1, 1, f, d), lambda e, t: (layer, e, 0, 0)),
        ],
        out_specs=pl.BlockSpec((tr, d), lambda e, t: (e * nt + t, 0)),
        out_shape=jax.ShapeDtypeStruct((ne * cap, d), BF16),
        compiler_params=_cparams(("parallel", "parallel")),
        name="expert_ffn",
    )(xe, wg, wu, wd)


def _combine_kernel(off_ref, h_ref, aff_ref, pos_ref, p_ref, ye_ref, png_ref, wpg_ref, wpu_ref, fng_ref,
                    o_ref, wins_ref, acc_ref, sem, *, cap, final_norm):
    s = pl.program_id(0)
    ntile = pl.num_programs(0)
    nsub, ne, tt = pos_ref.shape
    spill_buf = 2 * nsub

    def window_copies(starts_t, w, buf):
        copies, wstarts = [], []
        for e in range(ne):
            ws = pl.multiple_of(jnp.minimum(starts_t[e] + w * MOE_SLOTS, cap - MOE_SLOTS), BF16_ROWS)
            copies.append(pltpu.make_async_copy(
                ye_ref.at[pl.ds(pl.multiple_of(e * cap + ws, BF16_ROWS), MOE_SLOTS)],
                wins_ref.at[buf, pl.ds(e * MOE_SLOTS, MOE_SLOTS)], sem.at[buf]))
            wstarts.append(ws)
        return copies, wstarts

    cur = s % 2
    tiles = [_tile_windows(off_ref, s * nsub + k, ne) for k in range(nsub)]

    @pl.when(s == 0)
    def _():
        for k in range(nsub):
            for cp in window_copies(tiles[k][0], 0, cur * nsub + k)[0]:
                cp.start()

    @pl.when(s + 1 < ntile)
    def _():
        for k in range(nsub):
            nxt_starts, _ = _tile_windows(off_ref, (s + 1) * nsub + k, ne)
            for cp in window_copies(nxt_starts, 0, (1 - cur) * nsub + k)[0]:
                cp.start()

    slot = lax.broadcasted_iota(I32, (MOE_SLOTS, tt), 0)
    tn = (((0,), (0,)), ((), ()))

    def gates(k, w, wstarts):
        starts = tiles[k][0]
        pos = pos_ref[k]
        aff = aff_ref[k]
        pieces = []
        for e in range(ne):
            base = starts[e] + w * MOE_SLOTS
            pe = pos[e:e + 1, :]
            in_round = (pe >= base) & (pe < base + MOE_SLOTS)
            hit = (jnp.broadcast_to(pe - wstarts[e], (MOE_SLOTS, tt)) == slot) & jnp.broadcast_to(in_round, (MOE_SLOTS, tt))
            pieces.append(jnp.where(hit, jnp.broadcast_to(aff[e:e + 1, :], (MOE_SLOTS, tt)), 0.0))
        return jnp.concatenate(pieces, axis=0).astype(BF16)

    for k in range(nsub):
        starts, used = tiles[k]
        rows = pl.ds(k * tt, tt)
        buf0 = cur * nsub + k
        copies0, wstarts0 = window_copies(starts, 0, buf0)
        a0 = gates(k, 0, wstarts0)
        for cp in copies0:
            cp.wait()
        acc_ref[rows, :] = lax.dot_general(a0, wins_ref[buf0], tn, preferred_element_type=F32)

        def round_body(w, c, k=k, starts=starts, rows=rows):
            copies, wstarts = window_copies(starts, w, spill_buf)
            for cp in copies:
                cp.start()
            a_w = gates(k, w, wstarts)
            for cp in copies:
                cp.wait()
            acc_ref[rows, :] += lax.dot_general(a_w, wins_ref[spill_buf], tn, preferred_element_type=F32)
            return c

        lax.fori_loop(1, _num_rounds(used), round_body, 0)
    h2 = h_ref[...] + acc_ref[...]
    gate = jax.nn.sigmoid(jnp.dot(_rms(h2, png_ref[...]).astype(BF16), wpg_ref[...], preferred_element_type=F32))
    up = jnp.dot(p_ref[0].astype(BF16), wpu_ref[...], preferred_element_type=F32)
    h3 = h2 + up * gate
    if final_norm:
        h3 = _rms(h3, fng_ref[...])
    o_ref[...] = h3


def _combine(offs, h1, aff3, pos3, p_all, ye, ple_norm, wpg, wpu, final_g, *, layer, cap, final_norm):
    n, d = h1.shape
    ntile, ne, tt = pos3.shape
    pd = p_all.shape[2]
    nsub = min(COMBINE_TILES_PER_STEP, ntile)
    assert ntile % nsub == 0
    rows = nsub * tt
    nbuf = 2 * nsub + 1
    kern = functools.partial(_combine_kernel, cap=cap, final_norm=final_norm)
    return pl.pallas_call(
        kern,
        grid_spec=pltpu.PrefetchScalarGridSpec(
            num_scalar_prefetch=1,
            grid=(ntile // nsub,),
            in_specs=[
                pl.BlockSpec((rows, d), lambda i, off: (i, 0)),
                pl.BlockSpec((nsub, ne, tt), lambda i, off: (i, 0, 0)),
                pl.BlockSpec((nsub, ne, tt), lambda i, off: (i, 0, 0)),
                pl.BlockSpec((1, rows, pd), lambda i, off: (layer, i, 0)),
                pl.BlockSpec(memory_space=pl.ANY),
                pl.BlockSpec((1, d), lambda i, off: (0, 0)),
                pl.BlockSpec((d, d), lambda i, off: (0, 0)),
                pl.BlockSpec((pd, d), lambda i, off: (0, 0)),
                pl.BlockSpec((1, d), lambda i, off: (0, 0)),
            ],
            out_specs=pl.BlockSpec((rows, d), lambda i, off: (i, 0)),
            scratch_shapes=[pltpu.VMEM((nbuf, ne * MOE_SLOTS, d), BF16), pltpu.VMEM((rows, d), F32),
                            pltpu.SemaphoreType.DMA((nbuf,))],
        ),
        out_shape=jax.ShapeDtypeStruct((n, d), F32),
        compiler_params=_cparams(("arbitrary",)),
        name="combine_ple",
    )(offs, h1, aff3, pos3, p_all, ye, ple_norm.reshape(1, d), wpg, wpu, final_g.reshape(1, d))


def _moe_block(h1, xn, aff3, p_all, w, i, *, final_norm):
    n, d = h1.shape
    cap = EC_CAPACITY * n // N_EXPERTS
    pos3, offs = _select(aff3, cap=cap)
    xe = _dispatch(offs, xn, pos3, cap=cap)
    ye = _expert_ffn(xe, w["exp_w_gate"], w["exp_w_up"], w["exp_w_down"], layer=i, cap=cap)
    return _combine(offs, h1, aff3, pos3, p_all, ye, w["ple_norm"][i], w["ple_w_gate"][i], w["ple_w_up"][i],
                    w["final_norm"], layer=i, cap=cap, final_norm=final_norm)


def _rope_tables(seq, dim):
    inv = ROPE_THETA ** (-jnp.arange(0, dim, 2, dtype=F32) / dim)
    ang = jnp.arange(seq, dtype=F32)[:, None] * inv[None, :]
    return jnp.cos(ang), jnp.sin(ang)


def _trunk(x, p, w):
    batch, seq, d = x.shape
    n = batch * seq
    h = x.reshape(n, d)
    dk = d // RET_HEADS
    cos, sin = _rope_tables(seq, dk)

    proj = _in_proj(h, w["mix_norm"][0], w["ret_w_in"][0], cos, sin, seq=seq, rope_units=2,
                    unit_scales=(1.0, float(dk) ** -0.5, 1.0, 1.0, 1.0, 1.0), slab_w=dk)
    a = _retention(proj, w["ret_decay_logit"][0], w["ret_gn_w"][0], batch=batch, seq=seq, d_model=d)
    h1, xn, aff3 = _out_router(a, w["ret_w_out"][0], h, w["ffn_norm"][0], w["router_w"][0])
    p_all = p.reshape(p.shape[0], n, p.shape[-1])
    h = _moe_block(h1, xn, aff3, p_all, w, 0, final_norm=False)

    qkv = _in_proj(h, w["mix_norm"][1], w["na_w_in"][0], cos, sin, seq=seq, rope_units=0,
                   unit_scales=(float(NA_HEAD_DIM) ** -0.5, 1.0, 1.0), slab_w=2 * NA_HEAD_DIM)
    a = _na_attention(qkv, _na_bias_table(w["na_rpb"][0], seq // GRID_W), batch=batch, seq=seq, d_model=d)
    h1, xn, aff3 = _out_router(a, w["na_w_out"][0], h, w["ffn_norm"][1], w["router_w"][1])
    y = _moe_block(h1, xn, aff3, p_all, w, 1, final_norm=True)
    return y.reshape(batch, seq, d)


def kernel(x_prompt, x_sample, p_prompt, p_sample, ret_w_in, ret_decay_logit, ret_gn_w, ret_w_out, na_w_in, na_rpb, na_w_out, mix_norm, ffn_norm, ple_norm, router_w, exp_w_gate, exp_w_up, exp_w_down, ple_w_up, ple_w_gate, final_norm):
    w = dict(
        ret_w_in=ret_w_in.astype(BF16), ret_decay_logit=ret_decay_logit, ret_gn_w=ret_gn_w,
        ret_w_out=ret_w_out.astype(BF16), na_w_in=na_w_in.astype(BF16), na_rpb=na_rpb,
        na_w_out=na_w_out.astype(BF16), mix_norm=mix_norm, ffn_norm=ffn_norm, ple_norm=ple_norm,
        router_w=router_w, exp_w_gate=exp_w_gate.astype(BF16), exp_w_up=exp_w_up.astype(BF16),
        exp_w_down=exp_w_down.astype(BF16), ple_w_up=ple_w_up.astype(BF16), ple_w_gate=ple_w_gate.astype(BF16),
        final_norm=final_norm,
    )
    return _trunk(x_prompt, p_prompt, w), _trunk(x_sample, p_sample, w)
```
